```python
import math
import jax, jax.numpy as jnp
from jax import lax
import numpy as np

D_MODEL = 1024
BATCH = 8
SEQ = 2048
DEPTH = 2

CHUNK = 64
Q_BLOCK = 128
EPS = 1e-6
N_EVEN = (DEPTH + 1) // 2
N_ODD = DEPTH // 2

A_HEADS = 8
A_NOPE = 64
A_ROPE = 32
A_V = 64
A_Q_LORA = 256
A_KV_LORA = 128
A_THETA = 10000.0
B_HEADS = 8
B_DH = 64
B_LEFT_CHUNKS = 8
B_BAND = B_LEFT_CHUNKS + 1
B_REL_CLIP = 128
EVEN_IN = A_Q_LORA + A_KV_LORA + A_ROPE + 3 * B_HEADS * B_DH
EVEN_MIX = A_HEADS * A_V + B_HEADS * B_DH
C_HEADS = 8
C_DH = 64
C_ROPE = C_DH // 4
ROPE_THETA = 500000.0
C_MIX = C_HEADS * 2 * C_DH
FF_DENSE = 2752
N_EXPERTS = 8
TOP_K = 2
FF_EXPERT = 3584

kernel_name = 'hybrid_mla_band_diffattn_moe_trunk'


def rmsnorm(x, g):
    xf = x.astype(jnp.float32)
    y = xf * lax.rsqrt(jnp.mean(xf * xf, axis=-1, keepdims=True) + EPS)
    return y.astype(x.dtype) * g


def rope_tables(positions, dim, theta):
    inv = 1.0 / (theta ** (jnp.arange(0, dim, 2, dtype=jnp.float32) / dim))
    ang = positions.astype(jnp.float32)[..., None] * inv
    return jnp.cos(ang), jnp.sin(ang)


def apply_rope(x, cos, sin):
    shape = (cos.shape[0],) + (1,) * (x.ndim - 3) + cos.shape[1:]
    c = cos.reshape(shape)
    s = sin.reshape(shape)
    x1, x2 = jnp.split(x.astype(jnp.float32), 2, axis=-1)
    return jnp.concatenate([x1 * c - x2 * s, x2 * c + x1 * s], axis=-1).astype(x.dtype)


def partial_rope(x, cos, sin, rot_dim):
    return jnp.concatenate([apply_rope(x[..., :rot_dim], cos, sin), x[..., rot_dim:]], axis=-1)


def split_heads(t, n_heads):
    b, s, _ = t.shape
    return t.reshape(b, s, n_heads, -1).transpose(0, 2, 1, 3)


def chunk_causal_softmax(scores, q0):
    nq, nk = scores.shape[-2:]
    q_chunk = (q0 + jnp.arange(nq)) // CHUNK
    k_chunk = jnp.arange(nk) // CHUNK
    mask = k_chunk[None, :] <= q_chunk[:, None]
    return jax.nn.softmax(jnp.where(mask, scores, -jnp.inf), axis=-1)


def mla_attention(c_q, c_kv, k_rope, cos, sin, g_cq, g_ckv, w_uq, w_ukv, g_qn, g_kn):
    q = jnp.einsum('bsr,rhd->bhsd', rmsnorm(c_q, g_cq), w_uq)
    kv = jnp.einsum('bsr,rhd->bhsd', rmsnorm(c_kv, g_ckv), w_ukv)
    q_nope = rmsnorm(q[..., :A_NOPE], g_qn[:A_NOPE])
    q_rot = apply_rope(rmsnorm(q[..., A_NOPE:], g_qn[A_NOPE:]), cos, sin)
    k_nope = rmsnorm(kv[..., :A_NOPE], g_kn[:A_NOPE])
    v = kv[..., A_NOPE:]
    k_rot = apply_rope(rmsnorm(k_rope, g_kn[A_NOPE:]), cos, sin)
    scale = (A_NOPE + A_ROPE) ** -0.5
    outs = []
    for i in range(q.shape[2] // Q_BLOCK):
        q0, k_end = i * Q_BLOCK, (i + 1) * Q_BLOCK
        s = (jnp.einsum('bhqd,bhkd->bhqk', q_nope[:, :, q0:k_end], k_nope[:, :, :k_end],
                        preferred_element_type=jnp.float32)
             + jnp.einsum('bhqd,bkd->bhqk', q_rot[:, :, q0:k_end], k_rot[:, :k_end],
                          preferred_element_type=jnp.float32)) * scale
        p = chunk_causal_softmax(s, q0)
        outs.append(jnp.einsum('bhqk,bhkd->bhqd', p.astype(v.dtype), v[:, :, :k_end]))
    return jnp.concatenate(outs, axis=2)


def band_attention(q, k, v, rel_table):
    b, h, s_len, dh = q.shape
    nc = s_len // CHUNK
    qc = q.reshape(b, h, nc, CHUNK, dh)
    pad = ((0, 0), (0, 0), (B_LEFT_CHUNKS, 0), (0, 0), (0, 0))
    kp = jnp.pad(k.reshape(b, h, nc, CHUNK, dh), pad)
    vp = jnp.pad(v.reshape(b, h, nc, CHUNK, dh), pad)
    k_band = jnp.concatenate([kp[:, :, j:j + nc] for j in range(B_BAND)], axis=3)
    v_band = jnp.concatenate([vp[:, :, j:j + nc] for j in range(B_BAND)], axis=3)
    qi = np.arange(CHUNK)[:, None]
    km = np.arange(B_BAND * CHUNK)[None, :]
    rel = np.clip(B_LEFT_CHUNKS * CHUNK + qi - km, -B_REL_CLIP, B_REL_CLIP) + B_REL_CLIP
    bias = rel_table[:, rel].astype(jnp.float32)
    s = jnp.einsum('bhcqd,bhckd->bhcqk', qc, k_band, preferred_element_type=jnp.float32) * dh ** -0.5
    s = s + bias[None, :, None]
    key_chunk = np.arange(nc)[:, None] - B_LEFT_CHUNKS + (np.arange(B_BAND * CHUNK) // CHUNK)[None, :]
    valid = key_chunk >= 0
    p = jax.nn.softmax(jnp.where(valid[:, None, :], s, -jnp.inf), axis=-1)
    out = jnp.einsum('bhcqk,bhckd->bhcqd', p.astype(v.dtype), v_band)
    return out.reshape(b, h, s_len, dh)


def diff_attention(q, k, v, cos, sin, g_qn, g_kn, lam_q1, lam_k1, lam_q2, lam_k2, g_sub, lam_init):
    q = partial_rope(rmsnorm(q, g_qn), cos, sin, C_ROPE)
    k = partial_rope(rmsnorm(k, g_kn), cos, sin, C_ROPE)
    lam = (jnp.exp(jnp.sum((lam_q1 * lam_k1).astype(jnp.float32)))
           - jnp.exp(jnp.sum((lam_q2 * lam_k2).astype(jnp.float32))) + lam_init)
    scale = C_DH ** -0.5
    outs = []
    for i in range(q.shape[3] // Q_BLOCK):
        q0, k_end = i * Q_BLOCK, (i + 1) * Q_BLOCK
        s = jnp.einsum('bhmqd,bhmkd->bhmqk', q[:, :, :, q0:k_end], k[:, :, :, :k_end],
                       preferred_element_type=jnp.float32) * scale
        p = chunk_causal_softmax(s, q0)
        a = p[:, :, 0] - lam * p[:, :, 1]
        outs.append(jnp.einsum('bhqk,bhkd->bhqd', a.astype(v.dtype), v[:, :, :k_end]))
    o = jnp.concatenate(outs, axis=2)
    return rmsnorm(o, g_sub) * (1.0 - lam_init)


def swiglu(h, w_gate, w_up, w_down):
    return (jax.nn.silu(h @ w_gate) * (h @ w_up)) @ w_down


def moe_swiglu(h, w_router, w_gate, w_up, w_down):
    b, s_len, d = h.shape
    t = h.reshape(-1, d)
    logits = jnp.dot(t, w_router, preferred_element_type=jnp.float32)
    top_val, top_idx = lax.top_k(logits, TOP_K)
    gates = jax.nn.softmax(top_val, axis=-1)
    combine = jnp.sum(jax.nn.one_hot(top_idx, N_EXPERTS, dtype=jnp.float32) * gates[..., None], axis=1)
    out = jnp.zeros_like(t)
    for e in range(N_EXPERTS):
        out = out + combine[:, e:e + 1].astype(t.dtype) * swiglu(t, w_gate[e], w_up[e], w_down[e])
    return out.reshape(b, s_len, d)


def setup_inputs(seed: int = 0) -> dict:
    key = jax.random.key(seed)
    ks = iter(jax.random.split(key, 40))
    nrm = lambda shape, scale: jax.random.normal(next(ks), shape, jnp.float32) * scale
    gain = lambda shape: 1.0 + nrm(shape, 0.1)
    d = D_MODEL
    x = nrm((BATCH, SEQ, d), 1.0)
    positions = (jax.random.randint(next(ks), (BATCH, 1), 0, 8192, dtype=jnp.int32)
                 + jnp.arange(SEQ, dtype=jnp.int32)[None, :])
    return {
        'x': x,
        'positions': positions,
        'ev_norm_mix': gain((N_EVEN, d)),
        'ev_w_in': nrm((N_EVEN, d, EVEN_IN), d ** -0.5),
        'ev_g_cq': gain((N_EVEN, A_Q_LORA)),
        'ev_g_ckv': gain((N_EVEN, A_KV_LORA)),
        'ev_w_uq': nrm((N_EVEN, A_Q_LORA, A_HEADS, A_NOPE + A_ROPE), A_Q_LORA ** -0.5),
        'ev_w_ukv': nrm((N_EVEN, A_KV_LORA, A_HEADS, A_NOPE + A_V), A_KV_LORA ** -0.5),
        'ev_a_qnorm': gain((N_EVEN, A_NOPE + A_ROPE)),
        'ev_a_knorm': gain((N_EVEN, A_NOPE + A_ROPE)),
        'ev_b_qnorm': gain((N_EVEN, B_DH)),
        'ev_b_knorm': gain((N_EVEN, B_DH)),
        'ev_b_rel_bias': nrm((N_EVEN, B_HEADS, 2 * B_REL_CLIP + 1), 0.5),
        'ev_w_out': nrm((N_EVEN, EVEN_MIX, d), EVEN_MIX ** -0.5),
        'ev_norm_ff': gain((N_EVEN, d)),
        'ev_ff_gate': nrm((N_EVEN, d, FF_DENSE), d ** -0.5),
        'ev_ff_up': nrm((N_EVEN, d, FF_DENSE), d ** -0.5),
        'ev_ff_down': nrm((N_EVEN, FF_DENSE, d), FF_DENSE ** -0.5),
        'od_norm_mix': gain((N_ODD, d)),
        'od_w_qkv': nrm((N_ODD, d, 3 * C_MIX), d ** -0.5),
        'od_c_qnorm': gain((N_ODD, C_DH)),
        'od_c_knorm': gain((N_ODD, C_DH)),
        'od_lam_q1': nrm((N_ODD, C_DH), 0.1),
        'od_lam_k1': nrm((N_ODD, C_DH), 0.1),
        'od_lam_q2': nrm((N_ODD, C_DH), 0.1),
        'od_lam_k2': nrm((N_ODD, C_DH), 0.1),
        'od_c_subnorm': gain((N_ODD, 2 * C_DH)),
        'od_w_out': nrm((N_ODD, C_MIX, d), C_MIX ** -0.5),
        'od_norm_ff': gain((N_ODD, d)),
        'od_router': nrm((N_ODD, d, N_EXPERTS), d ** -0.5),
        'od_ex_gate': nrm((N_ODD, N_EXPERTS, d, FF_EXPERT), d ** -0.5),
        'od_ex_up': nrm((N_ODD, N_EXPERTS, d, FF_EXPERT), d ** -0.5),
        'od_ex_down': nrm((N_ODD, N_EXPERTS, FF_EXPERT, d), FF_EXPERT ** -0.5),
    }


def reference(x, positions, ev_norm_mix, ev_w_in, ev_g_cq, ev_g_ckv, ev_w_uq, ev_w_ukv,
              ev_a_qnorm, ev_a_knorm, ev_b_qnorm, ev_b_knorm, ev_b_rel_bias, ev_w_out,
              ev_norm_ff, ev_ff_gate, ev_ff_up, ev_ff_down,
              od_norm_mix, od_w_qkv, od_c_qnorm, od_c_knorm, od_lam_q1, od_lam_k1,
              od_lam_q2, od_lam_k2, od_c_subnorm, od_w_out, od_norm_ff, od_router,
              od_ex_gate, od_ex_up, od_ex_down):
    b, s_len, _ = x.shape
    cos_a, sin_a = rope_tables(positions, A_ROPE, A_THETA)
    cos_c, sin_c = rope_tables(positions, C_ROPE, ROPE_THETA)
    o1 = A_Q_LORA
    o2 = o1 + A_KV_LORA
    o3 = o2 + A_ROPE
    bw = B_HEADS * B_DH
    h = x
    for layer in range(DEPTH):
        i = layer // 2
        if layer % 2 == 0:
            u = rmsnorm(h, ev_norm_mix[i]) @ ev_w_in[i]
            ya = mla_attention(u[..., :o1], u[..., o1:o2], u[..., o2:o3], cos_a, sin_a,
                               ev_g_cq[i], ev_g_ckv[i], ev_w_uq[i], ev_w_ukv[i],
                               ev_a_qnorm[i], ev_a_knorm[i])
            qb = rmsnorm(split_heads(u[..., o3:o3 + bw], B_HEADS), ev_b_qnorm[i])
            kb = rmsnorm(split_heads(u[..., o3 + bw:o3 + 2 * bw], B_HEADS), ev_b_knorm[i])
            vb = split_heads(u[..., o3 + 2 * bw:o3 + 3 * bw], B_HEADS)
            yb = band_attention(qb, kb, vb, ev_b_rel_bias[i])
            y = jnp.concatenate([ya, yb], axis=1).transpose(0, 2, 1, 3).reshape(b, s_len, EVEN_MIX)
            h = h + y @ ev_w_out[i]
            h = h + swiglu(rmsnorm(h, ev_norm_ff[i]), ev_ff_gate[i], ev_ff_up[i], ev_ff_down[i])
        else:
            qkv = rmsnorm(h, od_norm_mix[i]) @ od_w_qkv[i]
            q, k, v = jnp.split(qkv, 3, axis=-1)
            q = q.reshape(b, s_len, C_HEADS, 2, C_DH).transpose(0, 2, 3, 1, 4)
            k = k.reshape(b, s_len, C_HEADS, 2, C_DH).transpose(0, 2, 3, 1, 4)
            v = v.reshape(b, s_len, C_HEADS, 2 * C_DH).transpose(0, 2, 1, 3)
            lam_init = 0.8 - 0.6 * math.exp(-0.3 * layer)
            yc = diff_attention(q, k, v, cos_c, sin_c, od_c_qnorm[i], od_c_knorm[i],
                                od_lam_q1[i], od_lam_k1[i], od_lam_q2[i], od_lam_k2[i],
                                od_c_subnorm[i], lam_init)
            h = h + yc.transpose(0, 2, 1, 3).reshape(b, s_len, C_MIX) @ od_w_out[i]
            h = h + moe_swiglu(rmsnorm(h, od_norm_ff[i]), od_router[i], od_ex_gate[i],
                               od_ex_up[i], od_ex_down[i])
    return h
```

```python
import functools
import math

import numpy as np
import jax
import jax.numpy as jnp
from jax import lax
from jax.experimental import pallas as pl
from jax.experimental.pallas import tpu as pltpu

F32 = jnp.float32
BF16 = jnp.bfloat16

D_MODEL = 1024
CHUNK = 64
EPS = 1e-6
LOG2E = 1.4426950408889634
NEG = -1e30

A_HEADS = 8
A_NOPE = 64
A_ROPE = 32
A_V = 64
A_Q_LORA = 256
A_KV_LORA = 128
A_THETA = 10000.0
B_HEADS = 8
B_DH = 64
B_LEFT_CHUNKS = 8
B_REL_CLIP = 128
C_HEADS = 8
C_DH = 64
C_ROPE = C_DH // 4
ROPE_THETA = 500000.0
FF_DENSE = 2752
N_EXPERTS = 8
FF_EXPERT = 3584

LANES = 128
FF_DENSE_PAD = 2816
VMEM_LIMIT = 56 * 1024 * 1024

TM_PROJ = 512
TQ_CAUSAL = 256
TQ_BAND = 128
BAND_KEYS = TQ_BAND + B_LEFT_CHUNKS * CHUNK
TR_MOE = 1024
FC_MOE = 896


def _cparams(sem):
    return pltpu.CompilerParams(dimension_semantics=sem, vmem_limit_bytes=VMEM_LIMIT)


def _const_spec(shape):
    nd = len(shape)
    return pl.BlockSpec(shape, lambda *_: (0,) * nd, pipeline_mode=pl.Buffered(1))


def _rms_scale(x):
    return lax.rsqrt(jnp.mean(x * x, axis=-1, keepdims=True) + EPS)


def _lane_iota():
    return lax.broadcasted_iota(jnp.int32, (1, LANES), 1)


def _proj_even_kernel(h_ref, gmix_ref, win_ref, gcq_ref, gckv_ref, wuq_ref, wuk_ref, wuv_ref,
                      gq_ref, gk_ref, gkr_ref, gbq_ref, gbk_ref, cos_ref, sa_ref, sb_ref,
                      qa_ref, ka_ref, va_ref, qb_ref, kb_ref, vb_ref):
    x = h_ref[...]
    hn = (x * _rms_scale(x) * gmix_ref[...]).astype(BF16)
    u = jnp.dot(hn, win_ref[...], preferred_element_type=F32)
    lane = _lane_iota()
    lo = lane < 64
    cos_t = cos_ref[...]
    sin_a = sa_ref[...]
    sin_b = sb_ref[...]

    def rope(xg):
        return (xg * cos_t + pltpu.roll(xg, LANES - A_ROPE // 2, 1) * sin_a
                + pltpu.roll(xg, A_ROPE // 2, 1) * sin_b)

    cq = u[:, 0:A_Q_LORA]
    cqn = (cq * _rms_scale(cq) * gcq_ref[...]).astype(BF16)
    q = jnp.dot(cqn, wuq_ref[...], preferred_element_type=F32)
    ckv = u[:, A_Q_LORA:A_Q_LORA + A_KV_LORA]
    ckvn = (ckv * _rms_scale(ckv) * gckv_ref[...]).astype(BF16)
    kn = jnp.dot(ckvn, wuk_ref[...], preferred_element_type=F32)
    va_ref[...] = jnp.dot(ckvn, wuv_ref[...], preferred_element_type=F32).astype(BF16)

    kr = u[:, 384:512]
    kr_s = lax.rsqrt(jnp.sum(kr * kr, axis=-1, keepdims=True) * (1.0 / A_ROPE) + EPS)
    krot = rope(kr * kr_s * gkr_ref[...])

    for hd in range(A_HEADS):
        sl = slice(hd * LANES, (hd + 1) * LANES)
        blk = q[:, sl]
        sq = blk * blk
        tot = jnp.sum(sq, axis=-1, keepdims=True)
        ssn = jnp.sum(jnp.where(lo, sq, 0.0), axis=-1, keepdims=True)
        sc = jnp.where(lo, lax.rsqrt(ssn * (1.0 / A_NOPE) + EPS),
                       lax.rsqrt((tot - ssn) * (1.0 / A_ROPE) + EPS))
        qa_ref[:, sl] = rope(blk * sc * gq_ref[:, sl]).astype(BF16)
        kb_ = kn[:, sl]
        ks = lax.rsqrt(jnp.sum(kb_ * kb_, axis=-1, keepdims=True) * (1.0 / A_NOPE) + EPS)
        ka_ref[:, sl] = (kb_ * ks * gk_ref[:, sl] + krot).astype(BF16)

    def pair_norm(blk, g):
        sq = blk * blk
        tot = jnp.sum(sq, axis=-1, keepdims=True)
        s_lo = jnp.sum(jnp.where(lo, sq, 0.0), axis=-1, keepdims=True)
        sc = jnp.where(lo, lax.rsqrt(s_lo * (1.0 / B_DH) + EPS),
                       lax.rsqrt((tot - s_lo) * (1.0 / B_DH) + EPS))
        return (blk * sc * g).astype(BF16)

    for p in range(B_HEADS // 2):
        sl = slice(p * LANES, (p + 1) * LANES)
        qb_ref[:, sl] = pair_norm(u[:, 512 + p * LANES:512 + (p + 1) * LANES], gbq_ref[:, sl])
        kb_ref[:, sl] = pair_norm(u[:, 1024 + p * LANES:1024 + (p + 1) * LANES], gbk_ref[:, sl])
    vb_ref[...] = u[:, 1536:2048].astype(BF16)


def _proj_even(h, gmix, win, gcq, gckv, wuq, wuk, wuv, gq, gk, gkr, gbq, gbk, cos_t, sin_a, sin_b):
    t = h.shape[0]
    tm = TM_PROJ
    row = lambda w: pl.BlockSpec((tm, w), lambda i: (i, 0))
    outs = [jax.ShapeDtypeStruct((t, w), BF16) for w in (1024, 1024, 512, 512, 512, 512)]
    return pl.pallas_call(
        _proj_even_kernel,
        grid=(t // tm,),
        in_specs=[row(D_MODEL), _const_spec(gmix.shape), _const_spec(win.shape),
                  _const_spec(gcq.shape), _const_spec(gckv.shape), _const_spec(wuq.shape),
                  _const_spec(wuk.shape), _const_spec(wuv.shape), _const_spec(gq.shape),
                  _const_spec(gk.shape), _const_spec(gkr.shape), _const_spec(gbq.shape),
                  _const_spec(gbk.shape), row(LANES), row(LANES), row(LANES)],
        out_specs=[row(1024), row(1024), row(512), row(512), row(512), row(512)],
        out_shape=outs,
        compiler_params=_cparams(("parallel",)),
        name="proj_even",
    )(h, gmix, win, gcq, gckv, wuq, wuk, wuv, gq, gk, gkr, gbq, gbk, cos_t, sin_a, sin_b)


def _causal_streams(q_ref, k_ref, v_ref, *, split):
    tq = q_ref.shape[0]
    tk = tq
    i = pl.program_id(2)
    q = q_ref[...]
    lo = _lane_iota() < 64
    if split:
        qs = [q[:, :LANES], q[:, LANES:]]
    else:
        zero = jnp.zeros_like(q)
        qs = [jnp.where(lo, q, zero), jnp.where(lo, zero, q)]
    rows = lax.broadcasted_iota(jnp.int32, (tq, tk), 0) // CHUNK
    cols = lax.broadcasted_iota(jnp.int32, (tq, tk), 1) // CHUNK
    allowed = cols <= rows

    def tile(j, carry, masked):
        start = pl.multiple_of(j * tk, tk)
        kt = k_ref[pl.ds(start, tk), :]
        vt = v_ref[pl.ds(start, tk), :]
        new = []
        for m in range(2):
            mx, l, acc = carry[m]
            km = kt[:, m * LANES:(m + 1) * LANES] if split else kt
            s = lax.dot_general(qs[m], km, (((1,), (1,)), ((), ())), preferred_element_type=F32)
            if masked:
                s = jnp.where(allowed, s, NEG)
            m_new = jnp.maximum(mx, jnp.max(s, axis=-1, keepdims=True))
            alpha = jnp.exp2(mx - m_new)
            p = jnp.exp2(s - m_new)
            l = alpha * l + jnp.sum(p, axis=-1, keepdims=True)
            acc = alpha * acc + jnp.dot(p.astype(BF16), vt, preferred_element_type=F32)
            new.append((m_new, l, acc))
        return tuple(new)

    init = tuple((jnp.full((tq, 1), NEG, F32), jnp.zeros((tq, 1), F32),
                  jnp.zeros((tq, LANES), F32)) for _ in range(2))
    carry = lax.fori_loop(0, i, lambda j, c: tile(j, c, False), init)
    carry = tile(i, carry, True)
    return [carry[m][2] / carry[m][1] for m in range(2)]


def _attn_mla_kernel(q_ref, k_ref, v_ref, o_ref):
    o0, o1 = _causal_streams(q_ref, k_ref, v_ref, split=True)
    o_ref[...] = jnp.where(_lane_iota() < 64, o0, o1).astype(o_ref.dtype)


def _attn_diff_kernel(q_ref, k_ref, v_ref, lq1_ref, lk1_ref, lq2_ref, lk2_ref, gsub_ref, o_ref,
                      *, lam_init):
    o0, o1 = _causal_streams(q_ref, k_ref, v_ref, split=False)
    lam = (jnp.exp(jnp.sum(lq1_ref[...] * lk1_ref[...], axis=-1, keepdims=True))
           - jnp.exp(jnp.sum(lq2_ref[...] * lk2_ref[...], axis=-1, keepdims=True)) + lam_init)
    o = o0 - lam * o1
    o = o * _rms_scale(o) * gsub_ref[...] * (1.0 - lam_init)
    o_ref[...] = o.astype(o_ref.dtype)


def _attn_causal(kernel, q, k, v, extras, *, batch, seq, n_blocks, qk_width):
    tq = TQ_CAUSAL
    nq = seq // tq
    in_specs = [pl.BlockSpec((tq, qk_width), lambda b, p, i: (b * nq + i, p)),
                pl.BlockSpec((seq, qk_width), lambda b, p, i: (b, p)),
                pl.BlockSpec((seq, LANES), lambda b, p, i: (b, p))]
    in_specs += [_const_spec(e.shape) for e in extras]
    return pl.pallas_call(
        kernel,
        grid=(batch, n_blocks, nq),
        in_specs=in_specs,
        out_specs=pl.BlockSpec((tq, LANES), lambda b, p, i: (b * nq + i, p)),
        out_shape=jax.ShapeDtypeStruct((batch * seq, n_blocks * LANES), BF16),
        compiler_params=_cparams(("parallel", "parallel", "arbitrary")),
        name=kernel.func.__name__ if isinstance(kernel, functools.partial) else kernel.__name__,
    )(q, k, v, *extras)


def _attn_band_kernel(q_ref, k_ref, v_ref, bias_ref, o_ref):
    tq = TQ_BAND
    i = pl.program_id(2)
    q = q_ref[...]
    lo = _lane_iota() < 64
    zero = jnp.zeros_like(q)
    qs = [jnp.where(lo, q, zero), jnp.where(lo, zero, q)]

    def window(start, width):
        kt = k_ref[pl.ds(start, width), :]
        vt = v_ref[pl.ds(start, width), :]
        outs = []
        for m in range(2):
            s = lax.dot_general(qs[m], kt, (((1,), (1,)), ((), ())), preferred_element_type=F32)
            s = s + bias_ref[m, :, BAND_KEYS - width:]
            p = jnp.exp2(s - jnp.max(s, axis=-1, keepdims=True))
            o = jnp.dot(p.astype(BF16), vt, preferred_element_type=F32)
            outs.append(o / jnp.sum(p, axis=-1, keepdims=True))
        o_ref[...] = jnp.where(lo, outs[0], outs[1]).astype(o_ref.dtype)

    n_edge = (BAND_KEYS - tq) // tq
    for e in range(n_edge):
        @pl.when(i == e)
        def _(e=e):
            window(0, (e + 1) * tq)

    @pl.when(i >= n_edge)
    def _():
        window(pl.multiple_of(i * tq - (BAND_KEYS - tq), tq), BAND_KEYS)


def _attn_band(q, k, v, bias, *, batch, seq):
    tq = TQ_BAND
    nq = seq // tq
    npair = B_HEADS // 2
    return pl.pallas_call(
        _attn_band_kernel,
        grid=(batch, npair, nq),
        in_specs=[pl.BlockSpec((tq, LANES), lambda b, p, i: (b * nq + i, p)),
                  pl.BlockSpec((seq, LANES), lambda b, p, i: (b, p)),
                  pl.BlockSpec((seq, LANES), lambda b, p, i: (b, p)),
                  pl.BlockSpec((2, tq, BAND_KEYS), lambda b, p, i: (p, 0, 0))],
        out_specs=pl.BlockSpec((tq, LANES), lambda b, p, i: (b * nq + i, p)),
        out_shape=jax.ShapeDtypeStruct((batch * seq, npair * LANES), BF16),
        compiler_params=_cparams(("parallel", "parallel", "arbitrary")),
        name="attn_band",
    )(q, k, v, bias)


def _silu(g):
    return g / (1.0 + jnp.exp(-g))


def _out_ffn_kernel(h_ref, ya_ref, yb_ref, woa_ref, wob_ref, gff_ref, wg_ref, wu_ref, wd_ref, o_ref):
    h1 = (h_ref[...] + jnp.dot(ya_ref[...], woa_ref[...], preferred_element_type=F32)
          + jnp.dot(yb_ref[...], wob_ref[...], preferred_element_type=F32))
    xn = (h1 * _rms_scale(h1) * gff_ref[...]).astype(BF16)
    g = jnp.dot(xn, wg_ref[...], preferred_element_type=F32)
    u = jnp.dot(xn, wu_ref[...], preferred_element_type=F32)
    a = (_silu(g) * u).astype(BF16)
    o_ref[...] = h1 + jnp.dot(a, wd_ref[...], preferred_element_type=F32)


def _out_ffn(h, ya, yb, woa, wob, gff, wg, wu, wd):
    t = h.shape[0]
    tm = TM_PROJ
    row = lambda w: pl.BlockSpec((tm, w), lambda i: (i, 0))
    return pl.pallas_call(
        _out_ffn_kernel,
        grid=(t // tm,),
        in_specs=[row(D_MODEL), row(ya.shape[1]), row(yb.shape[1])]
        + [_const_spec(w.shape) for w in (woa, wob, gff, wg, wu, wd)],
        out_specs=row(D_MODEL),
        out_shape=jax.ShapeDtypeStruct((t, D_MODEL), F32),
        compiler_params=_cparams(("parallel",)),
        name="out_ffn",
    )(h, ya, yb, woa, wob, gff, wg, wu, wd)


def _proj_odd_kernel(h_ref, gmix_ref, w_ref, gq_ref, gk_ref, cos_ref, sa_ref, sb_ref,
                     q_ref, k_ref, v_ref):
    x = h_ref[...]
    hn = (x * _rms_scale(x) * gmix_ref[...]).astype(BF16)
    u = jnp.dot(hn, w_ref[...], preferred_element_type=F32)
    lo = _lane_iota() < 64
    cos_t = cos_ref[...]
    sin_a = sa_ref[...]
    sin_b = sb_ref[...]
    width = C_HEADS * 2 * C_DH

    def norm_rope(blk, g):
        sq = blk * blk
        tot = jnp.sum(sq, axis=-1, keepdims=True)
        s_lo = jnp.sum(jnp.where(lo, sq, 0.0), axis=-1, keepdims=True)
        sc = jnp.where(lo, lax.rsqrt(s_lo * (1.0 / C_DH) + EPS),
                       lax.rsqrt((tot - s_lo) * (1.0 / C_DH) + EPS))
        xg = blk * sc * g
        return (xg * cos_t + pltpu.roll(xg, LANES - C_ROPE // 2, 1) * sin_a
                + pltpu.roll(xg, C_ROPE // 2, 1) * sin_b).astype(BF16)

    for hd in range(C_HEADS):
        sl = slice(hd * LANES, (hd + 1) * LANES)
        q_ref[:, sl] = norm_rope(u[:, hd * LANES:(hd + 1) * LANES], gq_ref[...])
        k_ref[:, sl] = norm_rope(u[:, width + hd * LANES:width + (hd + 1) * LANES], gk_ref[...])
    v_ref[...] = u[:, 2 * width:].astype(BF16)


def _proj_odd(h, gmix, w, gq, gk, cos_t, sin_a, sin_b):
    t = h.shape[0]
    tm = TM_PROJ
    row = lambda wd: pl.BlockSpec((tm, wd), lambda i: (i, 0))
    outs = [jax.ShapeDtypeStruct((t, 1024), BF16)] * 3
    return pl.pallas_call(
        _proj_odd_kernel,
        grid=(t // tm,),
        in_specs=[row(D_MODEL), _const_spec(gmix.shape), _const_spec(w.shape),
                  _const_spec(gq.shape), _const_spec(gk.shape), row(LANES), row(LANES), row(LANES)],
        out_specs=[row(1024)] * 3,
        out_shape=outs,
        compiler_params=_cparams(("parallel",)),
        name="proj_odd",
    )(h, gmix, w, gq, gk, cos_t, sin_a, sin_b)


def _out_router_kernel(h_ref, y_ref, wo_ref, gff_ref, wr_ref, h1_ref, xn_ref, cw_ref):
    h1 = h_ref[...] + jnp.dot(y_ref[...], wo_ref[...], preferred_element_type=F32)
    h1_ref[...] = h1
    xn = h1 * _rms_scale(h1) * gff_ref[...]
    xn_ref[...] = xn.astype(BF16)
    logits = [jnp.sum(xn * wr_ref[e:e + 1, :], axis=-1, keepdims=True) for e in range(N_EXPERTS)]
    m1 = functools.reduce(jnp.maximum, logits)
    idx1 = functools.reduce(jnp.minimum,
                            [jnp.where(logits[e] == m1, e, N_EXPERTS) for e in range(N_EXPERTS)])
    rest = [jnp.where(idx1 == e, -jnp.inf, logits[e]) for e in range(N_EXPERTS)]
    m2 = functools.reduce(jnp.maximum, rest)
    idx2 = functools.reduce(jnp.minimum,
                            [jnp.where(rest[e] == m2, e, N_EXPERTS) for e in range(N_EXPERTS)])
    e2 = jnp.exp(m2 - m1)
    g1 = 1.0 / (1.0 + e2)
    g2 = e2 * g1
    lane = _lane_iota()
    cw_ref[...] = jnp.where(lane == idx1, g1, jnp.where(lane == idx2, g2, 0.0))


def _out_router(h, y, wo, gff, wr):
    t = h.shape[0]
    tm = TM_PROJ
    row = lambda w: pl.BlockSpec((tm, w), lambda i: (i, 0))
    return pl.pallas_call(
        _out_router_kernel,
        grid=(t // tm,),
        in_specs=[row(D_MODEL), row(y.shape[1]), _const_spec(wo.shape), _const_spec(gff.shape),
                  _const_spec(wr.shape)],
        out_specs=[row(D_MODEL), row(D_MODEL), row(LANES)],
        out_shape=[jax.ShapeDtypeStruct((t, D_MODEL), F32), jax.ShapeDtypeStruct((t, D_MODEL), BF16),
                   jax.ShapeDtypeStruct((t, LANES), F32)],
        compiler_params=_cparams(("parallel",)),
        name="out_router",
    )(h, y, wo, gff, wr)


def _moe_kernel(te_ref, nu_ref, x_ref, gate_ref, wg_ref, wu_ref, wd_ref, o_ref):
    j = pl.program_id(0)
    c = pl.program_id(1)

    @pl.when(c == 0)
    def _():
        o_ref[...] = jnp.zeros_like(o_ref)

    @pl.when(j < nu_ref[0])
    def _():
        x = x_ref[...]
        g = jnp.dot(x, wg_ref[0].astype(BF16), preferred_element_type=F32)
        u = jnp.dot(x, wu_ref[0].astype(BF16), preferred_element_type=F32)
        a = (_silu(g) * u).astype(BF16)
        o_ref[...] += jnp.dot(a, wd_ref[0].astype(BF16), preferred_element_type=F32)

    @pl.when(c == pl.num_programs(1) - 1)
    def _():
        o_ref[...] = o_ref[...] * gate_ref[...]


def _moe(tile_expert, n_used, xs, gate, wg, wu, wd):
    r = xs.shape[0]
    tr, fc = TR_MOE, FC_MOE
    nc = FF_EXPERT // fc

    def wcol(j, c, te, nu):
        live = j < nu[0]
        return (te[j], 0, jnp.where(live, c, nc - 1))

    def wrow(j, c, te, nu):
        live = j < nu[0]
        return (te[j], jnp.where(live, c, nc - 1), 0)

    grid_spec = pltpu.PrefetchScalarGridSpec(
        num_scalar_prefetch=2,
        grid=(r // tr, nc),
        in_specs=[pl.BlockSpec((tr, D_MODEL), lambda j, c, te, nu: (j, 0)),
                  pl.BlockSpec((tr, 1), lambda j, c, te, nu: (j, 0)),
                  pl.BlockSpec((1, D_MODEL, fc), wcol),
                  pl.BlockSpec((1, D_MODEL, fc), wcol),
                  pl.BlockSpec((1, fc, D_MODEL), wrow)],
        out_specs=pl.BlockSpec((tr, D_MODEL), lambda j, c, te, nu: (j, 0)),
    )
    return pl.pallas_call(
        _moe_kernel,
        grid_spec=grid_spec,
        out_shape=jax.ShapeDtypeStruct((r, D_MODEL), F32),
        compiler_params=_cparams(("arbitrary", "arbitrary")),
        name="moe_grouped",
    )(tile_expert, n_used, xs, gate, wg, wu, wd)


def _route_plan(cw):
    t = cw.shape[0]
    tr = TR_MOE
    n_tiles = (2 * t) // tr + N_EXPERTS
    sel = cw > 0.0
    cnt = sel.astype(jnp.int32)
    pos = jnp.cumsum(cnt, axis=0) - cnt
    counts = jnp.sum(cnt, axis=0)
    tiles_per = (counts + tr - 1) // tr
    tile_end = jnp.cumsum(tiles_per)
    n_used = tile_end[-1]
    row = (tile_end - tiles_per)[None, :] * tr + pos
    row = jnp.where(sel, row, n_tiles * tr)
    tile_ids = jnp.minimum(jnp.arange(n_tiles, dtype=jnp.int32), n_used - 1)
    tile_expert = jnp.sum((tile_ids[:, None] >= tile_end[None, :]).astype(jnp.int32), axis=1)
    tok = jnp.broadcast_to(jnp.arange(t, dtype=jnp.int32)[:, None], row.shape)
    row_token = jnp.zeros((n_tiles * tr,), jnp.int32).at[row.reshape(-1)].set(tok.reshape(-1), mode="drop")
    row_gate = jnp.zeros((n_tiles * tr,), F32).at[row.reshape(-1)].set(cw.reshape(-1), mode="drop")
    return row, row_token, row_gate, tile_expert.astype(jnp.int32), n_used.astype(jnp.int32).reshape(1)


def _rope_tables(positions, dim, theta):
    inv = 1.0 / (theta ** (jnp.arange(0, dim, 2, dtype=F32) / dim))
    ang = positions.astype(F32).reshape(-1, 1) * inv
    return jnp.cos(ang), jnp.sin(ang)


def _tables_a(positions):
    cos, sin = _rope_tables(positions, A_ROPE, A_THETA)
    t = cos.shape[0]
    one, z16, z32 = jnp.ones((t, A_NOPE), F32), jnp.zeros((t, 16), F32), jnp.zeros((t, 32), F32)
    z64 = jnp.zeros((t, 64), F32)
    return (jnp.concatenate([one, cos, cos, z32], 1), jnp.concatenate([z64, -sin, z16, z32], 1),
            jnp.concatenate([z64, z16, sin, z32], 1))


def _tables_c(positions):
    cos, sin = _rope_tables(positions, C_ROPE, ROPE_THETA)
    t = cos.shape[0]
    one, z8, z48 = jnp.ones((t, 48), F32), jnp.zeros((t, 8), F32), jnp.zeros((t, 48), F32)
    two = lambda a: jnp.concatenate([a, a], 1)
    return (two(jnp.concatenate([cos, cos, one], 1)), two(jnp.concatenate([-sin, z8, z48], 1)),
            two(jnp.concatenate([z8, sin, z48], 1)))


def _band_bias(rel_table):
    r = np.arange(TQ_BAND)[:, None]
    j = np.arange(BAND_KEYS)[None, :]
    dist = r + B_LEFT_CHUNKS * CHUNK - j
    cdiff = r // CHUNK + B_LEFT_CHUNKS - j // CHUNK
    allowed = (cdiff >= 0) & (cdiff <= B_LEFT_CHUNKS)
    rel = np.clip(dist, -B_REL_CLIP, B_REL_CLIP) + B_REL_CLIP
    bias = rel_table[:, rel].astype(F32) * LOG2E
    return jnp.where(jnp.asarray(allowed)[None], bias, NEG)


def _pad_last(a, width):
    return jnp.pad(a, [(0, 0)] * (a.ndim - 1) + [(0, width - a.shape[-1])])


def kernel(x, positions, ev_norm_mix, ev_w_in, ev_g_cq, ev_g_ckv, ev_w_uq, ev_w_ukv, ev_a_qnorm, ev_a_knorm, ev_b_qnorm, ev_b_knorm, ev_b_rel_bias, ev_w_out, ev_norm_ff, ev_ff_gate, ev_ff_up, ev_ff_down, od_norm_mix, od_w_qkv, od_c_qnorm, od_c_knorm, od_lam_q1, od_lam_k1, od_lam_q2, od_lam_k2, od_c_subnorm, od_w_out, od_norm_ff, od_router, od_ex_gate, od_ex_up, od_ex_down):
    batch, seq, d = x.shape
    t = batch * seq
    depth = ev_w_in.shape[0] + od_w_qkv.shape[0]
    h = x.reshape(t, d)
    row = lambda a: a.reshape(1, -1).astype(F32)
    tab_a = _tables_a(positions)
    tab_c = _tables_c(positions)

    for layer in range(depth):
        i = layer // 2
        if layer % 2 == 0:
            w = ev_w_in[i]
            o1, o2, o3 = A_Q_LORA, A_Q_LORA + A_KV_LORA, A_Q_LORA + A_KV_LORA + A_ROPE
            zc = lambda n: jnp.zeros((d, n), F32)
            win = jnp.concatenate([w[:, :o2], zc(64), w[:, o2:o3], zc(32), w[:, o3:]], 1).astype(BF16)
            wuq = _pad_last(ev_w_uq[i], LANES).reshape(A_Q_LORA, A_HEADS * LANES).astype(BF16)
            wuk = _pad_last(ev_w_ukv[i][:, :, :A_NOPE], LANES).reshape(A_KV_LORA, A_HEADS * LANES).astype(BF16)
            wuv = ev_w_ukv[i][:, :, A_NOPE:].reshape(A_KV_LORA, A_HEADS * A_V).astype(BF16)
            sc_a = (A_NOPE + A_ROPE) ** -0.5 * LOG2E
            gq = jnp.tile(_pad_last(ev_a_qnorm[i], LANES), A_HEADS).reshape(1, -1) * sc_a
            gk = jnp.tile(_pad_last(ev_a_knorm[i][:A_NOPE], LANES), A_HEADS).reshape(1, -1)
            gkr = jnp.concatenate([jnp.zeros((A_NOPE,), F32), ev_a_knorm[i][A_NOPE:],
                                   jnp.zeros((32,), F32)]).reshape(1, -1)
            gbq = jnp.tile(ev_b_qnorm[i], B_HEADS).reshape(1, -1) * (B_DH ** -0.5 * LOG2E)
            gbk = jnp.tile(ev_b_knorm[i], B_HEADS).reshape(1, -1)
            qa, ka, va, qb, kb, vb = _proj_even(
                h, row(ev_norm_mix[i]), win, row(ev_g_cq[i]), row(ev_g_ckv[i]), wuq, wuk, wuv,
                gq, gk, gkr, gbq, gbk, *tab_a)
            ya = _attn_causal(_attn_mla_kernel, qa, ka, va, (), batch=batch, seq=seq,
                              n_blocks=A_HEADS // 2, qk_width=2 * LANES)
            bias = _band_bias(ev_b_rel_bias[i]).reshape(B_HEADS // 2, 2, TQ_BAND, BAND_KEYS)
            bias = bias.reshape(B_HEADS, TQ_BAND, BAND_KEYS)
            yb = _attn_band(qb, kb, vb, bias, batch=batch, seq=seq)
            wo = ev_w_out[i].astype(BF16)
            na = A_HEADS * A_V
            padf = lambda a: _pad_last(a, FF_DENSE_PAD).astype(BF16)
            wd = jnp.pad(ev_ff_down[i], ((0, FF_DENSE_PAD - FF_DENSE), (0, 0))).astype(BF16)
            h = _out_ffn(h, ya, yb, wo[:na], wo[na:], row(ev_norm_ff[i]),
                         padf(ev_ff_gate[i]), padf(ev_ff_up[i]), wd)
        else:
            lam_init = 0.8 - 0.6 * math.exp(-0.3 * layer)
            gq = jnp.tile(od_c_qnorm[i], 2).reshape(1, -1) * (C_DH ** -0.5 * LOG2E)
            gk = jnp.tile(od_c_knorm[i], 2).reshape(1, -1)
            q, k, v = _proj_odd(h, row(od_norm_mix[i]), od_w_qkv[i].astype(BF16), gq, gk, *tab_c)
            extras = (row(od_lam_q1[i]), row(od_lam_k1[i]), row(od_lam_q2[i]), row(od_lam_k2[i]),
                      row(od_c_subnorm[i]))
            yc = _attn_causal(functools.partial(_attn_diff_kernel, lam_init=lam_init), q, k, v, extras,
                              batch=batch, seq=seq, n_blocks=C_HEADS, qk_width=LANES)
            h1, xn, cw = _out_router(h, yc, od_w_out[i].astype(BF16), row(od_norm_ff[i]),
                                     od_router[i].T.astype(F32))
            rows, row_token, row_gate, tile_expert, n_used = _route_plan(cw[:, :N_EXPERTS])
            xs = jnp.take(xn, row_token, axis=0)
            y = _moe(tile_expert, n_used, xs, row_gate.reshape(-1, 1), od_ex_gate[i], od_ex_up[i],
                     od_ex_down[i])
            y = jnp.concatenate([y, jnp.zeros((8, d), F32)], 0)
            two = jnp.sort(rows, axis=1)[:, :2]
            h = h1 + jnp.sum(jnp.take(y, two, axis=0), axis=1)
    return h.reshape(batch, seq, d)
```

```python
import functools
import math

import numpy as np
import jax
import jax.numpy as jnp
from jax import lax
from jax.experimental import pallas as pl
from jax.experimental.pallas import tpu as pltpu

F32 = jnp.float32
BF16 = jnp.bfloat16

D_MODEL = 1024
CHUNK = 64
EPS = 1e-6
LOG2E = 1.4426950408889634
NEG = -1e30

A_HEADS = 8
A_NOPE = 64
A_ROPE = 32
A_V = 64
A_Q_LORA = 256
A_KV_LORA = 128
A_THETA = 10000.0
B_HEADS = 8
B_DH = 64
B_LEFT_CHUNKS = 8
B_REL_CLIP = 128
C_HEADS = 8
C_DH = 64
C_ROPE = C_DH // 4
ROPE_THETA = 500000.0
FF_DENSE = 2752
N_EXPERTS = 8
FF_EXPERT = 3584

LANES = 128
FF_DENSE_PAD = 2816
VMEM_LIMIT = 56 * 1024 * 1024

TM_PROJ = 512
TQ_CAUSAL = 256
TQ_BAND = 128
BAND_KEYS = TQ_BAND + B_LEFT_CHUNKS * CHUNK
TR_MOE = 1024
FC_MOE = 896


def _cparams(sem):
    return pltpu.CompilerParams(dimension_semantics=sem, vmem_limit_bytes=VMEM_LIMIT)


def _const_spec(shape):
    nd = len(shape)
    return pl.BlockSpec(shape, lambda *_: (0,) * nd, pipeline_mode=pl.Buffered(1))


def _rms_scale(x):
    return lax.rsqrt(jnp.mean(x * x, axis=-1, keepdims=True) + EPS)


def _lane_iota():
    return lax.broadcasted_iota(jnp.int32, (1, LANES), 1)


def _dot_nt(a, b):
    return lax.dot_general(a, b, (((1,), (1,)), ((), ())), preferred_element_type=F32)


def _proj_even_kernel(h_ref, gmix_ref, win_ref, wvbt_ref, gcq_ref, gckv_ref, wuq_ref, wuk_ref, wuvt_ref,
                      gq_ref, gk_ref, gkr_ref, gbq_ref, gbk_ref, cos_ref, sa_ref, sb_ref,
                      qa_ref, ka_ref, vat_ref, qb_ref, kb_ref, vbt_ref):
    x = h_ref[...]
    hn = (x * _rms_scale(x) * gmix_ref[...]).astype(BF16)
    u = jnp.dot(hn, win_ref[...], preferred_element_type=F32)
    vbt_ref[...] = _dot_nt(wvbt_ref[...], hn).astype(BF16)
    lane = _lane_iota()
    lo = lane < 64
    cos_t = cos_ref[...]
    sin_a = sa_ref[...]
    sin_b = sb_ref[...]

    def rope(xg):
        return (xg * cos_t + pltpu.roll(xg, LANES - A_ROPE // 2, 1) * sin_a
                + pltpu.roll(xg, A_ROPE // 2, 1) * sin_b)

    cq = u[:, 0:A_Q_LORA]
    cqn = (cq * _rms_scale(cq) * gcq_ref[...]).astype(BF16)
    q = jnp.dot(cqn, wuq_ref[...], preferred_element_type=F32)
    ckv = u[:, A_Q_LORA:A_Q_LORA + A_KV_LORA]
    ckvn = (ckv * _rms_scale(ckv) * gckv_ref[...]).astype(BF16)
    kn = jnp.dot(ckvn, wuk_ref[...], preferred_element_type=F32)
    vat_ref[...] = _dot_nt(wuvt_ref[...], ckvn).astype(BF16)

    kr = u[:, 384:512]
    kr_s = lax.rsqrt(jnp.sum(kr * kr, axis=-1, keepdims=True) * (1.0 / A_ROPE) + EPS)
    krot = rope(kr * kr_s * gkr_ref[...])

    for hd in range(A_HEADS):
        sl = slice(hd * LANES, (hd + 1) * LANES)
        blk = q[:, sl]
        sq = blk * blk
        tot = jnp.sum(sq, axis=-1, keepdims=True)
        ssn = jnp.sum(jnp.where(lo, sq, 0.0), axis=-1, keepdims=True)
        sc = jnp.where(lo, lax.rsqrt(ssn * (1.0 / A_NOPE) + EPS),
                       lax.rsqrt((tot - ssn) * (1.0 / A_ROPE) + EPS))
        qa_ref[:, sl] = rope(blk * sc * gq_ref[:, sl]).astype(BF16)
        kb_ = kn[:, sl]
        ks = lax.rsqrt(jnp.sum(kb_ * kb_, axis=-1, keepdims=True) * (1.0 / A_NOPE) + EPS)
        ka_ref[:, sl] = (kb_ * ks * gk_ref[:, sl] + krot).astype(BF16)

    def pair_norm(blk, g):
        sq = blk * blk
        tot = jnp.sum(sq, axis=-1, keepdims=True)
        s_lo = jnp.sum(jnp.where(lo, sq, 0.0), axis=-1, keepdims=True)
        sc = jnp.where(lo, lax.rsqrt(s_lo * (1.0 / B_DH) + EPS),
                       lax.rsqrt((tot - s_lo) * (1.0 / B_DH) + EPS))
        return (blk * sc * g).astype(BF16)

    for p in range(B_HEADS // 2):
        sl = slice(p * LANES, (p + 1) * LANES)
        qb_ref[:, sl] = pair_norm(u[:, 512 + p * LANES:512 + (p + 1) * LANES], gbq_ref[:, sl])
        kb_ref[:, sl] = pair_norm(u[:, 1024 + p * LANES:1024 + (p + 1) * LANES], gbk_ref[:, sl])


def _proj_even(h, gmix, win, wvbt, gcq, gckv, wuq, wuk, wuvt, gq, gk, gkr, gbq, gbk, cos_t, sin_a, sin_b):
    t = h.shape[0]
    tm = TM_PROJ
    row = lambda w: pl.BlockSpec((tm, w), lambda i: (i, 0))
    col = lambda w: pl.BlockSpec((w, tm), lambda i: (0, i))
    rshape = lambda w: jax.ShapeDtypeStruct((t, w), BF16)
    cshape = lambda w: jax.ShapeDtypeStruct((w, t), BF16)
    consts = (gmix, win, wvbt, gcq, gckv, wuq, wuk, wuvt, gq, gk, gkr, gbq, gbk)
    return pl.pallas_call(
        _proj_even_kernel,
        grid=(t // tm,),
        in_specs=[row(D_MODEL)] + [_const_spec(c.shape) for c in consts] + [row(LANES)] * 3,
        out_specs=[row(1024), row(1024), col(512), row(512), row(512), col(512)],
        out_shape=[rshape(1024), rshape(1024), cshape(512), rshape(512), rshape(512), cshape(512)],
        compiler_params=_cparams(("parallel",)),
        name="proj_even",
    )(h, *consts, cos_t, sin_a, sin_b)


def _causal_streams(q_ref, k_ref, vt_ref, c, *, split):
    tq = TQ_CAUSAL
    q = q_ref[c * tq:(c + 1) * tq, :]
    lo = _lane_iota() < 64
    if split:
        qs = [q[:, :LANES], q[:, LANES:]]
    else:
        zero = jnp.zeros_like(q)
        qs = [jnp.where(lo, q, zero), jnp.where(lo, zero, q)]
    kchunk = lax.broadcasted_iota(jnp.int32, (tq, tq), 0) // CHUNK
    qchunk = lax.broadcasted_iota(jnp.int32, (tq, tq), 1) // CHUNK
    allowed = kchunk <= qchunk
    n_past = c * tq
    outs = []
    for m in range(2):
        ksl = slice(m * LANES, (m + 1) * LANES) if split else slice(None)
        s_d = jnp.where(allowed, _dot_nt(k_ref[n_past:n_past + tq, ksl], qs[m]), NEG)
        mx = jnp.max(s_d, axis=0, keepdims=True)
        if c:
            s_p = _dot_nt(k_ref[0:n_past, ksl], qs[m])
            mx = jnp.maximum(mx, jnp.max(s_p, axis=0, keepdims=True))
        p_d = jnp.exp2(s_d - mx)
        l = jnp.sum(p_d, axis=0, keepdims=True)
        o = jnp.dot(vt_ref[:, n_past:n_past + tq], p_d.astype(BF16), preferred_element_type=F32)
        if c:
            p_p = jnp.exp2(s_p - mx)
            l = l + jnp.sum(p_p, axis=0, keepdims=True)
            o = o + jnp.dot(vt_ref[:, 0:n_past], p_p.astype(BF16), preferred_element_type=F32)
        outs.append(o * (1.0 / l))
    return outs


def _row_iota(n):
    return lax.broadcasted_iota(jnp.int32, (LANES, n), 0)


def _attn_mla_kernel(q_ref, k_ref, vt_ref, o_ref):
    tq = TQ_CAUSAL
    for c in range(q_ref.shape[0] // tq):
        o0, o1 = _causal_streams(q_ref, k_ref, vt_ref, c, split=True)
        o_ref[c * tq:(c + 1) * tq, :] = jnp.where(_row_iota(tq) < 64, o0, o1).T.astype(o_ref.dtype)


def _attn_diff_kernel(q_ref, k_ref, vt_ref, lq1_ref, lk1_ref, lq2_ref, lk2_ref, gsub_ref, o_ref,
                      *, lam_init):
    tq = TQ_CAUSAL
    lam = (jnp.exp(jnp.sum(lq1_ref[...] * lk1_ref[...], axis=-1, keepdims=True))
           - jnp.exp(jnp.sum(lq2_ref[...] * lk2_ref[...], axis=-1, keepdims=True)) + lam_init)
    for c in range(q_ref.shape[0] // tq):
        o0, o1 = _causal_streams(q_ref, k_ref, vt_ref, c, split=False)
        o = (o0 - lam * o1).T
        o = o * _rms_scale(o) * gsub_ref[...] * (1.0 - lam_init)
        o_ref[c * tq:(c + 1) * tq, :] = o.astype(o_ref.dtype)


def _attn_causal(kernel, q, k, vt, extras, *, batch, seq, n_blocks, qk_width):
    in_specs = [pl.BlockSpec((seq, qk_width), lambda b, p: (b, p)),
                pl.BlockSpec((seq, qk_width), lambda b, p: (b, p)),
                pl.BlockSpec((LANES, seq), lambda b, p: (p, b))]
    in_specs += [_const_spec(e.shape) for e in extras]
    return pl.pallas_call(
        kernel,
        grid=(batch, n_blocks),
        in_specs=in_specs,
        out_specs=pl.BlockSpec((seq, LANES), lambda b, p: (b, p)),
        out_shape=jax.ShapeDtypeStruct((batch * seq, n_blocks * LANES), BF16),
        compiler_params=_cparams(("parallel", "parallel")),
        name=kernel.func.__name__ if isinstance(kernel, functools.partial) else kernel.__name__,
    )(q, k, vt, *extras)


def _attn_band_kernel(q_ref, k_ref, vt_ref, bias_ref, o_ref):
    tq = TQ_BAND
    i = pl.program_id(2)
    q = q_ref[...]
    lo = _lane_iota() < 64
    zero = jnp.zeros_like(q)
    qs = [jnp.where(lo, q, zero), jnp.where(lo, zero, q)]

    def window(start, width):
        kt = k_ref[pl.ds(start, width), :]
        vt = vt_ref[:, pl.ds(start, width)]
        outs = []
        for m in range(2):
            s = lax.dot_general(kt, qs[m], (((1,), (1,)), ((), ())), preferred_element_type=F32)
            s = s + bias_ref[m, BAND_KEYS - width:, :]
            p = jnp.exp2(s - jnp.max(s, axis=0, keepdims=True))
            o = jnp.dot(vt, p.astype(BF16), preferred_element_type=F32)
            outs.append(o * (1.0 / jnp.sum(p, axis=0, keepdims=True)))
        o_ref[...] = jnp.where(_row_iota(tq) < 64, outs[0], outs[1]).T.astype(o_ref.dtype)

    n_edge = (BAND_KEYS - tq) // tq
    for e in range(n_edge):
        @pl.when(i == e)
        def _(e=e):
            window(0, (e + 1) * tq)

    @pl.when(i >= n_edge)
    def _():
        window(pl.multiple_of(i * tq - (BAND_KEYS - tq), tq), BAND_KEYS)


def _attn_band(q, k, vt, bias, *, batch, seq):
    tq = TQ_BAND
    nq = seq // tq
    npair = B_HEADS // 2
    return pl.pallas_call(
        _attn_band_kernel,
        grid=(batch, npair, nq),
        in_specs=[pl.BlockSpec((tq, LANES), lambda b, p, i: (b * nq + i, p)),
                  pl.BlockSpec((seq, LANES), lambda b, p, i: (b, p)),
                  pl.BlockSpec((LANES, seq), lambda b, p, i: (p, b)),
                  pl.BlockSpec((2, BAND_KEYS, tq), lambda b, p, i: (p, 0, 0))],
        out_specs=pl.BlockSpec((tq, LANES), lambda b, p, i: (b * nq + i, p)),
        out_shape=jax.ShapeDtypeStruct((batch * seq, npair * LANES), BF16),
        compiler_params=_cparams(("parallel", "parallel", "arbitrary")),
        name="attn_band",
    )(q, k, vt, bias)


def _silu(g):
    return g / (1.0 + jnp.exp(-g))


def _out_ffn_kernel(h_ref, ya_ref, yb_ref, woa_ref, wob_ref, gff_ref, wg_ref, wu_ref, wd_ref, o_ref):
    h1 = (h_ref[...] + jnp.dot(ya_ref[...], woa_ref[...], preferred_element_type=F32)
          + jnp.dot(yb_ref[...], wob_ref[...], preferred_element_type=F32))
    xn = (h1 * _rms_scale(h1) * gff_ref[...]).astype(BF16)
    g = jnp.dot(xn, wg_ref[...], preferred_element_type=F32)
    u = jnp.dot(xn, wu_ref[...], preferred_element_type=F32)
    a = (_silu(g) * u).astype(BF16)
    o_ref[...] = h1 + jnp.dot(a, wd_ref[...], preferred_element_type=F32)


def _out_ffn(h, ya, yb, woa, wob, gff, wg, wu, wd):
    t = h.shape[0]
    tm = TM_PROJ
    row = lambda w: pl.BlockSpec((tm, w), lambda i: (i, 0))
    return pl.pallas_call(
        _out_ffn_kernel,
        grid=(t // tm,),
        in_specs=[row(D_MODEL), row(ya.shape[1]), row(yb.shape[1])]
        + [_const_spec(w.shape) for w in (woa, wob, gff, wg, wu, wd)],
        out_specs=row(D_MODEL),
        out_shape=jax.ShapeDtypeStruct((t, D_MODEL), F32),
        compiler_params=_cparams(("parallel",)),
        name="out_ffn",
    )(h, ya, yb, woa, wob, gff, wg, wu, wd)


def _proj_odd_kernel(h_ref, gmix_ref, w_ref, wvt_ref, gq_ref, gk_ref, cos_ref, sa_ref, sb_ref,
                     q_ref, k_ref, vt_ref):
    x = h_ref[...]
    hn = (x * _rms_scale(x) * gmix_ref[...]).astype(BF16)
    u = jnp.dot(hn, w_ref[...], preferred_element_type=F32)
    vt_ref[...] = _dot_nt(wvt_ref[...], hn).astype(BF16)
    lo = _lane_iota() < 64
    cos_t = cos_ref[...]
    sin_a = sa_ref[...]
    sin_b = sb_ref[...]
    width = C_HEADS * 2 * C_DH

    def norm_rope(blk, g):
        sq = blk * blk
        tot = jnp.sum(sq, axis=-1, keepdims=True)
        s_lo = jnp.sum(jnp.where(lo, sq, 0.0), axis=-1, keepdims=True)
        sc = jnp.where(lo, lax.rsqrt(s_lo * (1.0 / C_DH) + EPS),
                       lax.rsqrt((tot - s_lo) * (1.0 / C_DH) + EPS))
        xg = blk * sc * g
        return (xg * cos_t + pltpu.roll(xg, LANES - C_ROPE // 2, 1) * sin_a
                + pltpu.roll(xg, C_ROPE // 2, 1) * sin_b).astype(BF16)

    for hd in range(C_HEADS):
        sl = slice(hd * LANES, (hd + 1) * LANES)
        q_ref[:, sl] = norm_rope(u[:, hd * LANES:(hd + 1) * LANES], gq_ref[...])
        k_ref[:, sl] = norm_rope(u[:, width + hd * LANES:width + (hd + 1) * LANES], gk_ref[...])


def _proj_odd(h, gmix, w, wvt, gq, gk, cos_t, sin_a, sin_b):
    t = h.shape[0]
    tm = TM_PROJ
    row = lambda wd: pl.BlockSpec((tm, wd), lambda i: (i, 0))
    width = C_HEADS * 2 * C_DH
    consts = (gmix, w, wvt, gq, gk)
    return pl.pallas_call(
        _proj_odd_kernel,
        grid=(t // tm,),
        in_specs=[row(D_MODEL)] + [_const_spec(c.shape) for c in consts] + [row(LANES)] * 3,
        out_specs=[row(width), row(width), pl.BlockSpec((width, tm), lambda i: (0, i))],
        out_shape=[jax.ShapeDtypeStruct((t, width), BF16), jax.ShapeDtypeStruct((t, width), BF16),
                   jax.ShapeDtypeStruct((width, t), BF16)],
        compiler_params=_cparams(("parallel",)),
        name="proj_odd",
    )(h, *consts, cos_t, sin_a, sin_b)


def _out_router_kernel(h_ref, y_ref, wo_ref, gff_ref, wr_ref, h1_ref, xn_ref, cw_ref):
    h1 = h_ref[...] + jnp.dot(y_ref[...], wo_ref[...], preferred_element_type=F32)
    h1_ref[...] = h1
    xn = h1 * _rms_scale(h1) * gff_ref[...]
    xn_ref[...] = xn.astype(BF16)
    logits = [jnp.sum(xn * wr_ref[e:e + 1, :], axis=-1, keepdims=True) for e in range(N_EXPERTS)]
    m1 = functools.reduce(jnp.maximum, logits)
    idx1 = functools.reduce(jnp.minimum,
                            [jnp.where(logits[e] == m1, e, N_EXPERTS) for e in range(N_EXPERTS)])
    rest = [jnp.where(idx1 == e, -jnp.inf, logits[e]) for e in range(N_EXPERTS)]
    m2 = functools.reduce(jnp.maximum, rest)
    idx2 = functools.reduce(jnp.minimum,
                            [jnp.where(rest[e] == m2, e, N_EXPERTS) for e in range(N_EXPERTS)])
    e2 = jnp.exp(m2 - m1)
    g1 = 1.0 / (1.0 + e2)
    g2 = e2 * g1
    lane = _lane_iota()
    cw_ref[...] = jnp.where(lane == idx1, g1, jnp.where(lane == idx2, g2, 0.0))


def _out_router(h, y, wo, gff, wr):
    t = h.shape[0]
    tm = TM_PROJ
    row = lambda w: pl.BlockSpec((tm, w), lambda i: (i, 0))
    return pl.pallas_call(
        _out_router_kernel,
        grid=(t // tm,),
        in_specs=[row(D_MODEL), row(y.shape[1]), _const_spec(wo.shape), _const_spec(gff.shape),
                  _const_spec(wr.shape)],
        out_specs=[row(D_MODEL), row(D_MODEL), row(LANES)],
        out_shape=[jax.ShapeDtypeStruct((t, D_MODEL), F32), jax.ShapeDtypeStruct((t, D_MODEL), BF16),
                   jax.ShapeDtypeStruct((t, LANES), F32)],
        compiler_params=_cparams(("parallel",)),
        name="out_router",
    )(h, y, wo, gff, wr)


def _moe_kernel(te_ref, nu_ref, x_ref, gate_ref, wg_ref, wu_ref, wd_ref, o_ref):
    j = pl.program_id(0)
    c = pl.program_id(1)

    @pl.when(c == 0)
    def _():
        o_ref[...] = jnp.zeros_like(o_ref)

    @pl.when(j < nu_ref[0])
    def _():
        x = x_ref[...]
        g = jnp.dot(x, wg_ref[0].astype(BF16), preferred_element_type=F32)
        u = jnp.dot(x, wu_ref[0].astype(BF16), preferred_element_type=F32)
        a = (_silu(g) * u).astype(BF16)
        o_ref[...] += jnp.dot(a, wd_ref[0].astype(BF16), preferred_element_type=F32)

    @pl.when(c == pl.num_programs(1) - 1)
    def _():
        o_ref[...] = o_ref[...] * gate_ref[...]


def _moe(tile_expert, n_used, xs, gate, wg, wu, wd):
    r = xs.shape[0]
    tr, fc = TR_MOE, FC_MOE
    nc = FF_EXPERT // fc

    def wcol(j, c, te, nu):
        live = j < nu[0]
        return (te[j], 0, jnp.where(live, c, nc - 1))

    def wrow(j, c, te, nu):
        live = j < nu[0]
        return (te[j], jnp.where(live, c, nc - 1), 0)

    grid_spec = pltpu.PrefetchScalarGridSpec(
        num_scalar_prefetch=2,
        grid=(r // tr, nc),
        in_specs=[pl.BlockSpec((tr, D_MODEL), lambda j, c, te, nu: (j, 0)),
                  pl.BlockSpec((tr, 1), lambda j, c, te, nu: (j, 0)),
                  pl.BlockSpec((1, D_MODEL, fc), wcol),
                  pl.BlockSpec((1, D_MODEL, fc), wcol),
                  pl.BlockSpec((1, fc, D_MODEL), wrow)],
        out_specs=pl.BlockSpec((tr, D_MODEL), lambda j, c, te, nu: (j, 0)),
    )
    return pl.pallas_call(
        _moe_kernel,
        grid_spec=grid_spec,
        out_shape=jax.ShapeDtypeStruct((r, D_MODEL), F32),
        compiler_params=_cparams(("arbitrary", "arbitrary")),
        name="moe_grouped",
    )(tile_expert, n_used, xs, gate, wg, wu, wd)


def _route_plan(cw):
    t = cw.shape[0]
    tr = TR_MOE
    n_tiles = (2 * t) // tr + N_EXPERTS
    sel = cw > 0.0
    cnt = sel.astype(jnp.int32)
    pos = jnp.cumsum(cnt, axis=0) - cnt
    counts = jnp.sum(cnt, axis=0)
    tiles_per = (counts + tr - 1) // tr
    tile_end = jnp.cumsum(tiles_per)
    n_used = tile_end[-1]
    row = (tile_end - tiles_per)[None, :] * tr + pos
    row = jnp.where(sel, row, n_tiles * tr)
    tile_ids = jnp.minimum(jnp.arange(n_tiles, dtype=jnp.int32), n_used - 1)
    tile_expert = jnp.sum((tile_ids[:, None] >= tile_end[None, :]).astype(jnp.int32), axis=1)
    tok = jnp.broadcast_to(jnp.arange(t, dtype=jnp.int32)[:, None], row.shape)
    row_token = jnp.zeros((n_tiles * tr,), jnp.int32).at[row.reshape(-1)].set(tok.reshape(-1), mode="drop")
    row_gate = jnp.zeros((n_tiles * tr,), F32).at[row.reshape(-1)].set(cw.reshape(-1), mode="drop")
    return row, row_token, row_gate, tile_expert.astype(jnp.int32), n_used.astype(jnp.int32).reshape(1)


def _rope_tables(positions, dim, theta):
    inv = 1.0 / (theta ** (jnp.arange(0, dim, 2, dtype=F32) / dim))
    ang = positions.astype(F32).reshape(-1, 1) * inv
    return jnp.cos(ang), jnp.sin(ang)


def _tables_a(positions):
    cos, sin = _rope_tables(positions, A_ROPE, A_THETA)
    t = cos.shape[0]
    one, z16, z32 = jnp.ones((t, A_NOPE), F32), jnp.zeros((t, 16), F32), jnp.zeros((t, 32), F32)
    z64 = jnp.zeros((t, 64), F32)
    return (jnp.concatenate([one, cos, cos, z32], 1), jnp.concatenate([z64, -sin, z16, z32], 1),
            jnp.concatenate([z64, z16, sin, z32], 1))


def _tables_c(positions):
    cos, sin = _rope_tables(positions, C_ROPE, ROPE_THETA)
    t = cos.shape[0]
    one, z8, z48 = jnp.ones((t, 48), F32), jnp.zeros((t, 8), F32), jnp.zeros((t, 48), F32)
    two = lambda a: jnp.concatenate([a, a], 1)
    return (two(jnp.concatenate([cos, cos, one], 1)), two(jnp.concatenate([-sin, z8, z48], 1)),
            two(jnp.concatenate([z8, sin, z48], 1)))


def _band_bias(rel_table):
    r = np.arange(TQ_BAND)[None, :]
    j = np.arange(BAND_KEYS)[:, None]
    dist = r + B_LEFT_CHUNKS * CHUNK - j
    cdiff = r // CHUNK + B_LEFT_CHUNKS - j // CHUNK
    allowed = (cdiff >= 0) & (cdiff <= B_LEFT_CHUNKS)
    rel = np.clip(dist, -B_REL_CLIP, B_REL_CLIP) + B_REL_CLIP
    bias = rel_table[:, rel].astype(F32) * LOG2E
    return jnp.where(jnp.asarray(allowed)[None], bias, NEG)


def _pad_last(a, width):
    return jnp.pad(a, [(0, 0)] * (a.ndim - 1) + [(0, width - a.shape[-1])])


def kernel(x, positions, ev_norm_mix, ev_w_in, ev_g_cq, ev_g_ckv, ev_w_uq, ev_w_ukv, ev_a_qnorm, ev_a_knorm, ev_b_qnorm, ev_b_knorm, ev_b_rel_bias, ev_w_out, ev_norm_ff, ev_ff_gate, ev_ff_up, ev_ff_down, od_norm_mix, od_w_qkv, od_c_qnorm, od_c_knorm, od_lam_q1, od_lam_k1, od_lam_q2, od_lam_k2, od_c_subnorm, od_w_out, od_norm_ff, od_router, od_ex_gate, od_ex_up, od_ex_down):
    batch, seq, d = x.shape
    t = batch * seq
    depth = ev_w_in.shape[0] + od_w_qkv.shape[0]
    h = x.reshape(t, d)
    row = lambda a: a.reshape(1, -1).astype(F32)
    tab_a = _tables_a(positions)
    tab_c = _tables_c(positions)

    for layer in range(depth):
        i = layer // 2
        if layer % 2 == 0:
            w = ev_w_in[i]
            o1, o2, o3 = A_Q_LORA, A_Q_LORA + A_KV_LORA, A_Q_LORA + A_KV_LORA + A_ROPE
            zc = lambda n: jnp.zeros((d, n), F32)
            bw = B_HEADS * B_DH
            win = jnp.concatenate([w[:, :o2], zc(64), w[:, o2:o3], zc(32), w[:, o3:o3 + 2 * bw]], 1).astype(BF16)
            wvbt = w[:, o3 + 2 * bw:].T.astype(BF16)
            wuq = _pad_last(ev_w_uq[i], LANES).reshape(A_Q_LORA, A_HEADS * LANES).astype(BF16)
            wuk = _pad_last(ev_w_ukv[i][:, :, :A_NOPE], LANES).reshape(A_KV_LORA, A_HEADS * LANES).astype(BF16)
            wuvt = ev_w_ukv[i][:, :, A_NOPE:].reshape(A_KV_LORA, A_HEADS * A_V).T.astype(BF16)
            sc_a = (A_NOPE + A_ROPE) ** -0.5 * LOG2E
            gq = jnp.tile(_pad_last(ev_a_qnorm[i], LANES), A_HEADS).reshape(1, -1) * sc_a
            gk = jnp.tile(_pad_last(ev_a_knorm[i][:A_NOPE], LANES), A_HEADS).reshape(1, -1)
            gkr = jnp.concatenate([jnp.zeros((A_NOPE,), F32), ev_a_knorm[i][A_NOPE:],
                                   jnp.zeros((32,), F32)]).reshape(1, -1)
            gbq = jnp.tile(ev_b_qnorm[i], B_HEADS).reshape(1, -1) * (B_DH ** -0.5 * LOG2E)
            gbk = jnp.tile(ev_b_knorm[i], B_HEADS).reshape(1, -1)
            qa, ka, vat, qb, kb, vbt = _proj_even(
                h, row(ev_norm_mix[i]), win, wvbt, row(ev_g_cq[i]), row(ev_g_ckv[i]), wuq, wuk, wuvt,
                gq, gk, gkr, gbq, gbk, *tab_a)
            ya = _attn_causal(_attn_mla_kernel, qa, ka, vat, (), batch=batch, seq=seq,
                              n_blocks=A_HEADS // 2, qk_width=2 * LANES)
            yb = _attn_band(qb, kb, vbt, _band_bias(ev_b_rel_bias[i]), batch=batch, seq=seq)
            wo = ev_w_out[i].astype(BF16)
            na = A_HEADS * A_V
            padf = lambda a: _pad_last(a, FF_DENSE_PAD).astype(BF16)
            wd = jnp.pad(ev_ff_down[i], ((0, FF_DENSE_PAD - FF_DENSE), (0, 0))).astype(BF16)
            h = _out_ffn(h, ya, yb, wo[:na], wo[na:], row(ev_norm_ff[i]),
                         padf(ev_ff_gate[i]), padf(ev_ff_up[i]), wd)
        else:
            lam_init = 0.8 - 0.6 * math.exp(-0.3 * layer)
            gq = jnp.tile(od_c_qnorm[i], 2).reshape(1, -1) * (C_DH ** -0.5 * LOG2E)
            gk = jnp.tile(od_c_knorm[i], 2).reshape(1, -1)
            c_mix = C_HEADS * 2 * C_DH
            wqkv = od_w_qkv[i]
            q, k, v = _proj_odd(h, row(od_norm_mix[i]), wqkv[:, :2 * c_mix].astype(BF16),
                                wqkv[:, 2 * c_mix:].T.astype(BF16), gq, gk, *tab_c)
            extras = (row(od_lam_q1[i]), row(od_lam_k1[i]), row(od_lam_q2[i]), row(od_lam_k2[i]),
                      row(od_c_subnorm[i]))
            yc = _attn_causal(functools.partial(_attn_diff_kernel, lam_init=lam_init), q, k, v, extras,
                              batch=batch, seq=seq, n_blocks=C_HEADS, qk_width=LANES)
            h1, xn, cw = _out_router(h, yc, od_w_out[i].astype(BF16), row(od_norm_ff[i]),
                                     od_router[i].T.astype(F32))
            rows, row_token, row_gate, tile_expert, n_used = _route_plan(cw[:, :N_EXPERTS])
            xs = jnp.take(xn, row_token, axis=0)
            y = _moe(tile_expert, n_used, xs, row_gate.reshape(-1, 1), od_ex_gate[i], od_ex_up[i],
                     od_ex_down[i])
            y = jnp.concatenate([y, jnp.zeros((8, d), F32)], 0)
            two = jnp.sort(rows, axis=1)[:, :2]
            h = h1 + jnp.sum(jnp.take(y, two, axis=0), axis=1)
    return h.reshape(batch, seq, d)
```

```python
import functools
import math

import numpy as np
import jax
import jax.numpy as jnp
from jax import lax
from jax.experimental import pallas as pl
from jax.experimental.pallas import tpu as pltpu

F32 = jnp.float32
BF16 = jnp.bfloat16

D_MODEL = 1024
CHUNK = 64
EPS = 1e-6
LOG2E = 1.4426950408889634
NEG = -1e30

A_HEADS = 8
A_NOPE = 64
A_ROPE = 32
A_V = 64
A_Q_LORA = 256
A_KV_LORA = 128
A_THETA = 10000.0
B_HEADS = 8
B_DH = 64
B_LEFT_CHUNKS = 8
B_REL_CLIP = 128
C_HEADS = 8
C_DH = 64
C_ROPE = C_DH // 4
ROPE_THETA = 500000.0
FF_DENSE = 2752
N_EXPERTS = 8
FF_EXPERT = 3584

LANES = 128
FF_DENSE_PAD = 2816
VMEM_LIMIT = 56 * 1024 * 1024

TM_PROJ = 512
TQ_CAUSAL = 256
TQ_BAND = 128
BAND_KEYS = TQ_BAND + B_LEFT_CHUNKS * CHUNK
TM_DISPATCH = 256
TR_MOE = 1024
FC_MOE = 896


def _cparams(sem):
    return pltpu.CompilerParams(dimension_semantics=sem, vmem_limit_bytes=VMEM_LIMIT)


def _const_spec(shape):
    nd = len(shape)
    return pl.BlockSpec(shape, lambda *_: (0,) * nd, pipeline_mode=pl.Buffered(1))


def _rms_scale(x):
    return lax.rsqrt(jnp.mean(x * x, axis=-1, keepdims=True) + EPS)


def _lane_iota():
    return lax.broadcasted_iota(jnp.int32, (1, LANES), 1)


def _dot_nt(a, b):
    return lax.dot_general(a, b, (((1,), (1,)), ((), ())), preferred_element_type=F32)


def _proj_even_kernel(h_ref, gmix_ref, win_ref, wvbt_ref, gcq_ref, gckv_ref, wuq_ref, wuk_ref, wuvt_ref,
                      gq_ref, gk_ref, gkr_ref, gbq_ref, gbk_ref, cos_ref, sa_ref, sb_ref,
                      qa_ref, ka_ref, vat_ref, qb_ref, kb_ref, vbt_ref):
    x = h_ref[...]
    hn = (x * _rms_scale(x) * gmix_ref[...]).astype(BF16)
    u = jnp.dot(hn, win_ref[...], preferred_element_type=F32)
    vbt_ref[...] = _dot_nt(wvbt_ref[...], hn).astype(BF16)
    lane = _lane_iota()
    lo = lane < 64
    cos_t = cos_ref[...]
    sin_a = sa_ref[...]
    sin_b = sb_ref[...]

    def rope(xg):
        return (xg * cos_t + pltpu.roll(xg, LANES - A_ROPE // 2, 1) * sin_a
                + pltpu.roll(xg, A_ROPE // 2, 1) * sin_b)

    cq = u[:, 0:A_Q_LORA]
    cqn = (cq * _rms_scale(cq) * gcq_ref[...]).astype(BF16)
    q = jnp.dot(cqn, wuq_ref[...], preferred_element_type=F32)
    ckv = u[:, A_Q_LORA:A_Q_LORA + A_KV_LORA]
    ckvn = (ckv * _rms_scale(ckv) * gckv_ref[...]).astype(BF16)
    kn = jnp.dot(ckvn, wuk_ref[...], preferred_element_type=F32)
    vat_ref[...] = _dot_nt(wuvt_ref[...], ckvn).astype(BF16)

    kr = u[:, 384:512]
    kr_s = lax.rsqrt(jnp.sum(kr * kr, axis=-1, keepdims=True) * (1.0 / A_ROPE) + EPS)
    krot = rope(kr * kr_s * gkr_ref[...])

    for hd in range(A_HEADS):
        sl = slice(hd * LANES, (hd + 1) * LANES)
        blk = q[:, sl]
        sq = blk * blk
        tot = jnp.sum(sq, axis=-1, keepdims=True)
        ssn = jnp.sum(jnp.where(lo, sq, 0.0), axis=-1, keepdims=True)
        sc = jnp.where(lo, lax.rsqrt(ssn * (1.0 / A_NOPE) + EPS),
                       lax.rsqrt((tot - ssn) * (1.0 / A_ROPE) + EPS))
        qa_ref[:, sl] = rope(blk * sc * gq_ref[:, sl]).astype(BF16)
        kb_ = kn[:, sl]
        ks = lax.rsqrt(jnp.sum(kb_ * kb_, axis=-1, keepdims=True) * (1.0 / A_NOPE) + EPS)
        ka_ref[:, sl] = (kb_ * ks * gk_ref[:, sl] + krot).astype(BF16)

    def pair_norm(blk, g):
        sq = blk * blk
        tot = jnp.sum(sq, axis=-1, keepdims=True)
        s_lo = jnp.sum(jnp.where(lo, sq, 0.0), axis=-1, keepdims=True)
        sc = jnp.where(lo, lax.rsqrt(s_lo * (1.0 / B_DH) + EPS),
                       lax.rsqrt((tot - s_lo) * (1.0 / B_DH) + EPS))
        return (blk * sc * g).astype(BF16)

    for p in range(B_HEADS // 2):
        sl = slice(p * LANES, (p + 1) * LANES)
        qb_ref[:, sl] = pair_norm(u[:, 512 + p * LANES:512 + (p + 1) * LANES], gbq_ref[:, sl])
        kb_ref[:, sl] = pair_norm(u[:, 1024 + p * LANES:1024 + (p + 1) * LANES], gbk_ref[:, sl])


def _proj_even(h, gmix, win, wvbt, gcq, gckv, wuq, wuk, wuvt, gq, gk, gkr, gbq, gbk, cos_t, sin_a, sin_b):
    t = h.shape[0]
    tm = TM_PROJ
    row = lambda w: pl.BlockSpec((tm, w), lambda i: (i, 0))
    col = lambda w: pl.BlockSpec((w, tm), lambda i: (0, i))
    rshape = lambda w: jax.ShapeDtypeStruct((t, w), BF16)
    cshape = lambda w: jax.ShapeDtypeStruct((w, t), BF16)
    consts = (gmix, win, wvbt, gcq, gckv, wuq, wuk, wuvt, gq, gk, gkr, gbq, gbk)
    return pl.pallas_call(
        _proj_even_kernel,
        grid=(t // tm,),
        in_specs=[row(D_MODEL)] + [_const_spec(c.shape) for c in consts] + [row(LANES)] * 3,
        out_specs=[row(1024), row(1024), col(512), row(512), row(512), col(512)],
        out_shape=[rshape(1024), rshape(1024), cshape(512), rshape(512), rshape(512), cshape(512)],
        compiler_params=_cparams(("parallel",)),
        name="proj_even",
    )(h, *consts, cos_t, sin_a, sin_b)


def _causal_streams(q_ref, k_ref, vt_ref, c, *, split):
    tq = TQ_CAUSAL
    q = q_ref[c * tq:(c + 1) * tq, :]
    lo = _lane_iota() < 64
    if split:
        qs = [q[:, :LANES], q[:, LANES:]]
    else:
        zero = jnp.zeros_like(q)
        qs = [jnp.where(lo, q, zero), jnp.where(lo, zero, q)]
    kchunk = lax.broadcasted_iota(jnp.int32, (tq, tq), 0) // CHUNK
    qchunk = lax.broadcasted_iota(jnp.int32, (tq, tq), 1) // CHUNK
    allowed = kchunk <= qchunk
    n_past = c * tq
    outs = []
    for m in range(2):
        ksl = slice(m * LANES, (m + 1) * LANES) if split else slice(None)
        s_d = jnp.where(allowed, _dot_nt(k_ref[n_past:n_past + tq, ksl], qs[m]), NEG)
        mx = jnp.max(s_d, axis=0, keepdims=True)
        if c:
            s_p = _dot_nt(k_ref[0:n_past, ksl], qs[m])
            mx = jnp.maximum(mx, jnp.max(s_p, axis=0, keepdims=True))
        p_d = jnp.exp2(s_d - mx)
        l = jnp.sum(p_d, axis=0, keepdims=True)
        o = jnp.dot(vt_ref[:, n_past:n_past + tq], p_d.astype(BF16), preferred_element_type=F32)
        if c:
            p_p = jnp.exp2(s_p - mx)
            l = l + jnp.sum(p_p, axis=0, keepdims=True)
            o = o + jnp.dot(vt_ref[:, 0:n_past], p_p.astype(BF16), preferred_element_type=F32)
        outs.append(o * (1.0 / l))
    return outs


def _row_iota(n):
    return lax.broadcasted_iota(jnp.int32, (LANES, n), 0)


def _attn_mla_kernel(q_ref, k_ref, vt_ref, o_ref):
    tq = TQ_CAUSAL
    for c in range(q_ref.shape[0] // tq):
        o0, o1 = _causal_streams(q_ref, k_ref, vt_ref, c, split=True)
        o_ref[c * tq:(c + 1) * tq, :] = jnp.where(_row_iota(tq) < 64, o0, o1).T.astype(o_ref.dtype)


def _attn_diff_kernel(q_ref, k_ref, vt_ref, lq1_ref, lk1_ref, lq2_ref, lk2_ref, gsub_ref, o_ref,
                      *, lam_init):
    tq = TQ_CAUSAL
    lam = (jnp.exp(jnp.sum(lq1_ref[...] * lk1_ref[...], axis=-1, keepdims=True))
           - jnp.exp(jnp.sum(lq2_ref[...] * lk2_ref[...], axis=-1, keepdims=True)) + lam_init)
    for c in range(q_ref.shape[0] // tq):
        o0, o1 = _causal_streams(q_ref, k_ref, vt_ref, c, split=False)
        o = (o0 - lam * o1).T
        o = o * _rms_scale(o) * gsub_ref[...] * (1.0 - lam_init)
        o_ref[c * tq:(c + 1) * tq, :] = o.astype(o_ref.dtype)


def _attn_causal(kernel, q, k, vt, extras, *, batch, seq, n_blocks, qk_width):
    in_specs = [pl.BlockSpec((seq, qk_width), lambda b, p: (b, p)),
                pl.BlockSpec((seq, qk_width), lambda b, p: (b, p)),
                pl.BlockSpec((LANES, seq), lambda b, p: (p, b))]
    in_specs += [_const_spec(e.shape) for e in extras]
    return pl.pallas_call(
        kernel,
        grid=(batch, n_blocks),
        in_specs=in_specs,
        out_specs=pl.BlockSpec((seq, LANES), lambda b, p: (b, p)),
        out_shape=jax.ShapeDtypeStruct((batch * seq, n_blocks * LANES), BF16),
        compiler_params=_cparams(("parallel", "parallel")),
        name=kernel.func.__name__ if isinstance(kernel, functools.partial) else kernel.__name__,
    )(q, k, vt, *extras)


def _attn_band_kernel(q_ref, k_ref, vt_ref, bias_ref, o_ref):
    tq = TQ_BAND
    i = pl.program_id(2)
    q = q_ref[...]
    lo = _lane_iota() < 64
    zero = jnp.zeros_like(q)
    qs = [jnp.where(lo, q, zero), jnp.where(lo, zero, q)]

    def window(start, width):
        kt = k_ref[pl.ds(start, width), :]
        vt = vt_ref[:, pl.ds(start, width)]
        outs = []
        for m in range(2):
            s = lax.dot_general(kt, qs[m], (((1,), (1,)), ((), ())), preferred_element_type=F32)
            s = s + bias_ref[m, BAND_KEYS - width:, :]
            p = jnp.exp2(s - jnp.max(s, axis=0, keepdims=True))
            o = jnp.dot(vt, p.astype(BF16), preferred_element_type=F32)
            outs.append(o * (1.0 / jnp.sum(p, axis=0, keepdims=True)))
        o_ref[...] = jnp.where(_row_iota(tq) < 64, outs[0], outs[1]).T.astype(o_ref.dtype)

    n_edge = (BAND_KEYS - tq) // tq
    for e in range(n_edge):
        @pl.when(i == e)
        def _(e=e):
            window(0, (e + 1) * tq)

    @pl.when(i >= n_edge)
    def _():
        window(pl.multiple_of(i * tq - (BAND_KEYS - tq), tq), BAND_KEYS)


def _attn_band(q, k, vt, bias, *, batch, seq):
    tq = TQ_BAND
    nq = seq // tq
    npair = B_HEADS // 2
    return pl.pallas_call(
        _attn_band_kernel,
        grid=(batch, npair, nq),
        in_specs=[pl.BlockSpec((tq, LANES), lambda b, p, i: (b * nq + i, p)),
                  pl.BlockSpec((seq, LANES), lambda b, p, i: (b, p)),
                  pl.BlockSpec((LANES, seq), lambda b, p, i: (p, b)),
                  pl.BlockSpec((2, BAND_KEYS, tq), lambda b, p, i: (p, 0, 0))],
        out_specs=pl.BlockSpec((tq, LANES), lambda b, p, i: (b * nq + i, p)),
        out_shape=jax.ShapeDtypeStruct((batch * seq, npair * LANES), BF16),
        compiler_params=_cparams(("parallel", "parallel", "arbitrary")),
        name="attn_band",
    )(q, k, vt, bias)


def _silu(g):
    return g / (1.0 + jnp.exp(-g))


def _out_ffn_kernel(h_ref, ya_ref, yb_ref, woa_ref, wob_ref, gff_ref, wg_ref, wu_ref, wd_ref, o_ref):
    h1 = (h_ref[...] + jnp.dot(ya_ref[...], woa_ref[...], preferred_element_type=F32)
          + jnp.dot(yb_ref[...], wob_ref[...], preferred_element_type=F32))
    xn = (h1 * _rms_scale(h1) * gff_ref[...]).astype(BF16)
    g = jnp.dot(xn, wg_ref[...], preferred_element_type=F32)
    u = jnp.dot(xn, wu_ref[...], preferred_element_type=F32)
    a = (_silu(g) * u).astype(BF16)
    o_ref[...] = h1 + jnp.dot(a, wd_ref[...], preferred_element_type=F32)


def _out_ffn(h, ya, yb, woa, wob, gff, wg, wu, wd):
    t = h.shape[0]
    tm = TM_PROJ
    row = lambda w: pl.BlockSpec((tm, w), lambda i: (i, 0))
    return pl.pallas_call(
        _out_ffn_kernel,
        grid=(t // tm,),
        in_specs=[row(D_MODEL), row(ya.shape[1]), row(yb.shape[1])]
        + [_const_spec(w.shape) for w in (woa, wob, gff, wg, wu, wd)],
        out_specs=row(D_MODEL),
        out_shape=jax.ShapeDtypeStruct((t, D_MODEL), F32),
        compiler_params=_cparams(("parallel",)),
        name="out_ffn",
    )(h, ya, yb, woa, wob, gff, wg, wu, wd)


def _proj_odd_kernel(h_ref, gmix_ref, w_ref, wvt_ref, gq_ref, gk_ref, cos_ref, sa_ref, sb_ref,
                     q_ref, k_ref, vt_ref):
    x = h_ref[...]
    hn = (x * _rms_scale(x) * gmix_ref[...]).astype(BF16)
    u = jnp.dot(hn, w_ref[...], preferred_element_type=F32)
    vt_ref[...] = _dot_nt(wvt_ref[...], hn).astype(BF16)
    lo = _lane_iota() < 64
    cos_t = cos_ref[...]
    sin_a = sa_ref[...]
    sin_b = sb_ref[...]
    width = C_HEADS * 2 * C_DH

    def norm_rope(blk, g):
        sq = blk * blk
        tot = jnp.sum(sq, axis=-1, keepdims=True)
        s_lo = jnp.sum(jnp.where(lo, sq, 0.0), axis=-1, keepdims=True)
        sc = jnp.where(lo, lax.rsqrt(s_lo * (1.0 / C_DH) + EPS),
                       lax.rsqrt((tot - s_lo) * (1.0 / C_DH) + EPS))
        xg = blk * sc * g
        return (xg * cos_t + pltpu.roll(xg, LANES - C_ROPE // 2, 1) * sin_a
                + pltpu.roll(xg, C_ROPE // 2, 1) * sin_b).astype(BF16)

    for hd in range(C_HEADS):
        sl = slice(hd * LANES, (hd + 1) * LANES)
        q_ref[:, sl] = norm_rope(u[:, hd * LANES:(hd + 1) * LANES], gq_ref[...])
        k_ref[:, sl] = norm_rope(u[:, width + hd * LANES:width + (hd + 1) * LANES], gk_ref[...])


def _proj_odd(h, gmix, w, wvt, gq, gk, cos_t, sin_a, sin_b):
    t = h.shape[0]
    tm = TM_PROJ
    row = lambda wd: pl.BlockSpec((tm, wd), lambda i: (i, 0))
    width = C_HEADS * 2 * C_DH
    consts = (gmix, w, wvt, gq, gk)
    return pl.pallas_call(
        _proj_odd_kernel,
        grid=(t // tm,),
        in_specs=[row(D_MODEL)] + [_const_spec(c.shape) for c in consts] + [row(LANES)] * 3,
        out_specs=[row(width), row(width), pl.BlockSpec((width, tm), lambda i: (0, i))],
        out_shape=[jax.ShapeDtypeStruct((t, width), BF16), jax.ShapeDtypeStruct((t, width), BF16),
                   jax.ShapeDtypeStruct((width, t), BF16)],
        compiler_params=_cparams(("parallel",)),
        name="proj_odd",
    )(h, *consts, cos_t, sin_a, sin_b)


def _out_router_kernel(h_ref, y_ref, wo_ref, gff_ref, wr_ref, h1_ref, xn_ref, idx_ref, gate_ref):
    h1 = h_ref[...] + jnp.dot(y_ref[...], wo_ref[...], preferred_element_type=F32)
    h1_ref[...] = h1
    xn = h1 * _rms_scale(h1) * gff_ref[...]
    xn_ref[...] = xn
    logits = [jnp.sum(xn * wr_ref[e:e + 1, :], axis=-1, keepdims=True) for e in range(N_EXPERTS)]
    m1 = functools.reduce(jnp.maximum, logits)
    idx1 = functools.reduce(jnp.minimum,
                            [jnp.where(logits[e] == m1, e, N_EXPERTS) for e in range(N_EXPERTS)])
    rest = [jnp.where(idx1 == e, -jnp.inf, logits[e]) for e in range(N_EXPERTS)]
    m2 = functools.reduce(jnp.maximum, rest)
    idx2 = functools.reduce(jnp.minimum,
                            [jnp.where(rest[e] == m2, e, N_EXPERTS) for e in range(N_EXPERTS)])
    e2 = jnp.exp(m2 - m1)
    g1 = 1.0 / (1.0 + e2)
    g2 = e2 * g1
    lane = _lane_iota()
    idx_ref[...] = jnp.where(lane == 0, idx1, jnp.where(lane == 1, idx2, 0))
    gate_ref[...] = jnp.where(lane == 0, g1, jnp.where(lane == 1, g2, 0.0))


def _out_router(h, y, wo, gff, wr):
    t = h.shape[0]
    tm = TM_PROJ
    row = lambda w: pl.BlockSpec((tm, w), lambda i: (i, 0))
    return pl.pallas_call(
        _out_router_kernel,
        grid=(t // tm,),
        in_specs=[row(D_MODEL), row(y.shape[1]), _const_spec(wo.shape), _const_spec(gff.shape),
                  _const_spec(wr.shape)],
        out_specs=[row(D_MODEL), row(D_MODEL), row(LANES), row(LANES)],
        out_shape=[jax.ShapeDtypeStruct((t, D_MODEL), F32), jax.ShapeDtypeStruct((t, D_MODEL), F32),
                   jax.ShapeDtypeStruct((t, LANES), jnp.int32), jax.ShapeDtypeStruct((t, LANES), F32)],
        compiler_params=_cparams(("parallel",)),
        name="out_router",
    )(h, y, wo, gff, wr)


def _row_copy(src_ref, src_row, dst_ref, dst_row, sem):
    return pltpu.make_async_copy(src_ref.at[pl.ds(src_row, 1)], dst_ref.at[pl.ds(dst_row, 1)], sem)


def _dispatch_kernel(row_ref, x_ref, xs_ref, sem):
    tm = x_ref.shape[0]
    base = pl.program_id(0) * tm

    def issue(r, carry):
        for k in range(2):
            _row_copy(x_ref, r, xs_ref, row_ref[2 * (base + r) + k], sem).start()
        return carry

    lax.fori_loop(0, tm, issue, 0)

    def drain(r, carry):
        for k in range(2):
            _row_copy(x_ref, r, xs_ref, row_ref[2 * (base + r) + k], sem).wait()
        return carry

    lax.fori_loop(0, tm, drain, 0)


def _dispatch(row_of_pair, xn):
    t, d = xn.shape
    tm = TM_DISPATCH
    grid_spec = pltpu.PrefetchScalarGridSpec(
        num_scalar_prefetch=1,
        grid=(t // tm,),
        in_specs=[pl.BlockSpec((tm, d), lambda i, rows: (i, 0))],
        out_specs=pl.BlockSpec(memory_space=pl.ANY),
        scratch_shapes=[pltpu.SemaphoreType.DMA(())],
    )
    return pl.pallas_call(
        _dispatch_kernel,
        grid_spec=grid_spec,
        out_shape=jax.ShapeDtypeStruct((2 * t, d), xn.dtype),
        compiler_params=_cparams(("arbitrary",)),
        name="moe_dispatch",
    )(row_of_pair, xn)


def _moe_kernel(vt_ref, ve_ref, lo_ref, hi_ref, first_ref, x_ref, wg_ref, wu_ref, wd_ref, o_ref):
    v = pl.program_id(0)
    c = pl.program_id(1)
    tr = x_ref.shape[0]

    @pl.when((c == 0) & (first_ref[v] == 1))
    def _():
        o_ref[...] = jnp.zeros_like(o_ref)

    @pl.when(lo_ref[v] < hi_ref[v])
    def _():
        x = x_ref[...].astype(BF16)
        g = jnp.dot(x, wg_ref[0].astype(BF16), preferred_element_type=F32)
        u = jnp.dot(x, wu_ref[0].astype(BF16), preferred_element_type=F32)
        a = (_silu(g) * u).astype(BF16)
        y = jnp.dot(a, wd_ref[0].astype(BF16), preferred_element_type=F32)
        rows = vt_ref[v] * tr + lax.broadcasted_iota(jnp.int32, (tr, 1), 0)
        mine = (rows >= lo_ref[v]) & (rows < hi_ref[v])
        o_ref[...] += jnp.where(mine, y, 0.0)


def _moe(visits, xs, wg, wu, wd):
    r = xs.shape[0]
    tr, fc = TR_MOE, FC_MOE
    nc = FF_EXPERT // fc
    n_visits = r // tr + N_EXPERTS - 1

    def chunk(v, c, lo, hi):
        return jnp.where(lo[v] < hi[v], c, nc - 1)

    tile = lambda v, c, vt, ve, lo, hi, first: (vt[v], 0)
    wcol = lambda v, c, vt, ve, lo, hi, first: (ve[v], 0, chunk(v, c, lo, hi))
    wrow = lambda v, c, vt, ve, lo, hi, first: (ve[v], chunk(v, c, lo, hi), 0)
    grid_spec = pltpu.PrefetchScalarGridSpec(
        num_scalar_prefetch=5,
        grid=(n_visits, nc),
        in_specs=[pl.BlockSpec((tr, D_MODEL), tile),
                  pl.BlockSpec((1, D_MODEL, fc), wcol),
                  pl.BlockSpec((1, D_MODEL, fc), wcol),
                  pl.BlockSpec((1, fc, D_MODEL), wrow)],
        out_specs=pl.BlockSpec((tr, D_MODEL), tile),
    )
    return pl.pallas_call(
        _moe_kernel,
        grid_spec=grid_spec,
        out_shape=jax.ShapeDtypeStruct((r, D_MODEL), F32),
        compiler_params=_cparams(("arbitrary", "arbitrary")),
        name="moe_grouped",
    )(*visits, xs, wg, wu, wd)


def _combine_kernel(row_ref, h_ref, gate_ref, ys_ref, o_ref, buf, sem):
    tm = h_ref.shape[0]
    base = pl.program_id(0) * tm

    def issue(r, carry):
        for k in range(2):
            _row_copy(ys_ref, row_ref[2 * (base + r) + k], buf.at[k], r, sem).start()
        return carry

    lax.fori_loop(0, tm, issue, 0)

    def drain(r, carry):
        for k in range(2):
            _row_copy(ys_ref, row_ref[2 * (base + r) + k], buf.at[k], r, sem).wait()
        return carry

    lax.fori_loop(0, tm, drain, 0)
    gate = gate_ref[...]
    o_ref[...] = h_ref[...] + gate[:, 0:1] * buf[0] + gate[:, 1:2] * buf[1]


def _combine(row_of_pair, h1, gates, ys):
    t, d = h1.shape
    tm = TM_DISPATCH
    grid_spec = pltpu.PrefetchScalarGridSpec(
        num_scalar_prefetch=1,
        grid=(t // tm,),
        in_specs=[pl.BlockSpec((tm, d), lambda i, rows: (i, 0)),
                  pl.BlockSpec((tm, LANES), lambda i, rows: (i, 0)),
                  pl.BlockSpec(memory_space=pl.ANY)],
        out_specs=pl.BlockSpec((tm, d), lambda i, rows: (i, 0)),
        scratch_shapes=[pltpu.VMEM((2, tm, d), F32), pltpu.SemaphoreType.DMA(())],
    )
    return pl.pallas_call(
        _combine_kernel,
        grid_spec=grid_spec,
        out_shape=jax.ShapeDtypeStruct((t, d), F32),
        compiler_params=_cparams(("arbitrary",)),
        name="moe_combine",
    )(row_of_pair, h1, gates, ys)


def _route_plan(idx):
    n_pairs = idx.size
    tr = TR_MOE
    n_tiles = n_pairs // tr
    n_visits = n_tiles + N_EXPERTS - 1
    e_flat = idx.reshape(-1)
    onehot = (e_flat[:, None] == jnp.arange(N_EXPERTS, dtype=jnp.int32)[None, :]).astype(jnp.int32)
    csum = jnp.cumsum(onehot, axis=0)
    counts = csum[-1]
    ends = jnp.cumsum(counts)
    starts = ends - counts
    pos = jnp.sum((csum - onehot) * onehot, axis=1)
    row_of_pair = (starts[e_flat] + pos).astype(jnp.int32)
    nv = jnp.where(counts > 0, (ends - 1) // tr - starts // tr + 1, 0)
    vend = jnp.cumsum(nv)
    vid = jnp.arange(n_visits, dtype=jnp.int32)
    live = vid < vend[-1]
    ve = jnp.minimum(jnp.sum((vid[:, None] >= vend[None, :]).astype(jnp.int32), axis=1), N_EXPERTS - 1)
    vt = jnp.where(live, starts[ve] // tr + vid - (vend - nv)[ve], n_tiles - 1)
    lo = jnp.where(live, jnp.maximum(starts[ve], vt * tr), 0)
    hi = jnp.where(live, jnp.minimum(ends[ve], (vt + 1) * tr), 0)
    last_e = jnp.max(jnp.where(counts > 0, jnp.arange(N_EXPERTS, dtype=jnp.int32), 0))
    ve = jnp.where(live, ve, last_e)
    first = (live & (lo == vt * tr)).astype(jnp.int32)
    i32 = lambda a: a.astype(jnp.int32)
    return row_of_pair, (i32(vt), i32(ve), i32(lo), i32(hi), first)


def _rope_tables(positions, dim, theta):
    inv = 1.0 / (theta ** (jnp.arange(0, dim, 2, dtype=F32) / dim))
    ang = positions.astype(F32).reshape(-1, 1) * inv
    return jnp.cos(ang), jnp.sin(ang)


def _tables_a(positions):
    cos, sin = _rope_tables(positions, A_ROPE, A_THETA)
    t = cos.shape[0]
    one, z16, z32 = jnp.ones((t, A_NOPE), F32), jnp.zeros((t, 16), F32), jnp.zeros((t, 32), F32)
    z64 = jnp.zeros((t, 64), F32)
    return (jnp.concatenate([one, cos, cos, z32], 1), jnp.concatenate([z64, -sin, z16, z32], 1),
            jnp.concatenate([z64, z16, sin, z32], 1))


def _tables_c(positions):
    cos, sin = _rope_tables(positions, C_ROPE, ROPE_THETA)
    t = cos.shape[0]
    one, z8, z48 = jnp.ones((t, 48), F32), jnp.zeros((t, 8), F32), jnp.zeros((t, 48), F32)
    two = lambda a: jnp.concatenate([a, a], 1)
    return (two(jnp.concatenate([cos, cos, one], 1)), two(jnp.concatenate([-sin, z8, z48], 1)),
            two(jnp.concatenate([z8, sin, z48], 1)))


def _band_bias(rel_table):
    r = np.arange(TQ_BAND)[None, :]
    j = np.arange(BAND_KEYS)[:, None]
    dist = r + B_LEFT_CHUNKS * CHUNK - j
    cdiff = r // CHUNK + B_LEFT_CHUNKS - j // CHUNK
    allowed = (cdiff >= 0) & (cdiff <= B_LEFT_CHUNKS)
    rel = np.clip(dist, -B_REL_CLIP, B_REL_CLIP) + B_REL_CLIP
    bias = rel_table[:, rel].astype(F32) * LOG2E
    return jnp.where(jnp.asarray(allowed)[None], bias, NEG)


def _pad_last(a, width):
    return jnp.pad(a, [(0, 0)] * (a.ndim - 1) + [(0, width - a.shape[-1])])


def kernel(x, positions, ev_norm_mix, ev_w_in, ev_g_cq, ev_g_ckv, ev_w_uq, ev_w_ukv, ev_a_qnorm, ev_a_knorm, ev_b_qnorm, ev_b_knorm, ev_b_rel_bias, ev_w_out, ev_norm_ff, ev_ff_gate, ev_ff_up, ev_ff_down, od_norm_mix, od_w_qkv, od_c_qnorm, od_c_knorm, od_lam_q1, od_lam_k1, od_lam_q2, od_lam_k2, od_c_subnorm, od_w_out, od_norm_ff, od_router, od_ex_gate, od_ex_up, od_ex_down):
    batch, seq, d = x.shape
    t = batch * seq
    depth = ev_w_in.shape[0] + od_w_qkv.shape[0]
    h = x.reshape(t, d)
    row = lambda a: a.reshape(1, -1).astype(F32)
    tab_a = _tables_a(positions)
    tab_c = _tables_c(positions)

    for layer in range(depth):
        i = layer // 2
        if layer % 2 == 0:
            w = ev_w_in[i]
            o1, o2, o3 = A_Q_LORA, A_Q_LORA + A_KV_LORA, A_Q_LORA + A_KV_LORA + A_ROPE
            zc = lambda n: jnp.zeros((d, n), F32)
            bw = B_HEADS * B_DH
            win = jnp.concatenate([w[:, :o2], zc(64), w[:, o2:o3], zc(32), w[:, o3:o3 + 2 * bw]], 1).astype(BF16)
            wvbt = w[:, o3 + 2 * bw:].T.astype(BF16)
            wuq = _pad_last(ev_w_uq[i], LANES).reshape(A_Q_LORA, A_HEADS * LANES).astype(BF16)
            wuk = _pad_last(ev_w_ukv[i][:, :, :A_NOPE], LANES).reshape(A_KV_LORA, A_HEADS * LANES).astype(BF16)
            wuvt = ev_w_ukv[i][:, :, A_NOPE:].reshape(A_KV_LORA, A_HEADS * A_V).T.astype(BF16)
            sc_a = (A_NOPE + A_ROPE) ** -0.5 * LOG2E
            gq = jnp.tile(_pad_last(ev_a_qnorm[i], LANES), A_HEADS).reshape(1, -1) * sc_a
            gk = jnp.tile(_pad_last(ev_a_knorm[i][:A_NOPE], LANES), A_HEADS).reshape(1, -1)
            gkr = jnp.concatenate([jnp.zeros((A_NOPE,), F32), ev_a_knorm[i][A_NOPE:],
                                   jnp.zeros((32,), F32)]).reshape(1, -1)
            gbq = jnp.tile(ev_b_qnorm[i], B_HEADS).reshape(1, -1) * (B_DH ** -0.5 * LOG2E)
            gbk = jnp.tile(ev_b_knorm[i], B_HEADS).reshape(1, -1)
            qa, ka, vat, qb, kb, vbt = _proj_even(
                h, row(ev_norm_mix[i]), win, wvbt, row(ev_g_cq[i]), row(ev_g_ckv[i]), wuq, wuk, wuvt,
                gq, gk, gkr, gbq, gbk, *tab_a)
            ya = _attn_causal(_attn_mla_kernel, qa, ka, vat, (), batch=batch, seq=seq,
                              n_blocks=A_HEADS // 2, qk_width=2 * LANES)
            yb = _attn_band(qb, kb, vbt, _band_bias(ev_b_rel_bias[i]), batch=batch, seq=seq)
            wo = ev_w_out[i].astype(BF16)
            na = A_HEADS * A_V
            padf = lambda a: _pad_last(a, FF_DENSE_PAD).astype(BF16)
            wd = jnp.pad(ev_ff_down[i], ((0, FF_DENSE_PAD - FF_DENSE), (0, 0))).astype(BF16)
            h = _out_ffn(h, ya, yb, wo[:na], wo[na:], row(ev_norm_ff[i]),
                         padf(ev_ff_gate[i]), padf(ev_ff_up[i]), wd)
        else:
            lam_init = 0.8 - 0.6 * math.exp(-0.3 * layer)
            gq = jnp.tile(od_c_qnorm[i], 2).reshape(1, -1) * (C_DH ** -0.5 * LOG2E)
            gk = jnp.tile(od_c_knorm[i], 2).reshape(1, -1)
            c_mix = C_HEADS * 2 * C_DH
            wqkv = od_w_qkv[i]
            q, k, v = _proj_odd(h, row(od_norm_mix[i]), wqkv[:, :2 * c_mix].astype(BF16),
                                wqkv[:, 2 * c_mix:].T.astype(BF16), gq, gk, *tab_c)
            extras = (row(od_lam_q1[i]), row(od_lam_k1[i]), row(od_lam_q2[i]), row(od_lam_k2[i]),
                      row(od_c_subnorm[i]))
            yc = _attn_causal(functools.partial(_attn_diff_kernel, lam_init=lam_init), q, k, v, extras,
                              batch=batch, seq=seq, n_blocks=C_HEADS, qk_width=LANES)
            h1, xn, idx, gates = _out_router(h, yc, od_w_out[i].astype(BF16), row(od_norm_ff[i]),
                                             od_router[i].T.astype(F32))
            row_of_pair, visits = _route_plan(idx[:, :2])
            xs = _dispatch(row_of_pair, xn)
            ys = _moe(visits, xs, od_ex_gate[i], od_ex_up[i], od_ex_down[i])
            h = _combine(row_of_pair, h1, gates, ys)
    return h.reshape(batch, seq, d)
```

```python
import functools
import math

import numpy as np
import jax
import jax.numpy as jnp
from jax import lax
from jax.experimental import pallas as pl
from jax.experimental.pallas import tpu as pltpu

F32 = jnp.float32
BF16 = jnp.bfloat16

D_MODEL = 1024
CHUNK = 64
EPS = 1e-6
LOG2E = 1.4426950408889634
NEG = -1e30

A_HEADS = 8
A_NOPE = 64
A_ROPE = 32
A_V = 64
A_Q_LORA = 256
A_KV_LORA = 128
A_THETA = 10000.0
B_HEADS = 8
B_DH = 64
B_LEFT_CHUNKS = 8
B_REL_CLIP = 128
C_HEADS = 8
C_DH = 64
C_ROPE = C_DH // 4
ROPE_THETA = 500000.0
FF_DENSE = 2752
N_EXPERTS = 8
FF_EXPERT = 3584

LANES = 128
FF_DENSE_PAD = 2816
VMEM_LIMIT = 56 * 1024 * 1024

TM_PROJ = 512
TQ_CAUSAL = 256
TQ_BAND = 256
BAND_KEYS = TQ_BAND + B_LEFT_CHUNKS * CHUNK
TM_DISPATCH = 256
DMA_UNROLL = 8
TR_MOE = 1024
FC_MOE = 896


def _cparams(sem):
    return pltpu.CompilerParams(dimension_semantics=sem, vmem_limit_bytes=VMEM_LIMIT)


def _const_spec(shape):
    nd = len(shape)
    return pl.BlockSpec(shape, lambda *_: (0,) * nd, pipeline_mode=pl.Buffered(1))


def _rms_scale(x):
    return lax.rsqrt(jnp.mean(x * x, axis=-1, keepdims=True) + EPS)


def _lane_iota():
    return lax.broadcasted_iota(jnp.int32, (1, LANES), 1)


def _dot_nt(a, b):
    return lax.dot_general(a, b, (((1,), (1,)), ((), ())), preferred_element_type=F32)


def _proj_even_kernel(h_ref, gmix_ref, win_ref, wvbt_ref, gcq_ref, gckv_ref, wuq_ref, wuk_ref, wuvt_ref,
                      gq_ref, gk_ref, gkr_ref, gbq_ref, gbk_ref, cos_ref, sa_ref, sb_ref,
                      qa_ref, ka_ref, vat_ref, qb_ref, kb_ref, vbt_ref):
    x = h_ref[...]
    hn = (x * _rms_scale(x) * gmix_ref[...]).astype(BF16)
    u = jnp.dot(hn, win_ref[...], preferred_element_type=F32)
    vbt_ref[...] = _dot_nt(wvbt_ref[...], hn).astype(BF16)
    lane = _lane_iota()
    lo = lane < 64
    cos_t = cos_ref[...]
    sin_a = sa_ref[...]
    sin_b = sb_ref[...]

    def rope(xg):
        return (xg * cos_t + pltpu.roll(xg, LANES - A_ROPE // 2, 1) * sin_a
                + pltpu.roll(xg, A_ROPE // 2, 1) * sin_b)

    cq = u[:, 0:A_Q_LORA]
    cqn = (cq * _rms_scale(cq) * gcq_ref[...]).astype(BF16)
    q = jnp.dot(cqn, wuq_ref[...], preferred_element_type=F32)
    ckv = u[:, A_Q_LORA:A_Q_LORA + A_KV_LORA]
    ckvn = (ckv * _rms_scale(ckv) * gckv_ref[...]).astype(BF16)
    kn = jnp.dot(ckvn, wuk_ref[...], preferred_element_type=F32)
    vat_ref[...] = _dot_nt(wuvt_ref[...], ckvn).astype(BF16)

    kr = u[:, 384:512]
    kr_s = lax.rsqrt(jnp.sum(kr * kr, axis=-1, keepdims=True) * (1.0 / A_ROPE) + EPS)
    krot = rope(kr * kr_s * gkr_ref[...])

    for hd in range(A_HEADS):
        sl = slice(hd * LANES, (hd + 1) * LANES)
        blk = q[:, sl]
        sq = blk * blk
        tot = jnp.sum(sq, axis=-1, keepdims=True)
        ssn = jnp.sum(jnp.where(lo, sq, 0.0), axis=-1, keepdims=True)
        sc = jnp.where(lo, lax.rsqrt(ssn * (1.0 / A_NOPE) + EPS),
                       lax.rsqrt((tot - ssn) * (1.0 / A_ROPE) + EPS))
        qa_ref[:, sl] = rope(blk * sc * gq_ref[:, sl]).astype(BF16)
        kb_ = kn[:, sl]
        ks = lax.rsqrt(jnp.sum(kb_ * kb_, axis=-1, keepdims=True) * (1.0 / A_NOPE) + EPS)
        ka_ref[:, sl] = (kb_ * ks * gk_ref[:, sl] + krot).astype(BF16)

    def pair_norm(blk, g):
        sq = blk * blk
        tot = jnp.sum(sq, axis=-1, keepdims=True)
        s_lo = jnp.sum(jnp.where(lo, sq, 0.0), axis=-1, keepdims=True)
        sc = jnp.where(lo, lax.rsqrt(s_lo * (1.0 / B_DH) + EPS),
                       lax.rsqrt((tot - s_lo) * (1.0 / B_DH) + EPS))
        return (blk * sc * g).astype(BF16)

    for p in range(B_HEADS // 2):
        sl = slice(p * LANES, (p + 1) * LANES)
        qb_ref[:, sl] = pair_norm(u[:, 512 + p * LANES:512 + (p + 1) * LANES], gbq_ref[:, sl])
        kb_ref[:, sl] = pair_norm(u[:, 1024 + p * LANES:1024 + (p + 1) * LANES], gbk_ref[:, sl])


def _proj_even(h, gmix, win, wvbt, gcq, gckv, wuq, wuk, wuvt, gq, gk, gkr, gbq, gbk, cos_t, sin_a, sin_b):
    t = h.shape[0]
    tm = TM_PROJ
    row = lambda w: pl.BlockSpec((tm, w), lambda i: (i, 0))
    col = lambda w: pl.BlockSpec((w, tm), lambda i: (0, i))
    rshape = lambda w: jax.ShapeDtypeStruct((t, w), BF16)
    cshape = lambda w: jax.ShapeDtypeStruct((w, t), BF16)
    consts = (gmix, win, wvbt, gcq, gckv, wuq, wuk, wuvt, gq, gk, gkr, gbq, gbk)
    return pl.pallas_call(
        _proj_even_kernel,
        grid=(t // tm,),
        in_specs=[row(D_MODEL)] + [_const_spec(c.shape) for c in consts] + [row(LANES)] * 3,
        out_specs=[row(1024), row(1024), col(512), row(512), row(512), col(512)],
        out_shape=[rshape(1024), rshape(1024), cshape(512), rshape(512), rshape(512), cshape(512)],
        compiler_params=_cparams(("parallel",)),
        name="proj_even",
    )(h, *consts, cos_t, sin_a, sin_b)


def _causal_streams(q_ref, k_ref, vt_ref, c, *, split):
    tq = TQ_CAUSAL
    q = q_ref[c * tq:(c + 1) * tq, :]
    lo = _lane_iota() < 64
    if split:
        qs = [q[:, :LANES], q[:, LANES:]]
    else:
        zero = jnp.zeros_like(q)
        qs = [jnp.where(lo, q, zero), jnp.where(lo, zero, q)]
    kchunk = lax.broadcasted_iota(jnp.int32, (tq, tq), 0) // CHUNK
    qchunk = lax.broadcasted_iota(jnp.int32, (tq, tq), 1) // CHUNK
    allowed = kchunk <= qchunk
    n_past = c * tq
    outs = []
    for m in range(2):
        ksl = slice(m * LANES, (m + 1) * LANES) if split else slice(None)
        s_d = jnp.where(allowed, _dot_nt(k_ref[n_past:n_past + tq, ksl], qs[m]), NEG)
        mx = jnp.max(s_d, axis=0, keepdims=True)
        if c:
            s_p = _dot_nt(k_ref[0:n_past, ksl], qs[m])
            mx = jnp.maximum(mx, jnp.max(s_p, axis=0, keepdims=True))
        p_d = jnp.exp2(s_d - mx)
        l = jnp.sum(p_d, axis=0, keepdims=True)
        o = jnp.dot(vt_ref[:, n_past:n_past + tq], p_d.astype(BF16), preferred_element_type=F32)
        if c:
            p_p = jnp.exp2(s_p - mx)
            l = l + jnp.sum(p_p, axis=0, keepdims=True)
            o = o + jnp.dot(vt_ref[:, 0:n_past], p_p.astype(BF16), preferred_element_type=F32)
        outs.append(o * (1.0 / l))
    return outs


def _row_iota(n):
    return lax.broadcasted_iota(jnp.int32, (LANES, n), 0)


def _attn_mla_kernel(q_ref, k_ref, vt_ref, o_ref):
    tq = TQ_CAUSAL
    for c in range(q_ref.shape[0] // tq):
        o0, o1 = _causal_streams(q_ref, k_ref, vt_ref, c, split=True)
        o_ref[c * tq:(c + 1) * tq, :] = jnp.where(_row_iota(tq) < 64, o0, o1).T.astype(o_ref.dtype)


def _attn_diff_kernel(q_ref, k_ref, vt_ref, lq1_ref, lk1_ref, lq2_ref, lk2_ref, gsub_ref, o_ref,
                      *, lam_init):
    tq = TQ_CAUSAL
    lam = (jnp.exp(jnp.sum(lq1_ref[...] * lk1_ref[...], axis=-1, keepdims=True))
           - jnp.exp(jnp.sum(lq2_ref[...] * lk2_ref[...], axis=-1, keepdims=True)) + lam_init)
    for c in range(q_ref.shape[0] // tq):
        o0, o1 = _causal_streams(q_ref, k_ref, vt_ref, c, split=False)
        o = (o0 - lam * o1).T
        o = o * _rms_scale(o) * gsub_ref[...] * (1.0 - lam_init)
        o_ref[c * tq:(c + 1) * tq, :] = o.astype(o_ref.dtype)


def _attn_causal(kernel, q, k, vt, extras, *, batch, seq, n_blocks, qk_width):
    in_specs = [pl.BlockSpec((seq, qk_width), lambda b, p: (b, p)),
                pl.BlockSpec((seq, qk_width), lambda b, p: (b, p)),
                pl.BlockSpec((LANES, seq), lambda b, p: (p, b))]
    in_specs += [_const_spec(e.shape) for e in extras]
    return pl.pallas_call(
        kernel,
        grid=(batch, n_blocks),
        in_specs=in_specs,
        out_specs=pl.BlockSpec((seq, LANES), lambda b, p: (b, p)),
        out_shape=jax.ShapeDtypeStruct((batch * seq, n_blocks * LANES), BF16),
        compiler_params=_cparams(("parallel", "parallel")),
        name=kernel.func.__name__ if isinstance(kernel, functools.partial) else kernel.__name__,
    )(q, k, vt, *extras)


def _attn_band_kernel(q_ref, k_ref, vt_ref, bias_ref, o_ref):
    tq = TQ_BAND
    lo = _lane_iota() < 64
    for c in range(q_ref.shape[0] // tq):
        q = q_ref[c * tq:(c + 1) * tq, :]
        zero = jnp.zeros_like(q)
        qs = [jnp.where(lo, q, zero), jnp.where(lo, zero, q)]
        end = (c + 1) * tq
        start = max(0, end - BAND_KEYS)
        outs = []
        for m in range(2):
            s = _dot_nt(k_ref[start:end, :], qs[m]) + bias_ref[m, BAND_KEYS - (end - start):, :]
            p = jnp.exp2(s - jnp.max(s, axis=0, keepdims=True))
            o = jnp.dot(vt_ref[:, start:end], p.astype(BF16), preferred_element_type=F32)
            outs.append(o * (1.0 / jnp.sum(p, axis=0, keepdims=True)))
        o_ref[c * tq:(c + 1) * tq, :] = jnp.where(_row_iota(tq) < 64, outs[0], outs[1]).T.astype(o_ref.dtype)


def _attn_band(q, k, vt, bias, *, batch, seq):
    npair = B_HEADS // 2
    return pl.pallas_call(
        _attn_band_kernel,
        grid=(batch, npair),
        in_specs=[pl.BlockSpec((seq, LANES), lambda b, p: (b, p)),
                  pl.BlockSpec((seq, LANES), lambda b, p: (b, p)),
                  pl.BlockSpec((LANES, seq), lambda b, p: (p, b)),
                  pl.BlockSpec((2, BAND_KEYS, TQ_BAND), lambda b, p: (p, 0, 0))],
        out_specs=pl.BlockSpec((seq, LANES), lambda b, p: (b, p)),
        out_shape=jax.ShapeDtypeStruct((batch * seq, npair * LANES), BF16),
        compiler_params=_cparams(("parallel", "parallel")),
        name="attn_band",
    )(q, k, vt, bias)


def _silu(g):
    return g / (1.0 + jnp.exp(-g))


def _out_ffn_kernel(h_ref, ya_ref, yb_ref, woa_ref, wob_ref, gff_ref, wg_ref, wu_ref, wd_ref, o_ref):
    h1 = (h_ref[...] + jnp.dot(ya_ref[...], woa_ref[...], preferred_element_type=F32)
          + jnp.dot(yb_ref[...], wob_ref[...], preferred_element_type=F32))
    xn = (h1 * _rms_scale(h1) * gff_ref[...]).astype(BF16)
    g = jnp.dot(xn, wg_ref[...], preferred_element_type=F32)
    u = jnp.dot(xn, wu_ref[...], preferred_element_type=F32)
    a = (_silu(g) * u).astype(BF16)
    o_ref[...] = h1 + jnp.dot(a, wd_ref[...], preferred_element_type=F32)


def _out_ffn(h, ya, yb, woa, wob, gff, wg, wu, wd):
    t = h.shape[0]
    tm = TM_PROJ
    row = lambda w: pl.BlockSpec((tm, w), lambda i: (i, 0))
    return pl.pallas_call(
        _out_ffn_kernel,
        grid=(t // tm,),
        in_specs=[row(D_MODEL), row(ya.shape[1]), row(yb.shape[1])]
        + [_const_spec(w.shape) for w in (woa, wob, gff, wg, wu, wd)],
        out_specs=row(D_MODEL),
        out_shape=jax.ShapeDtypeStruct((t, D_MODEL), F32),
        compiler_params=_cparams(("parallel",)),
        name="out_ffn",
    )(h, ya, yb, woa, wob, gff, wg, wu, wd)


def _proj_odd_kernel(h_ref, gmix_ref, w_ref, wvt_ref, gq_ref, gk_ref, cos_ref, sa_ref, sb_ref,
                     q_ref, k_ref, vt_ref):
    x = h_ref[...]
    hn = (x * _rms_scale(x) * gmix_ref[...]).astype(BF16)
    u = jnp.dot(hn, w_ref[...], preferred_element_type=F32)
    vt_ref[...] = _dot_nt(wvt_ref[...], hn).astype(BF16)
    lo = _lane_iota() < 64
    cos_t = cos_ref[...]
    sin_a = sa_ref[...]
    sin_b = sb_ref[...]
    width = C_HEADS * 2 * C_DH

    def norm_rope(blk, g):
        sq = blk * blk
        tot = jnp.sum(sq, axis=-1, keepdims=True)
        s_lo = jnp.sum(jnp.where(lo, sq, 0.0), axis=-1, keepdims=True)
        sc = jnp.where(lo, lax.rsqrt(s_lo * (1.0 / C_DH) + EPS),
                       lax.rsqrt((tot - s_lo) * (1.0 / C_DH) + EPS))
        xg = blk * sc * g
        return (xg * cos_t + pltpu.roll(xg, LANES - C_ROPE // 2, 1) * sin_a
                + pltpu.roll(xg, C_ROPE // 2, 1) * sin_b).astype(BF16)

    for hd in range(C_HEADS):
        sl = slice(hd * LANES, (hd + 1) * LANES)
        q_ref[:, sl] = norm_rope(u[:, hd * LANES:(hd + 1) * LANES], gq_ref[...])
        k_ref[:, sl] = norm_rope(u[:, width + hd * LANES:width + (hd + 1) * LANES], gk_ref[...])


def _proj_odd(h, gmix, w, wvt, gq, gk, cos_t, sin_a, sin_b):
    t = h.shape[0]
    tm = TM_PROJ
    row = lambda wd: pl.BlockSpec((tm, wd), lambda i: (i, 0))
    width = C_HEADS * 2 * C_DH
    consts = (gmix, w, wvt, gq, gk)
    return pl.pallas_call(
        _proj_odd_kernel,
        grid=(t // tm,),
        in_specs=[row(D_MODEL)] + [_const_spec(c.shape) for c in consts] + [row(LANES)] * 3,
        out_specs=[row(width), row(width), pl.BlockSpec((width, tm), lambda i: (0, i))],
        out_shape=[jax.ShapeDtypeStruct((t, width), BF16), jax.ShapeDtypeStruct((t, width), BF16),
                   jax.ShapeDtypeStruct((width, t), BF16)],
        compiler_params=_cparams(("parallel",)),
        name="proj_odd",
    )(h, *consts, cos_t, sin_a, sin_b)


def _out_router_kernel(h_ref, y_ref, wo_ref, gff_ref, wr_ref, h1_ref, xn_ref, idx_ref, gate_ref):
    h1 = h_ref[...] + jnp.dot(y_ref[...], wo_ref[...], preferred_element_type=F32)
    h1_ref[...] = h1
    xn = h1 * _rms_scale(h1) * gff_ref[...]
    xn_ref[...] = xn
    logits = [jnp.sum(xn * wr_ref[e:e + 1, :], axis=-1, keepdims=True) for e in range(N_EXPERTS)]
    m1 = functools.reduce(jnp.maximum, logits)
    idx1 = functools.reduce(jnp.minimum,
                            [jnp.where(logits[e] == m1, e, N_EXPERTS) for e in range(N_EXPERTS)])
    rest = [jnp.where(idx1 == e, -jnp.inf, logits[e]) for e in range(N_EXPERTS)]
    m2 = functools.reduce(jnp.maximum, rest)
    idx2 = functools.reduce(jnp.minimum,
                            [jnp.where(rest[e] == m2, e, N_EXPERTS) for e in range(N_EXPERTS)])
    e2 = jnp.exp(m2 - m1)
    g1 = 1.0 / (1.0 + e2)
    g2 = e2 * g1
    lane = _lane_iota()
    idx_ref[...] = jnp.where(lane == 0, idx1, jnp.where(lane == 1, idx2, 0))
    gate_ref[...] = jnp.where(lane == 0, g1, jnp.where(lane == 1, g2, 0.0))


def _out_router(h, y, wo, gff, wr):
    t = h.shape[0]
    tm = TM_PROJ
    row = lambda w: pl.BlockSpec((tm, w), lambda i: (i, 0))
    return pl.pallas_call(
        _out_router_kernel,
        grid=(t // tm,),
        in_specs=[row(D_MODEL), row(y.shape[1]), _const_spec(wo.shape), _const_spec(gff.shape),
                  _const_spec(wr.shape)],
        out_specs=[row(D_MODEL), row(D_MODEL), row(LANES), row(LANES)],
        out_shape=[jax.ShapeDtypeStruct((t, D_MODEL), F32), jax.ShapeDtypeStruct((t, D_MODEL), F32),
                   jax.ShapeDtypeStruct((t, LANES), jnp.int32), jax.ShapeDtypeStruct((t, LANES), F32)],
        compiler_params=_cparams(("parallel",)),
        name="out_router",
    )(h, y, wo, gff, wr)


def _row_copy(src_ref, src_row, dst_ref, dst_row, sem):
    return pltpu.make_async_copy(src_ref.at[pl.ds(src_row, 1)], dst_ref.at[pl.ds(dst_row, 1)], sem)


def _dispatch_kernel(row_ref, x_ref, xs_ref, sem):
    tm = x_ref.shape[0]
    base = pl.program_id(0) * tm

    def issue(r, carry):
        for k in range(2):
            _row_copy(x_ref, r, xs_ref, row_ref[2 * (base + r) + k], sem).start()
        return carry

    lax.fori_loop(0, tm, issue, 0, unroll=DMA_UNROLL)
    for k in range(2):
        pltpu.make_async_copy(x_ref, xs_ref.at[pl.ds(0, tm)], sem).wait()


def _dispatch(row_of_pair, xn):
    t, d = xn.shape
    tm = TM_DISPATCH
    grid_spec = pltpu.PrefetchScalarGridSpec(
        num_scalar_prefetch=1,
        grid=(t // tm,),
        in_specs=[pl.BlockSpec((tm, d), lambda i, rows: (i, 0))],
        out_specs=pl.BlockSpec(memory_space=pl.ANY),
        scratch_shapes=[pltpu.SemaphoreType.DMA(())],
    )
    return pl.pallas_call(
        _dispatch_kernel,
        grid_spec=grid_spec,
        out_shape=jax.ShapeDtypeStruct((2 * t, d), xn.dtype),
        compiler_params=_cparams(("arbitrary",)),
        name="moe_dispatch",
    )(row_of_pair, xn)


def _moe_kernel(vt_ref, ve_ref, lo_ref, hi_ref, first_ref, x_ref, wg_ref, wu_ref, wd_ref, o_ref):
    v = pl.program_id(0)
    c = pl.program_id(1)
    tr = x_ref.shape[0]

    @pl.when((c == 0) & (first_ref[v] == 1))
    def _():
        o_ref[...] = jnp.zeros_like(o_ref)

    @pl.when(lo_ref[v] < hi_ref[v])
    def _():
        x = x_ref[...].astype(BF16)
        g = jnp.dot(x, wg_ref[0].astype(BF16), preferred_element_type=F32)
        u = jnp.dot(x, wu_ref[0].astype(BF16), preferred_element_type=F32)
        a = (_silu(g) * u).astype(BF16)
        y = jnp.dot(a, wd_ref[0].astype(BF16), preferred_element_type=F32)
        rows = vt_ref[v] * tr + lax.broadcasted_iota(jnp.int32, (tr, 1), 0)
        mine = (rows >= lo_ref[v]) & (rows < hi_ref[v])
        o_ref[...] += jnp.where(mine, y, 0.0)


def _moe(visits, xs, wg, wu, wd):
    r = xs.shape[0]
    tr, fc = TR_MOE, FC_MOE
    nc = FF_EXPERT // fc
    n_visits = r // tr + N_EXPERTS - 1

    def chunk(v, c, lo, hi):
        return jnp.where(lo[v] < hi[v], c, nc - 1)

    tile = lambda v, c, vt, ve, lo, hi, first: (vt[v], 0)
    wcol = lambda v, c, vt, ve, lo, hi, first: (ve[v], 0, chunk(v, c, lo, hi))
    wrow = lambda v, c, vt, ve, lo, hi, first: (ve[v], chunk(v, c, lo, hi), 0)
    grid_spec = pltpu.PrefetchScalarGridSpec(
        num_scalar_prefetch=5,
        grid=(n_visits, nc),
        in_specs=[pl.BlockSpec((tr, D_MODEL), tile),
                  pl.BlockSpec((1, D_MODEL, fc), wcol),
                  pl.BlockSpec((1, D_MODEL, fc), wcol),
                  pl.BlockSpec((1, fc, D_MODEL), wrow)],
        out_specs=pl.BlockSpec((tr, D_MODEL), tile),
    )
    return pl.pallas_call(
        _moe_kernel,
        grid_spec=grid_spec,
        out_shape=jax.ShapeDtypeStruct((r, D_MODEL), F32),
        compiler_params=_cparams(("arbitrary", "arbitrary")),
        name="moe_grouped",
    )(*visits, xs, wg, wu, wd)


def _combine_kernel(row_ref, h_ref, gate_ref, ys_ref, o_ref, buf, sem):
    tm = h_ref.shape[0]
    base = pl.program_id(0) * tm

    def issue(r, carry):
        for k in range(2):
            _row_copy(ys_ref, row_ref[2 * (base + r) + k], buf, k * tm + r, sem).start()
        return carry

    lax.fori_loop(0, tm, issue, 0, unroll=DMA_UNROLL)
    pltpu.make_async_copy(ys_ref.at[pl.ds(0, 2 * tm)], buf, sem).wait()
    gate = gate_ref[...]
    o_ref[...] = h_ref[...] + gate[:, 0:1] * buf[0:tm, :] + gate[:, 1:2] * buf[tm:2 * tm, :]


def _combine(row_of_pair, h1, gates, ys):
    t, d = h1.shape
    tm = TM_DISPATCH
    grid_spec = pltpu.PrefetchScalarGridSpec(
        num_scalar_prefetch=1,
        grid=(t // tm,),
        in_specs=[pl.BlockSpec((tm, d), lambda i, rows: (i, 0)),
                  pl.BlockSpec((tm, LANES), lambda i, rows: (i, 0)),
                  pl.BlockSpec(memory_space=pl.ANY)],
        out_specs=pl.BlockSpec((tm, d), lambda i, rows: (i, 0)),
        scratch_shapes=[pltpu.VMEM((2 * tm, d), F32), pltpu.SemaphoreType.DMA(())],
    )
    return pl.pallas_call(
        _combine_kernel,
        grid_spec=grid_spec,
        out_shape=jax.ShapeDtypeStruct((t, d), F32),
        compiler_params=_cparams(("arbitrary",)),
        name="moe_combine",
    )(row_of_pair, h1, gates, ys)


def _route_plan(idx):
    n_pairs = idx.size
    tr = TR_MOE
    n_tiles = n_pairs // tr
    n_visits = n_tiles + N_EXPERTS - 1
    e_flat = idx.reshape(-1)
    onehot = (e_flat[:, None] == jnp.arange(N_EXPERTS, dtype=jnp.int32)[None, :]).astype(jnp.int32)
    csum = jnp.cumsum(onehot, axis=0)
    counts = csum[-1]
    ends = jnp.cumsum(counts)
    starts = ends - counts
    pos = jnp.sum((csum - onehot) * onehot, axis=1)
    row_of_pair = (starts[e_flat] + pos).astype(jnp.int32)
    nv = jnp.where(counts > 0, (ends - 1) // tr - starts // tr + 1, 0)
    vend = jnp.cumsum(nv)
    vid = jnp.arange(n_visits, dtype=jnp.int32)
    live = vid < vend[-1]
    ve = jnp.minimum(jnp.sum((vid[:, None] >= vend[None, :]).astype(jnp.int32), axis=1), N_EXPERTS - 1)
    vt = jnp.where(live, starts[ve] // tr + vid - (vend - nv)[ve], n_tiles - 1)
    lo = jnp.where(live, jnp.maximum(starts[ve], vt * tr), 0)
    hi = jnp.where(live, jnp.minimum(ends[ve], (vt + 1) * tr), 0)
    last_e = jnp.max(jnp.where(counts > 0, jnp.arange(N_EXPERTS, dtype=jnp.int32), 0))
    ve = jnp.where(live, ve, last_e)
    first = (live & (lo == vt * tr)).astype(jnp.int32)
    i32 = lambda a: a.astype(jnp.int32)
    return row_of_pair, (i32(vt), i32(ve), i32(lo), i32(hi), first)


def _rope_tables(positions, dim, theta):
    inv = 1.0 / (theta ** (jnp.arange(0, dim, 2, dtype=F32) / dim))
    ang = positions.astype(F32).reshape(-1, 1) * inv
    return jnp.cos(ang), jnp.sin(ang)


def _tables_a(positions):
    cos, sin = _rope_tables(positions, A_ROPE, A_THETA)
    t = cos.shape[0]
    one, z16, z32 = jnp.ones((t, A_NOPE), F32), jnp.zeros((t, 16), F32), jnp.zeros((t, 32), F32)
    z64 = jnp.zeros((t, 64), F32)
    return (jnp.concatenate([one, cos, cos, z32], 1), jnp.concatenate([z64, -sin, z16, z32], 1),
            jnp.concatenate([z64, z16, sin, z32], 1))


def _tables_c(positions):
    cos, sin = _rope_tables(positions, C_ROPE, ROPE_THETA)
    t = cos.shape[0]
    one, z8, z48 = jnp.ones((t, 48), F32), jnp.zeros((t, 8), F32), jnp.zeros((t, 48), F32)
    two = lambda a: jnp.concatenate([a, a], 1)
    return (two(jnp.concatenate([cos, cos, one], 1)), two(jnp.concatenate([-sin, z8, z48], 1)),
            two(jnp.concatenate([z8, sin, z48], 1)))


def _band_bias(rel_table):
    nh = rel_table.shape[0]
    nk, nq = BAND_KEYS, TQ_BAND
    length = nk + nq - 1
    n_lo = nk - 1 - B_LEFT_CHUNKS * CHUNK - B_REL_CLIP
    n_hi = length - n_lo - rel_table.shape[1]
    prof = jnp.concatenate([jnp.broadcast_to(rel_table[:, :1], (nh, n_lo)), rel_table,
                            jnp.broadcast_to(rel_table[:, -1:], (nh, n_hi)),
                            jnp.zeros((nh, 1), rel_table.dtype)], axis=1)
    toep = jnp.tile(prof, (1, nk))[:, :nk * length].reshape(nh, nk, length)
    bias = toep[:, :, nk - 1:nk - 1 + nq].astype(F32) * LOG2E
    r = np.arange(nq)[None, :]
    j = np.arange(nk)[:, None]
    cdiff = r // CHUNK + B_LEFT_CHUNKS - j // CHUNK
    allowed = (cdiff >= 0) & (cdiff <= B_LEFT_CHUNKS)
    return jnp.where(jnp.asarray(allowed)[None], bias, NEG)


def _pad_last(a, width):
    return jnp.pad(a, [(0, 0)] * (a.ndim - 1) + [(0, width - a.shape[-1])])


def kernel(x, positions, ev_norm_mix, ev_w_in, ev_g_cq, ev_g_ckv, ev_w_uq, ev_w_ukv, ev_a_qnorm, ev_a_knorm, ev_b_qnorm, ev_b_knorm, ev_b_rel_bias, ev_w_out, ev_norm_ff, ev_ff_gate, ev_ff_up, ev_ff_down, od_norm_mix, od_w_qkv, od_c_qnorm, od_c_knorm, od_lam_q1, od_lam_k1, od_lam_q2, od_lam_k2, od_c_subnorm, od_w_out, od_norm_ff, od_router, od_ex_gate, od_ex_up, od_ex_down):
    batch, seq, d = x.shape
    t = batch * seq
    depth = ev_w_in.shape[0] + od_w_qkv.shape[0]
    h = x.reshape(t, d)
    row = lambda a: a.reshape(1, -1).astype(F32)
    tab_a = _tables_a(positions)
    tab_c = _tables_c(positions)

    for layer in range(depth):
        i = layer // 2
        if layer % 2 == 0:
            w = ev_w_in[i]
            o1, o2, o3 = A_Q_LORA, A_Q_LORA + A_KV_LORA, A_Q_LORA + A_KV_LORA + A_ROPE
            zc = lambda n: jnp.zeros((d, n), F32)
            bw = B_HEADS * B_DH
            win = jnp.concatenate([w[:, :o2], zc(64), w[:, o2:o3], zc(32), w[:, o3:o3 + 2 * bw]], 1).astype(BF16)
            wvbt = w[:, o3 + 2 * bw:].T.astype(BF16)
            wuq = _pad_last(ev_w_uq[i], LANES).reshape(A_Q_LORA, A_HEADS * LANES).astype(BF16)
            wuk = _pad_last(ev_w_ukv[i][:, :, :A_NOPE], LANES).reshape(A_KV_LORA, A_HEADS * LANES).astype(BF16)
            wuvt = ev_w_ukv[i][:, :, A_NOPE:].reshape(A_KV_LORA, A_HEADS * A_V).T.astype(BF16)
            sc_a = (A_NOPE + A_ROPE) ** -0.5 * LOG2E
            gq = jnp.tile(_pad_last(ev_a_qnorm[i], LANES), A_HEADS).reshape(1, -1) * sc_a
            gk = jnp.tile(_pad_last(ev_a_knorm[i][:A_NOPE], LANES), A_HEADS).reshape(1, -1)
            gkr = jnp.concatenate([jnp.zeros((A_NOPE,), F32), ev_a_knorm[i][A_NOPE:],
                                   jnp.zeros((32,), F32)]).reshape(1, -1)
            gbq = jnp.tile(ev_b_qnorm[i], B_HEADS).reshape(1, -1) * (B_DH ** -0.5 * LOG2E)
            gbk = jnp.tile(ev_b_knorm[i], B_HEADS).reshape(1, -1)
            qa, ka, vat, qb, kb, vbt = _proj_even(
                h, row(ev_norm_mix[i]), win, wvbt, row(ev_g_cq[i]), row(ev_g_ckv[i]), wuq, wuk, wuvt,
                gq, gk, gkr, gbq, gbk, *tab_a)
            ya = _attn_causal(_attn_mla_kernel, qa, ka, vat, (), batch=batch, seq=seq,
                              n_blocks=A_HEADS // 2, qk_width=2 * LANES)
            yb = _attn_band(qb, kb, vbt, _band_bias(ev_b_rel_bias[i]), batch=batch, seq=seq)
            wo = ev_w_out[i].astype(BF16)
            na = A_HEADS * A_V
            padf = lambda a: _pad_last(a, FF_DENSE_PAD).astype(BF16)
            wd = jnp.pad(ev_ff_down[i], ((0, FF_DENSE_PAD - FF_DENSE), (0, 0))).astype(BF16)
            h = _out_ffn(h, ya, yb, wo[:na], wo[na:], row(ev_norm_ff[i]),
                         padf(ev_ff_gate[i]), padf(ev_ff_up[i]), wd)
        else:
            lam_init = 0.8 - 0.6 * math.exp(-0.3 * layer)
            gq = jnp.tile(od_c_qnorm[i], 2).reshape(1, -1) * (C_DH ** -0.5 * LOG2E)
            gk = jnp.tile(od_c_knorm[i], 2).reshape(1, -1)
            c_mix = C_HEADS * 2 * C_DH
            wqkv = od_w_qkv[i]
            q, k, v = _proj_odd(h, row(od_norm_mix[i]), wqkv[:, :2 * c_mix].astype(BF16),
                                wqkv[:, 2 * c_mix:].T.astype(BF16), gq, gk, *tab_c)
            extras = (row(od_lam_q1[i]), row(od_lam_k1[i]), row(od_lam_q2[i]), row(od_lam_k2[i]),
                      row(od_c_subnorm[i]))
            yc = _attn_causal(functools.partial(_attn_diff_kernel, lam_init=lam_init), q, k, v, extras,
                              batch=batch, seq=seq, n_blocks=C_HEADS, qk_width=LANES)
            h1, xn, idx, gates = _out_router(h, yc, od_w_out[i].astype(BF16), row(od_norm_ff[i]),
                                             od_router[i].T.astype(F32))
            row_of_pair, visits = _route_plan(idx[:, :2])
            xs = _dispatch(row_of_pair, xn)
            ys = _moe(visits, xs, od_ex_gate[i], od_ex_up[i], od_ex_down[i])
            h = _combine(row_of_pair, h1, gates, ys)
    return h.reshape(batch, seq, d)
```

```python
import functools
import math

import numpy as np
import jax
import jax.numpy as jnp
from jax import lax
from jax.experimental import pallas as pl
from jax.experimental.pallas import tpu as pltpu

F32 = jnp.float32
BF16 = jnp.bfloat16

D_MODEL = 1024
CHUNK = 64
EPS = 1e-6
LOG2E = 1.4426950408889634
NEG = -1e30

A_HEADS = 8
A_NOPE = 64
A_ROPE = 32
A_V = 64
A_Q_LORA = 256
A_KV_LORA = 128
A_THETA = 10000.0
B_HEADS = 8
B_DH = 64
B_LEFT_CHUNKS = 8
B_REL_CLIP = 128
C_HEADS = 8
C_DH = 64
C_ROPE = C_DH // 4
ROPE_THETA = 500000.0
FF_DENSE = 2752
N_EXPERTS = 8
FF_EXPERT = 3584

LANES = 128
FF_DENSE_PAD = 2816
VMEM_LIMIT = 56 * 1024 * 1024

TM_PROJ = 512
TQ_CAUSAL = 256
TQ_BAND = 256
BAND_KEYS = TQ_BAND + B_LEFT_CHUNKS * CHUNK
BAND_PROFILE = 1024
TM_DISPATCH = 256
DMA_UNROLL = 8
TR_MOE = 1024
FC_MOE = 896


def _cparams(sem):
    return pltpu.CompilerParams(dimension_semantics=sem, vmem_limit_bytes=VMEM_LIMIT)


def _const_spec(shape):
    nd = len(shape)
    return pl.BlockSpec(shape, lambda *_: (0,) * nd, pipeline_mode=pl.Buffered(1))


def _rms_scale(x):
    return lax.rsqrt(jnp.mean(x * x, axis=-1, keepdims=True) + EPS)


def _lane_iota():
    return lax.broadcasted_iota(jnp.int32, (1, LANES), 1)


def _dot_nt(a, b):
    return lax.dot_general(a, b, (((1,), (1,)), ((), ())), preferred_element_type=F32)


def _proj_even_kernel(h_ref, gmix_ref, win_ref, wvbt_ref, gcq_ref, gckv_ref, wuq_ref, wuk_ref, wuvt_ref,
                      gq_ref, gk_ref, gkr_ref, gbq_ref, gbk_ref, cos_ref, sa_ref, sb_ref,
                      qa_ref, ka_ref, vat_ref, qb_ref, kb_ref, vbt_ref):
    x = h_ref[...]
    hn = (x * _rms_scale(x) * gmix_ref[...]).astype(BF16)
    u = jnp.dot(hn, win_ref[...], preferred_element_type=F32)
    vbt_ref[...] = _dot_nt(wvbt_ref[...], hn).astype(BF16)
    lane = _lane_iota()
    lo = lane < 64
    cos_t = cos_ref[...]
    sin_a = sa_ref[...]
    sin_b = sb_ref[...]

    def rope(xg):
        return (xg * cos_t + pltpu.roll(xg, LANES - A_ROPE // 2, 1) * sin_a
                + pltpu.roll(xg, A_ROPE // 2, 1) * sin_b)

    cq = u[:, 0:A_Q_LORA]
    cqn = (cq * _rms_scale(cq) * gcq_ref[...]).astype(BF16)
    q = jnp.dot(cqn, wuq_ref[...], preferred_element_type=F32)
    ckv = u[:, A_Q_LORA:A_Q_LORA + A_KV_LORA]
    ckvn = (ckv * _rms_scale(ckv) * gckv_ref[...]).astype(BF16)
    kn = jnp.dot(ckvn, wuk_ref[...], preferred_element_type=F32)
    vat_ref[...] = _dot_nt(wuvt_ref[...], ckvn).astype(BF16)

    kr = u[:, 384:512]
    kr_s = lax.rsqrt(jnp.sum(kr * kr, axis=-1, keepdims=True) * (1.0 / A_ROPE) + EPS)
    krot = rope(kr * kr_s * gkr_ref[...])

    for hd in range(A_HEADS):
        sl = slice(hd * LANES, (hd + 1) * LANES)
        blk = q[:, sl]
        sq = blk * blk
        tot = jnp.sum(sq, axis=-1, keepdims=True)
        ssn = jnp.sum(jnp.where(lo, sq, 0.0), axis=-1, keepdims=True)
        sc = jnp.where(lo, lax.rsqrt(ssn * (1.0 / A_NOPE) + EPS),
                       lax.rsqrt((tot - ssn) * (1.0 / A_ROPE) + EPS))
        qa_ref[:, sl] = rope(blk * sc * gq_ref[:, sl]).astype(BF16)
        kb_ = kn[:, sl]
        ks = lax.rsqrt(jnp.sum(kb_ * kb_, axis=-1, keepdims=True) * (1.0 / A_NOPE) + EPS)
        ka_ref[:, sl] = (kb_ * ks * gk_ref[:, sl] + krot).astype(BF16)

    def pair_norm(blk, g):
        sq = blk * blk
        tot = jnp.sum(sq, axis=-1, keepdims=True)
        s_lo = jnp.sum(jnp.where(lo, sq, 0.0), axis=-1, keepdims=True)
        sc = jnp.where(lo, lax.rsqrt(s_lo * (1.0 / B_DH) + EPS),
                       lax.rsqrt((tot - s_lo) * (1.0 / B_DH) + EPS))
        return (blk * sc * g).astype(BF16)

    for p in range(B_HEADS // 2):
        sl = slice(p * LANES, (p + 1) * LANES)
        qb_ref[:, sl] = pair_norm(u[:, 512 + p * LANES:512 + (p + 1) * LANES], gbq_ref[:, sl])
        kb_ref[:, sl] = pair_norm(u[:, 1024 + p * LANES:1024 + (p + 1) * LANES], gbk_ref[:, sl])


def _proj_even(h, gmix, win, wvbt, gcq, gckv, wuq, wuk, wuvt, gq, gk, gkr, gbq, gbk, cos_t, sin_a, sin_b):
    t = h.shape[0]
    tm = TM_PROJ
    row = lambda w: pl.BlockSpec((tm, w), lambda i: (i, 0))
    col = lambda w: pl.BlockSpec((w, tm), lambda i: (0, i))
    rshape = lambda w: jax.ShapeDtypeStruct((t, w), BF16)
    cshape = lambda w: jax.ShapeDtypeStruct((w, t), BF16)
    consts = (gmix, win, wvbt, gcq, gckv, wuq, wuk, wuvt, gq, gk, gkr, gbq, gbk)
    return pl.pallas_call(
        _proj_even_kernel,
        grid=(t // tm,),
        in_specs=[row(D_MODEL)] + [_const_spec(c.shape) for c in consts] + [row(LANES)] * 3,
        out_specs=[row(1024), row(1024), col(512), row(512), row(512), col(512)],
        out_shape=[rshape(1024), rshape(1024), cshape(512), rshape(512), rshape(512), cshape(512)],
        compiler_params=_cparams(("parallel",)),
        name="proj_even",
    )(h, *consts, cos_t, sin_a, sin_b)


def _causal_streams(q_ref, k_ref, vt_ref, c, *, split):
    tq = TQ_CAUSAL
    q = q_ref[c * tq:(c + 1) * tq, :]
    lo = _lane_iota() < 64
    if split:
        qs = [q[:, :LANES], q[:, LANES:]]
    else:
        zero = jnp.zeros_like(q)
        qs = [jnp.where(lo, q, zero), jnp.where(lo, zero, q)]
    kchunk = lax.broadcasted_iota(jnp.int32, (tq, tq), 0) // CHUNK
    qchunk = lax.broadcasted_iota(jnp.int32, (tq, tq), 1) // CHUNK
    allowed = kchunk <= qchunk
    n_past = c * tq
    outs = []
    for m in range(2):
        ksl = slice(m * LANES, (m + 1) * LANES) if split else slice(None)
        s_d = jnp.where(allowed, _dot_nt(k_ref[n_past:n_past + tq, ksl], qs[m]), NEG)
        mx = jnp.max(s_d, axis=0, keepdims=True)
        if c:
            s_p = _dot_nt(k_ref[0:n_past, ksl], qs[m])
            mx = jnp.maximum(mx, jnp.max(s_p, axis=0, keepdims=True))
        p_d = jnp.exp2(s_d - mx)
        l = jnp.sum(p_d, axis=0, keepdims=True)
        o = jnp.dot(vt_ref[:, n_past:n_past + tq], p_d.astype(BF16), preferred_element_type=F32)
        if c:
            p_p = jnp.exp2(s_p - mx)
            l = l + jnp.sum(p_p, axis=0, keepdims=True)
            o = o + jnp.dot(vt_ref[:, 0:n_past], p_p.astype(BF16), preferred_element_type=F32)
        outs.append(o * (1.0 / l))
    return outs


def _row_iota(n):
    return lax.broadcasted_iota(jnp.int32, (LANES, n), 0)


def _attn_mla_kernel(q_ref, k_ref, vt_ref, o_ref):
    tq = TQ_CAUSAL
    for c in range(q_ref.shape[0] // tq):
        o0, o1 = _causal_streams(q_ref, k_ref, vt_ref, c, split=True)
        o_ref[c * tq:(c + 1) * tq, :] = jnp.where(_row_iota(tq) < 64, o0, o1).T.astype(o_ref.dtype)


def _attn_diff_kernel(q_ref, k_ref, vt_ref, lq1_ref, lk1_ref, lq2_ref, lk2_ref, gsub_ref, o_ref,
                      *, lam_init):
    tq = TQ_CAUSAL
    lam = (jnp.exp(jnp.sum(lq1_ref[...] * lk1_ref[...], axis=-1, keepdims=True))
           - jnp.exp(jnp.sum(lq2_ref[...] * lk2_ref[...], axis=-1, keepdims=True)) + lam_init)
    for c in range(q_ref.shape[0] // tq):
        o0, o1 = _causal_streams(q_ref, k_ref, vt_ref, c, split=False)
        o = (o0 - lam * o1).T
        o = o * _rms_scale(o) * gsub_ref[...] * (1.0 - lam_init)
        o_ref[c * tq:(c + 1) * tq, :] = o.astype(o_ref.dtype)


def _attn_causal(kernel, q, k, vt, extras, *, batch, seq, n_blocks, qk_width):
    in_specs = [pl.BlockSpec((seq, qk_width), lambda b, p: (b, p)),
                pl.BlockSpec((seq, qk_width), lambda b, p: (b, p)),
                pl.BlockSpec((LANES, seq), lambda b, p: (p, b))]
    in_specs += [_const_spec(e.shape) for e in extras]
    return pl.pallas_call(
        kernel,
        grid=(batch, n_blocks),
        in_specs=in_specs,
        out_specs=pl.BlockSpec((seq, LANES), lambda b, p: (b, p)),
        out_shape=jax.ShapeDtypeStruct((batch * seq, n_blocks * LANES), BF16),
        compiler_params=_cparams(("parallel", "parallel")),
        name=kernel.func.__name__ if isinstance(kernel, functools.partial) else kernel.__name__,
    )(q, k, vt, *extras)


def _attn_band_kernel(q_ref, k_ref, vt_ref, bias_ref, o_ref):
    tq = TQ_BAND
    lo = _lane_iota() < 64
    for c in range(q_ref.shape[0] // tq):
        q = q_ref[c * tq:(c + 1) * tq, :]
        zero = jnp.zeros_like(q)
        qs = [jnp.where(lo, q, zero), jnp.where(lo, zero, q)]
        end = (c + 1) * tq
        start = max(0, end - BAND_KEYS)
        outs = []
        for m in range(2):
            s = _dot_nt(k_ref[start:end, :], qs[m]) + bias_ref[m, BAND_KEYS - (end - start):, :]
            p = jnp.exp2(s - jnp.max(s, axis=0, keepdims=True))
            o = jnp.dot(vt_ref[:, start:end], p.astype(BF16), preferred_element_type=F32)
            outs.append(o * (1.0 / jnp.sum(p, axis=0, keepdims=True)))
        o_ref[c * tq:(c + 1) * tq, :] = jnp.where(_row_iota(tq) < 64, outs[0], outs[1]).T.astype(o_ref.dtype)


def _attn_band(q, k, vt, bias, *, batch, seq):
    npair = B_HEADS // 2
    return pl.pallas_call(
        _attn_band_kernel,
        grid=(batch, npair),
        in_specs=[pl.BlockSpec((seq, LANES), lambda b, p: (b, p)),
                  pl.BlockSpec((seq, LANES), lambda b, p: (b, p)),
                  pl.BlockSpec((LANES, seq), lambda b, p: (p, b)),
                  pl.BlockSpec((2, BAND_KEYS, TQ_BAND), lambda b, p: (p, 0, 0))],
        out_specs=pl.BlockSpec((seq, LANES), lambda b, p: (b, p)),
        out_shape=jax.ShapeDtypeStruct((batch * seq, npair * LANES), BF16),
        compiler_params=_cparams(("parallel", "parallel")),
        name="attn_band",
    )(q, k, vt, bias)


def _silu(g):
    return g / (1.0 + jnp.exp(-g))


def _out_ffn_kernel(h_ref, ya_ref, yb_ref, woa_ref, wob_ref, gff_ref, wg_ref, wu_ref, wd_ref, o_ref):
    h1 = (h_ref[...] + jnp.dot(ya_ref[...], woa_ref[...], preferred_element_type=F32)
          + jnp.dot(yb_ref[...], wob_ref[...], preferred_element_type=F32))
    xn = (h1 * _rms_scale(h1) * gff_ref[...]).astype(BF16)
    g = jnp.dot(xn, wg_ref[...], preferred_element_type=F32)
    u = jnp.dot(xn, wu_ref[...], preferred_element_type=F32)
    a = (_silu(g) * u).astype(BF16)
    o_ref[...] = h1 + jnp.dot(a, wd_ref[...], preferred_element_type=F32)


def _out_ffn(h, ya, yb, woa, wob, gff, wg, wu, wd):
    t = h.shape[0]
    tm = TM_PROJ
    row = lambda w: pl.BlockSpec((tm, w), lambda i: (i, 0))
    return pl.pallas_call(
        _out_ffn_kernel,
        grid=(t // tm,),
        in_specs=[row(D_MODEL), row(ya.shape[1]), row(yb.shape[1])]
        + [_const_spec(w.shape) for w in (woa, wob, gff, wg, wu, wd)],
        out_specs=row(D_MODEL),
        out_shape=jax.ShapeDtypeStruct((t, D_MODEL), F32),
        compiler_params=_cparams(("parallel",)),
        name="out_ffn",
    )(h, ya, yb, woa, wob, gff, wg, wu, wd)


def _proj_odd_kernel(h_ref, gmix_ref, w_ref, wvt_ref, gq_ref, gk_ref, cos_ref, sa_ref, sb_ref,
                     q_ref, k_ref, vt_ref):
    x = h_ref[...]
    hn = (x * _rms_scale(x) * gmix_ref[...]).astype(BF16)
    u = jnp.dot(hn, w_ref[...], preferred_element_type=F32)
    vt_ref[...] = _dot_nt(wvt_ref[...], hn).astype(BF16)
    lo = _lane_iota() < 64
    cos_t = cos_ref[...]
    sin_a = sa_ref[...]
    sin_b = sb_ref[...]
    width = C_HEADS * 2 * C_DH

    def norm_rope(blk, g):
        sq = blk * blk
        tot = jnp.sum(sq, axis=-1, keepdims=True)
        s_lo = jnp.sum(jnp.where(lo, sq, 0.0), axis=-1, keepdims=True)
        sc = jnp.where(lo, lax.rsqrt(s_lo * (1.0 / C_DH) + EPS),
                       lax.rsqrt((tot - s_lo) * (1.0 / C_DH) + EPS))
        xg = blk * sc * g
        return (xg * cos_t + pltpu.roll(xg, LANES - C_ROPE // 2, 1) * sin_a
                + pltpu.roll(xg, C_ROPE // 2, 1) * sin_b).astype(BF16)

    for hd in range(C_HEADS):
        sl = slice(hd * LANES, (hd + 1) * LANES)
        q_ref[:, sl] = norm_rope(u[:, hd * LANES:(hd + 1) * LANES], gq_ref[...])
        k_ref[:, sl] = norm_rope(u[:, width + hd * LANES:width + (hd + 1) * LANES], gk_ref[...])


def _proj_odd(h, gmix, w, wvt, gq, gk, cos_t, sin_a, sin_b):
    t = h.shape[0]
    tm = TM_PROJ
    row = lambda wd: pl.BlockSpec((tm, wd), lambda i: (i, 0))
    width = C_HEADS * 2 * C_DH
    consts = (gmix, w, wvt, gq, gk)
    return pl.pallas_call(
        _proj_odd_kernel,
        grid=(t // tm,),
        in_specs=[row(D_MODEL)] + [_const_spec(c.shape) for c in consts] + [row(LANES)] * 3,
        out_specs=[row(width), row(width), pl.BlockSpec((width, tm), lambda i: (0, i))],
        out_shape=[jax.ShapeDtypeStruct((t, width), BF16), jax.ShapeDtypeStruct((t, width), BF16),
                   jax.ShapeDtypeStruct((width, t), BF16)],
        compiler_params=_cparams(("parallel",)),
        name="proj_odd",
    )(h, *consts, cos_t, sin_a, sin_b)


def _out_router_kernel(h_ref, y_ref, wo_ref, gff_ref, wr_ref, h1_ref, xn_ref, idx_ref, gate_ref):
    h1 = h_ref[...] + jnp.dot(y_ref[...], wo_ref[...], preferred_element_type=F32)
    h1_ref[...] = h1
    xn = h1 * _rms_scale(h1) * gff_ref[...]
    xn_ref[...] = xn
    logits = [jnp.sum(xn * wr_ref[e:e + 1, :], axis=-1, keepdims=True) for e in range(N_EXPERTS)]
    m1 = functools.reduce(jnp.maximum, logits)
    idx1 = functools.reduce(jnp.minimum,
                            [jnp.where(logits[e] == m1, e, N_EXPERTS) for e in range(N_EXPERTS)])
    rest = [jnp.where(idx1 == e, -jnp.inf, logits[e]) for e in range(N_EXPERTS)]
    m2 = functools.reduce(jnp.maximum, rest)
    idx2 = functools.reduce(jnp.minimum,
                            [jnp.where(rest[e] == m2, e, N_EXPERTS) for e in range(N_EXPERTS)])
    e2 = jnp.exp(m2 - m1)
    g1 = 1.0 / (1.0 + e2)
    g2 = e2 * g1
    lane = _lane_iota()
    idx_ref[...] = jnp.where(lane == 0, idx1, jnp.where(lane == 1, idx2, 0))
    gate_ref[...] = jnp.where(lane == 0, g1, jnp.where(lane == 1, g2, 0.0))


def _out_router(h, y, wo, gff, wr):
    t = h.shape[0]
    tm = TM_PROJ
    row = lambda w: pl.BlockSpec((tm, w), lambda i: (i, 0))
    return pl.pallas_call(
        _out_router_kernel,
        grid=(t // tm,),
        in_specs=[row(D_MODEL), row(y.shape[1]), _const_spec(wo.shape), _const_spec(gff.shape),
                  _const_spec(wr.shape)],
        out_specs=[row(D_MODEL), row(D_MODEL), row(LANES), row(LANES)],
        out_shape=[jax.ShapeDtypeStruct((t, D_MODEL), F32), jax.ShapeDtypeStruct((t, D_MODEL), F32),
                   jax.ShapeDtypeStruct((t, LANES), jnp.int32), jax.ShapeDtypeStruct((t, LANES), F32)],
        compiler_params=_cparams(("parallel",)),
        name="out_router",
    )(h, y, wo, gff, wr)


def _row_copy(src_ref, src_row, dst_ref, dst_row, sem):
    return pltpu.make_async_copy(src_ref.at[pl.ds(src_row, 1)], dst_ref.at[pl.ds(dst_row, 1)], sem)


def _dispatch_kernel(row_ref, x_ref, xs_ref, sem):
    tm = x_ref.shape[0]
    base = pl.program_id(0) * tm

    def issue(r, carry):
        for k in range(2):
            _row_copy(x_ref, r, xs_ref, row_ref[2 * (base + r) + k], sem).start()
        return carry

    lax.fori_loop(0, tm, issue, 0, unroll=DMA_UNROLL)
    for k in range(2):
        pltpu.make_async_copy(x_ref, xs_ref.at[pl.ds(0, tm)], sem).wait()


def _dispatch(row_of_pair, xn):
    t, d = xn.shape
    tm = TM_DISPATCH
    grid_spec = pltpu.PrefetchScalarGridSpec(
        num_scalar_prefetch=1,
        grid=(t // tm,),
        in_specs=[pl.BlockSpec((tm, d), lambda i, rows: (i, 0))],
        out_specs=pl.BlockSpec(memory_space=pl.ANY),
        scratch_shapes=[pltpu.SemaphoreType.DMA(())],
    )
    return pl.pallas_call(
        _dispatch_kernel,
        grid_spec=grid_spec,
        out_shape=jax.ShapeDtypeStruct((2 * t, d), xn.dtype),
        compiler_params=_cparams(("arbitrary",)),
        name="moe_dispatch",
    )(row_of_pair, xn)


def _moe_kernel(vt_ref, ve_ref, lo_ref, hi_ref, first_ref, x_ref, wg_ref, wu_ref, wd_ref, o_ref):
    v = pl.program_id(0)
    c = pl.program_id(1)
    tr = x_ref.shape[0]

    @pl.when((c == 0) & (first_ref[v] == 1))
    def _():
        o_ref[...] = jnp.zeros_like(o_ref)

    @pl.when(lo_ref[v] < hi_ref[v])
    def _():
        x = x_ref[...].astype(BF16)
        g = jnp.dot(x, wg_ref[0].astype(BF16), preferred_element_type=F32)
        u = jnp.dot(x, wu_ref[0].astype(BF16), preferred_element_type=F32)
        a = (_silu(g) * u).astype(BF16)
        y = jnp.dot(a, wd_ref[0].astype(BF16), preferred_element_type=F32)
        rows = vt_ref[v] * tr + lax.broadcasted_iota(jnp.int32, (tr, 1), 0)
        mine = (rows >= lo_ref[v]) & (rows < hi_ref[v])
        o_ref[...] += jnp.where(mine, y, 0.0)


def _moe(visits, xs, wg, wu, wd):
    r = xs.shape[0]
    tr, fc = TR_MOE, FC_MOE
    nc = FF_EXPERT // fc
    n_visits = r // tr + N_EXPERTS - 1

    def chunk(v, c, lo, hi):
        return jnp.where(lo[v] < hi[v], c, nc - 1)

    tile = lambda v, c, vt, ve, lo, hi, first: (vt[v], 0)
    wcol = lambda v, c, vt, ve, lo, hi, first: (ve[v], 0, chunk(v, c, lo, hi))
    wrow = lambda v, c, vt, ve, lo, hi, first: (ve[v], chunk(v, c, lo, hi), 0)
    grid_spec = pltpu.PrefetchScalarGridSpec(
        num_scalar_prefetch=5,
        grid=(n_visits, nc),
        in_specs=[pl.BlockSpec((tr, D_MODEL), tile),
                  pl.BlockSpec((1, D_MODEL, fc), wcol),
                  pl.BlockSpec((1, D_MODEL, fc), wcol),
                  pl.BlockSpec((1, fc, D_MODEL), wrow)],
        out_specs=pl.BlockSpec((tr, D_MODEL), tile),
    )
    return pl.pallas_call(
        _moe_kernel,
        grid_spec=grid_spec,
        out_shape=jax.ShapeDtypeStruct((r, D_MODEL), F32),
        compiler_params=_cparams(("arbitrary", "arbitrary")),
        name="moe_grouped",
    )(*visits, xs, wg, wu, wd)


def _combine_kernel(row_ref, h_ref, gate_ref, ys_ref, o_ref, buf, sem):
    tm = h_ref.shape[0]
    base = pl.program_id(0) * tm

    def issue(r, carry):
        for k in range(2):
            _row_copy(ys_ref, row_ref[2 * (base + r) + k], buf, k * tm + r, sem).start()
        return carry

    lax.fori_loop(0, tm, issue, 0, unroll=DMA_UNROLL)
    pltpu.make_async_copy(ys_ref.at[pl.ds(0, 2 * tm)], buf, sem).wait()
    gate = gate_ref[...]
    o_ref[...] = h_ref[...] + gate[:, 0:1] * buf[0:tm, :] + gate[:, 1:2] * buf[tm:2 * tm, :]


def _combine(row_of_pair, h1, gates, ys):
    t, d = h1.shape
    tm = TM_DISPATCH
    grid_spec = pltpu.PrefetchScalarGridSpec(
        num_scalar_prefetch=1,
        grid=(t // tm,),
        in_specs=[pl.BlockSpec((tm, d), lambda i, rows: (i, 0)),
                  pl.BlockSpec((tm, LANES), lambda i, rows: (i, 0)),
                  pl.BlockSpec(memory_space=pl.ANY)],
        out_specs=pl.BlockSpec((tm, d), lambda i, rows: (i, 0)),
        scratch_shapes=[pltpu.VMEM((2 * tm, d), F32), pltpu.SemaphoreType.DMA(())],
    )
    return pl.pallas_call(
        _combine_kernel,
        grid_spec=grid_spec,
        out_shape=jax.ShapeDtypeStruct((t, d), F32),
        compiler_params=_cparams(("arbitrary",)),
        name="moe_combine",
    )(row_of_pair, h1, gates, ys)


def _route_plan(idx):
    n_pairs = idx.size
    tr = TR_MOE
    n_tiles = n_pairs // tr
    n_visits = n_tiles + N_EXPERTS - 1
    e_flat = idx.reshape(-1)
    onehot = (e_flat[:, None] == jnp.arange(N_EXPERTS, dtype=jnp.int32)[None, :]).astype(jnp.int32)
    csum = jnp.cumsum(onehot, axis=0)
    counts = csum[-1]
    ends = jnp.cumsum(counts)
    starts = ends - counts
    pos = jnp.sum((csum - onehot) * onehot, axis=1)
    row_of_pair = (starts[e_flat] + pos).astype(jnp.int32)
    nv = jnp.where(counts > 0, (ends - 1) // tr - starts // tr + 1, 0)
    vend = jnp.cumsum(nv)
    vid = jnp.arange(n_visits, dtype=jnp.int32)
    live = vid < vend[-1]
    ve = jnp.minimum(jnp.sum((vid[:, None] >= vend[None, :]).astype(jnp.int32), axis=1), N_EXPERTS - 1)
    vt = jnp.where(live, starts[ve] // tr + vid - (vend - nv)[ve], n_tiles - 1)
    lo = jnp.where(live, jnp.maximum(starts[ve], vt * tr), 0)
    hi = jnp.where(live, jnp.minimum(ends[ve], (vt + 1) * tr), 0)
    last_e = jnp.max(jnp.where(counts > 0, jnp.arange(N_EXPERTS, dtype=jnp.int32), 0))
    ve = jnp.where(live, ve, last_e)
    first = (live & (lo == vt * tr)).astype(jnp.int32)
    i32 = lambda a: a.astype(jnp.int32)
    return row_of_pair, (i32(vt), i32(ve), i32(lo), i32(hi), first)


def _rope_tables(positions, dim, theta):
    inv = 1.0 / (theta ** (jnp.arange(0, dim, 2, dtype=F32) / dim))
    ang = positions.astype(F32).reshape(-1, 1) * inv
    return jnp.cos(ang), jnp.sin(ang)


def _tables_a(positions):
    cos, sin = _rope_tables(positions, A_ROPE, A_THETA)
    t = cos.shape[0]
    one, z16, z32 = jnp.ones((t, A_NOPE), F32), jnp.zeros((t, 16), F32), jnp.zeros((t, 32), F32)
    z64 = jnp.zeros((t, 64), F32)
    return (jnp.concatenate([one, cos, cos, z32], 1), jnp.concatenate([z64, -sin, z16, z32], 1),
            jnp.concatenate([z64, z16, sin, z32], 1))


def _tables_c(positions):
    cos, sin = _rope_tables(positions, C_ROPE, ROPE_THETA)
    t = cos.shape[0]
    one, z8, z48 = jnp.ones((t, 48), F32), jnp.zeros((t, 8), F32), jnp.zeros((t, 48), F32)
    two = lambda a: jnp.concatenate([a, a], 1)
    return (two(jnp.concatenate([cos, cos, one], 1)), two(jnp.concatenate([-sin, z8, z48], 1)),
            two(jnp.concatenate([z8, sin, z48], 1)))


def _band_bias(rel_table):
    nh = rel_table.shape[0]
    nk, nq = BAND_KEYS, TQ_BAND
    length = BAND_PROFILE
    n_lo = nk - 1 - B_LEFT_CHUNKS * CHUNK - B_REL_CLIP
    n_hi = length - n_lo - rel_table.shape[1]
    prof = jnp.concatenate([jnp.broadcast_to(rel_table[:, :1], (nh, n_lo)), rel_table,
                            jnp.broadcast_to(rel_table[:, -1:], (nh, n_hi))], axis=1)
    return pl.pallas_call(
        _band_bias_kernel,
        grid=(nh,),
        in_specs=[pl.BlockSpec((None, 1, length), lambda h: (h, 0, 0))],
        out_specs=pl.BlockSpec((None, nk, nq), lambda h: (h, 0, 0)),
        out_shape=jax.ShapeDtypeStruct((nh, nk, nq), F32),
        compiler_params=_cparams(("parallel",)),
        name="band_bias",
    )(prof.astype(F32).reshape(nh, 1, length))


def _band_bias_kernel(p_ref, o_ref):
    nk, nq = o_ref.shape
    length = p_ref.shape[1]
    x = jnp.broadcast_to(p_ref[...] * LOG2E, (8, length))
    sub = lax.broadcasted_iota(jnp.int32, (8, nq), 0)
    qchunk = lax.broadcasted_iota(jnp.int32, (8, nq), 1) // CHUNK

    def body(jb, carry):
        rolled = pltpu.roll(x, jb * 8 + (length - (nk - 1)), 1, stride=1, stride_axis=0)
        cdiff = qchunk + B_LEFT_CHUNKS - (jb * 8 + sub) // CHUNK
        allowed = (cdiff >= 0) & (cdiff <= B_LEFT_CHUNKS)
        o_ref[pl.ds(pl.multiple_of(jb * 8, 8), 8), :] = jnp.where(allowed, rolled[:, :nq], NEG)
        return carry

    lax.fori_loop(0, nk // 8, body, 0)


def _pad_last(a, width):
    return jnp.pad(a, [(0, 0)] * (a.ndim - 1) + [(0, width - a.shape[-1])])


def kernel(x, positions, ev_norm_mix, ev_w_in, ev_g_cq, ev_g_ckv, ev_w_uq, ev_w_ukv, ev_a_qnorm, ev_a_knorm, ev_b_qnorm, ev_b_knorm, ev_b_rel_bias, ev_w_out, ev_norm_ff, ev_ff_gate, ev_ff_up, ev_ff_down, od_norm_mix, od_w_qkv, od_c_qnorm, od_c_knorm, od_lam_q1, od_lam_k1, od_lam_q2, od_lam_k2, od_c_subnorm, od_w_out, od_norm_ff, od_router, od_ex_gate, od_ex_up, od_ex_down):
    batch, seq, d = x.shape
    t = batch * seq
    depth = ev_w_in.shape[0] + od_w_qkv.shape[0]
    h = x.reshape(t, d)
    row = lambda a: a.reshape(1, -1).astype(F32)
    tab_a = _tables_a(positions)
    tab_c = _tables_c(positions)

    for layer in range(depth):
        i = layer // 2
        if layer % 2 == 0:
            w = ev_w_in[i]
            o1, o2, o3 = A_Q_LORA, A_Q_LORA + A_KV_LORA, A_Q_LORA + A_KV_LORA + A_ROPE
            zc = lambda n: jnp.zeros((d, n), F32)
            bw = B_HEADS * B_DH
            win = jnp.concatenate([w[:, :o2], zc(64), w[:, o2:o3], zc(32), w[:, o3:o3 + 2 * bw]], 1).astype(BF16)
            wvbt = w[:, o3 + 2 * bw:].T.astype(BF16)
            wuq = _pad_last(ev_w_uq[i], LANES).reshape(A_Q_LORA, A_HEADS * LANES).astype(BF16)
            wuk = _pad_last(ev_w_ukv[i][:, :, :A_NOPE], LANES).reshape(A_KV_LORA, A_HEADS * LANES).astype(BF16)
            wuvt = ev_w_ukv[i][:, :, A_NOPE:].reshape(A_KV_LORA, A_HEADS * A_V).T.astype(BF16)
            sc_a = (A_NOPE + A_ROPE) ** -0.5 * LOG2E
            gq = jnp.tile(_pad_last(ev_a_qnorm[i], LANES), A_HEADS).reshape(1, -1) * sc_a
            gk = jnp.tile(_pad_last(ev_a_knorm[i][:A_NOPE], LANES), A_HEADS).reshape(1, -1)
            gkr = jnp.concatenate([jnp.zeros((A_NOPE,), F32), ev_a_knorm[i][A_NOPE:],
                                   jnp.zeros((32,), F32)]).reshape(1, -1)
            gbq = jnp.tile(ev_b_qnorm[i], B_HEADS).reshape(1, -1) * (B_DH ** -0.5 * LOG2E)
            gbk = jnp.tile(ev_b_knorm[i], B_HEADS).reshape(1, -1)
            qa, ka, vat, qb, kb, vbt = _proj_even(
                h, row(ev_norm_mix[i]), win, wvbt, row(ev_g_cq[i]), row(ev_g_ckv[i]), wuq, wuk, wuvt,
                gq, gk, gkr, gbq, gbk, *tab_a)
            ya = _attn_causal(_attn_mla_kernel, qa, ka, vat, (), batch=batch, seq=seq,
                              n_blocks=A_HEADS // 2, qk_width=2 * LANES)
            yb = _attn_band(qb, kb, vbt, _band_bias(ev_b_rel_bias[i]), batch=batch, seq=seq)
            wo = ev_w_out[i].astype(BF16)
            na = A_HEADS * A_V
            padf = lambda a: _pad_last(a, FF_DENSE_PAD).astype(BF16)
            wd = jnp.pad(ev_ff_down[i], ((0, FF_DENSE_PAD - FF_DENSE), (0, 0))).astype(BF16)
            h = _out_ffn(h, ya, yb, wo[:na], wo[na:], row(ev_norm_ff[i]),
                         padf(ev_ff_gate[i]), padf(ev_ff_up[i]), wd)
        else:
            lam_init = 0.8 - 0.6 * math.exp(-0.3 * layer)
            gq = jnp.tile(od_c_qnorm[i], 2).reshape(1, -1) * (C_DH ** -0.5 * LOG2E)
            gk = jnp.tile(od_c_knorm[i], 2).reshape(1, -1)
            c_mix = C_HEADS * 2 * C_DH
            wqkv = od_w_qkv[i]
            q, k, v = _proj_odd(h, row(od_norm_mix[i]), wqkv[:, :2 * c_mix].astype(BF16),
                                wqkv[:, 2 * c_mix:].T.astype(BF16), gq, gk, *tab_c)
            extras = (row(od_lam_q1[i]), row(od_lam_k1[i]), row(od_lam_q2[i]), row(od_lam_k2[i]),
                      row(od_c_subnorm[i]))
            yc = _attn_causal(functools.partial(_attn_diff_kernel, lam_init=lam_init), q, k, v, extras,
                              batch=batch, seq=seq, n_blocks=C_HEADS, qk_width=LANES)
            h1, xn, idx, gates = _out_router(h, yc, od_w_out[i].astype(BF16), row(od_norm_ff[i]),
                                             od_router[i].T.astype(F32))
            row_of_pair, visits = _route_plan(idx[:, :2])
            xs = _dispatch(row_of_pair, xn)
            ys = _moe(visits, xs, od_ex_gate[i], od_ex_up[i], od_ex_down[i])
            h = _combine(row_of_pair, h1, gates, ys)
    return h.reshape(batch, seq, d)
```

```python
import functools
import math

import numpy as np
import jax
import jax.numpy as jnp
from jax import lax
from jax.experimental import pallas as pl
from jax.experimental.pallas import tpu as pltpu

F32 = jnp.float32
BF16 = jnp.bfloat16

D_MODEL = 1024
CHUNK = 64
EPS = 1e-6
LOG2E = 1.4426950408889634
NEG = -1e30

A_HEADS = 8
A_NOPE = 64
A_ROPE = 32
A_V = 64
A_Q_LORA = 256
A_KV_LORA = 128
A_THETA = 10000.0
B_HEADS = 8
B_DH = 64
B_LEFT_CHUNKS = 8
B_REL_CLIP = 128
C_HEADS = 8
C_DH = 64
C_ROPE = C_DH // 4
ROPE_THETA = 500000.0
FF_DENSE = 2752
N_EXPERTS = 8
FF_EXPERT = 3584

LANES = 128
FF_DENSE_PAD = 2816
VMEM_LIMIT = 56 * 1024 * 1024

TM_PROJ = 512
TQ_CAUSAL = 256
TQ_BAND = 256
BAND_KEYS = TQ_BAND + B_LEFT_CHUNKS * CHUNK
BAND_PROFILE = 1024
TM_DISPATCH = 256
DMA_UNROLL = 8
TR_MOE = 1024
FC_MOE = 896


def _cparams(sem):
    return pltpu.CompilerParams(dimension_semantics=sem, vmem_limit_bytes=VMEM_LIMIT)


def _const_spec(shape):
    nd = len(shape)
    return pl.BlockSpec(shape, lambda *_: (0,) * nd, pipeline_mode=pl.Buffered(1))


def _rms_scale(x):
    return lax.rsqrt(jnp.mean(x * x, axis=-1, keepdims=True) + EPS)


def _lane_iota():
    return lax.broadcasted_iota(jnp.int32, (1, LANES), 1)


def _dot_nt(a, b):
    return lax.dot_general(a, b, (((1,), (1,)), ((), ())), preferred_element_type=F32)


def _proj_even_kernel(h_ref, gmix_ref, win_ref, wvbt_ref, gcq_ref, gckv_ref, wuq_ref, wuk_ref, wuvt_ref,
                      gq_ref, gk_ref, gkr_ref, gbq_ref, gbk_ref, cos_ref, sa_ref, sb_ref,
                      qa_ref, ka_ref, vat_ref, qb_ref, kb_ref, vbt_ref):
    x = h_ref[...]
    hn = (x * _rms_scale(x) * gmix_ref[...]).astype(BF16)
    u = jnp.dot(hn, win_ref[...], preferred_element_type=F32)
    vbt_ref[...] = _dot_nt(wvbt_ref[...], hn).astype(BF16)
    lane = _lane_iota()
    lo = lane < 64
    cos_t = cos_ref[...]
    sin_a = sa_ref[...]
    sin_b = sb_ref[...]

    def rope(xg):
        return (xg * cos_t + pltpu.roll(xg, LANES - A_ROPE // 2, 1) * sin_a
                + pltpu.roll(xg, A_ROPE // 2, 1) * sin_b)

    cq = u[:, 0:A_Q_LORA]
    cqn = (cq * _rms_scale(cq) * gcq_ref[...]).astype(BF16)
    q = jnp.dot(cqn, wuq_ref[...], preferred_element_type=F32)
    ckv = u[:, A_Q_LORA:A_Q_LORA + A_KV_LORA]
    ckvn = (ckv * _rms_scale(ckv) * gckv_ref[...]).astype(BF16)
    kn = jnp.dot(ckvn, wuk_ref[...], preferred_element_type=F32)
    vat_ref[...] = _dot_nt(wuvt_ref[...], ckvn).astype(BF16)

    kr = u[:, 384:512]
    kr_s = lax.rsqrt(jnp.sum(kr * kr, axis=-1, keepdims=True) * (1.0 / A_ROPE) + EPS)
    krot = rope(kr * kr_s * gkr_ref[...])

    for hd in range(A_HEADS):
        sl = slice(hd * LANES, (hd + 1) * LANES)
        blk = q[:, sl]
        sq = blk * blk
        tot = jnp.sum(sq, axis=-1, keepdims=True)
        ssn = jnp.sum(jnp.where(lo, sq, 0.0), axis=-1, keepdims=True)
        sc = jnp.where(lo, lax.rsqrt(ssn * (1.0 / A_NOPE) + EPS),
                       lax.rsqrt((tot - ssn) * (1.0 / A_ROPE) + EPS))
        qa_ref[:, sl] = rope(blk * sc * gq_ref[:, sl]).astype(BF16)
        kb_ = kn[:, sl]
        ks = lax.rsqrt(jnp.sum(kb_ * kb_, axis=-1, keepdims=True) * (1.0 / A_NOPE) + EPS)
        ka_ref[:, sl] = (kb_ * ks * gk_ref[:, sl] + krot).astype(BF16)

    def pair_norm(blk, g):
        sq = blk * blk
        tot = jnp.sum(sq, axis=-1, keepdims=True)
        s_lo = jnp.sum(jnp.where(lo, sq, 0.0), axis=-1, keepdims=True)
        sc = jnp.where(lo, lax.rsqrt(s_lo * (1.0 / B_DH) + EPS),
                       lax.rsqrt((tot - s_lo) * (1.0 / B_DH) + EPS))
        return (blk * sc * g).astype(BF16)

    for p in range(B_HEADS // 2):
        sl = slice(p * LANES, (p + 1) * LANES)
        qb_ref[:, sl] = pair_norm(u[:, 512 + p * LANES:512 + (p + 1) * LANES], gbq_ref[:, sl])
        kb_ref[:, sl] = pair_norm(u[:, 1024 + p * LANES:1024 + (p + 1) * LANES], gbk_ref[:, sl])


def _proj_even(h, gmix, win, wvbt, gcq, gckv, wuq, wuk, wuvt, gq, gk, gkr, gbq, gbk, cos_t, sin_a, sin_b):
    t = h.shape[0]
    tm = TM_PROJ
    row = lambda w: pl.BlockSpec((tm, w), lambda i: (i, 0))
    col = lambda w: pl.BlockSpec((w, tm), lambda i: (0, i))
    rshape = lambda w: jax.ShapeDtypeStruct((t, w), BF16)
    cshape = lambda w: jax.ShapeDtypeStruct((w, t), BF16)
    consts = (gmix, win, wvbt, gcq, gckv, wuq, wuk, wuvt, gq, gk, gkr, gbq, gbk)
    return pl.pallas_call(
        _proj_even_kernel,
        grid=(t // tm,),
        in_specs=[row(D_MODEL)] + [_const_spec(c.shape) for c in consts] + [row(LANES)] * 3,
        out_specs=[row(1024), row(1024), col(512), row(512), row(512), col(512)],
        out_shape=[rshape(1024), rshape(1024), cshape(512), rshape(512), rshape(512), cshape(512)],
        compiler_params=_cparams(("parallel",)),
        name="proj_even",
    )(h, *consts, cos_t, sin_a, sin_b)


def _causal_streams(q_ref, k_ref, vt_ref, c, *, split):
    tq = TQ_CAUSAL
    q = q_ref[c * tq:(c + 1) * tq, :]
    zero = jnp.zeros_like(q[:, :LANES])
    if split:
        q2 = jnp.concatenate([jnp.concatenate([q[:, :LANES], zero], axis=1),
                              jnp.concatenate([zero, q[:, LANES:]], axis=1)], axis=0)
    else:
        lo = _lane_iota() < 64
        q2 = jnp.concatenate([jnp.where(lo, q, zero), jnp.where(lo, zero, q)], axis=0)
    kchunk = lax.broadcasted_iota(jnp.int32, (tq, 2 * tq), 0) // CHUNK
    qchunk = (lax.broadcasted_iota(jnp.int32, (tq, 2 * tq), 1) % tq) // CHUNK
    allowed = kchunk <= qchunk
    n_past = c * tq
    s_d = jnp.where(allowed, _dot_nt(k_ref[n_past:n_past + tq, :], q2), NEG)
    mx = jnp.max(s_d, axis=0, keepdims=True)
    if c:
        s_p = _dot_nt(k_ref[0:n_past, :], q2)
        mx = jnp.maximum(mx, jnp.max(s_p, axis=0, keepdims=True))
    p_d = jnp.exp2(s_d - mx)
    l = jnp.sum(p_d, axis=0, keepdims=True)
    o = jnp.dot(vt_ref[:, n_past:n_past + tq], p_d.astype(BF16), preferred_element_type=F32)
    if c:
        p_p = jnp.exp2(s_p - mx)
        l = l + jnp.sum(p_p, axis=0, keepdims=True)
        o = o + jnp.dot(vt_ref[:, 0:n_past], p_p.astype(BF16), preferred_element_type=F32)
    o = o * (1.0 / l)
    return [o[:, :tq], o[:, tq:]]


def _row_iota(n):
    return lax.broadcasted_iota(jnp.int32, (LANES, n), 0)


def _attn_mla_kernel(q_ref, k_ref, vt_ref, o_ref):
    tq = TQ_CAUSAL
    for c in range(q_ref.shape[0] // tq):
        o0, o1 = _causal_streams(q_ref, k_ref, vt_ref, c, split=True)
        o_ref[c * tq:(c + 1) * tq, :] = jnp.where(_row_iota(tq) < 64, o0, o1).T.astype(o_ref.dtype)


def _attn_diff_kernel(q_ref, k_ref, vt_ref, lq1_ref, lk1_ref, lq2_ref, lk2_ref, gsub_ref, o_ref,
                      *, lam_init):
    tq = TQ_CAUSAL
    lam = (jnp.exp(jnp.sum(lq1_ref[...] * lk1_ref[...], axis=-1, keepdims=True))
           - jnp.exp(jnp.sum(lq2_ref[...] * lk2_ref[...], axis=-1, keepdims=True)) + lam_init)
    for c in range(q_ref.shape[0] // tq):
        o0, o1 = _causal_streams(q_ref, k_ref, vt_ref, c, split=False)
        o = (o0 - lam * o1).T
        o = o * _rms_scale(o) * gsub_ref[...] * (1.0 - lam_init)
        o_ref[c * tq:(c + 1) * tq, :] = o.astype(o_ref.dtype)


def _attn_causal(kernel, q, k, vt, extras, *, batch, seq, n_blocks, qk_width):
    in_specs = [pl.BlockSpec((seq, qk_width), lambda b, p: (b, p)),
                pl.BlockSpec((seq, qk_width), lambda b, p: (b, p)),
                pl.BlockSpec((LANES, seq), lambda b, p: (p, b))]
    in_specs += [_const_spec(e.shape) for e in extras]
    return pl.pallas_call(
        kernel,
        grid=(batch, n_blocks),
        in_specs=in_specs,
        out_specs=pl.BlockSpec((seq, LANES), lambda b, p: (b, p)),
        out_shape=jax.ShapeDtypeStruct((batch * seq, n_blocks * LANES), BF16),
        compiler_params=_cparams(("parallel", "parallel")),
        name=kernel.func.__name__ if isinstance(kernel, functools.partial) else kernel.__name__,
    )(q, k, vt, *extras)


def _attn_band_kernel(q_ref, k_ref, vt_ref, bias_ref, o_ref):
    tq = TQ_BAND
    lo = _lane_iota() < 64
    for c in range(q_ref.shape[0] // tq):
        q = q_ref[c * tq:(c + 1) * tq, :]
        zero = jnp.zeros_like(q)
        q2 = jnp.concatenate([jnp.where(lo, q, zero), jnp.where(lo, zero, q)], axis=0)
        end = (c + 1) * tq
        start = max(0, end - BAND_KEYS)
        s = _dot_nt(k_ref[start:end, :], q2) + bias_ref[BAND_KEYS - (end - start):, :]
        p = jnp.exp2(s - jnp.max(s, axis=0, keepdims=True))
        o = jnp.dot(vt_ref[:, start:end], p.astype(BF16), preferred_element_type=F32)
        o = o * (1.0 / jnp.sum(p, axis=0, keepdims=True))
        o_ref[c * tq:(c + 1) * tq, :] = jnp.where(_row_iota(tq) < 64, o[:, :tq], o[:, tq:]).T.astype(o_ref.dtype)


def _attn_band(q, k, vt, bias, *, batch, seq):
    npair = B_HEADS // 2
    return pl.pallas_call(
        _attn_band_kernel,
        grid=(batch, npair),
        in_specs=[pl.BlockSpec((seq, LANES), lambda b, p: (b, p)),
                  pl.BlockSpec((seq, LANES), lambda b, p: (b, p)),
                  pl.BlockSpec((LANES, seq), lambda b, p: (p, b)),
                  pl.BlockSpec((None, BAND_KEYS, 2 * TQ_BAND), lambda b, p: (p, 0, 0))],
        out_specs=pl.BlockSpec((seq, LANES), lambda b, p: (b, p)),
        out_shape=jax.ShapeDtypeStruct((batch * seq, npair * LANES), BF16),
        compiler_params=_cparams(("parallel", "parallel")),
        name="attn_band",
    )(q, k, vt, bias)


def _silu(g):
    return g / (1.0 + jnp.exp(-g))


def _out_ffn_kernel(h_ref, ya_ref, yb_ref, woa_ref, wob_ref, gff_ref, wg_ref, wu_ref, wd_ref, o_ref):
    h1 = (h_ref[...] + jnp.dot(ya_ref[...], woa_ref[...], preferred_element_type=F32)
          + jnp.dot(yb_ref[...], wob_ref[...], preferred_element_type=F32))
    xn = (h1 * _rms_scale(h1) * gff_ref[...]).astype(BF16)
    g = jnp.dot(xn, wg_ref[...], preferred_element_type=F32)
    u = jnp.dot(xn, wu_ref[...], preferred_element_type=F32)
    a = (_silu(g) * u).astype(BF16)
    o_ref[...] = h1 + jnp.dot(a, wd_ref[...], preferred_element_type=F32)


def _out_ffn(h, ya, yb, woa, wob, gff, wg, wu, wd):
    t = h.shape[0]
    tm = TM_PROJ
    row = lambda w: pl.BlockSpec((tm, w), lambda i: (i, 0))
    return pl.pallas_call(
        _out_ffn_kernel,
        grid=(t // tm,),
        in_specs=[row(D_MODEL), row(ya.shape[1]), row(yb.shape[1])]
        + [_const_spec(w.shape) for w in (woa, wob, gff, wg, wu, wd)],
        out_specs=row(D_MODEL),
        out_shape=jax.ShapeDtypeStruct((t, D_MODEL), F32),
        compiler_params=_cparams(("parallel",)),
        name="out_ffn",
    )(h, ya, yb, woa, wob, gff, wg, wu, wd)


def _proj_odd_kernel(h_ref, gmix_ref, w_ref, wvt_ref, gq_ref, gk_ref, cos_ref, sa_ref, sb_ref,
                     q_ref, k_ref, vt_ref):
    x = h_ref[...]
    hn = (x * _rms_scale(x) * gmix_ref[...]).astype(BF16)
    u = jnp.dot(hn, w_ref[...], preferred_element_type=F32)
    vt_ref[...] = _dot_nt(wvt_ref[...], hn).astype(BF16)
    lo = _lane_iota() < 64
    cos_t = cos_ref[...]
    sin_a = sa_ref[...]
    sin_b = sb_ref[...]
    width = C_HEADS * 2 * C_DH

    def norm_rope(blk, g):
        sq = blk * blk
        tot = jnp.sum(sq, axis=-1, keepdims=True)
        s_lo = jnp.sum(jnp.where(lo, sq, 0.0), axis=-1, keepdims=True)
        sc = jnp.where(lo, lax.rsqrt(s_lo * (1.0 / C_DH) + EPS),
                       lax.rsqrt((tot - s_lo) * (1.0 / C_DH) + EPS))
        xg = blk * sc * g
        return (xg * cos_t + pltpu.roll(xg, LANES - C_ROPE // 2, 1) * sin_a
                + pltpu.roll(xg, C_ROPE // 2, 1) * sin_b).astype(BF16)

    for hd in range(C_HEADS):
        sl = slice(hd * LANES, (hd + 1) * LANES)
        q_ref[:, sl] = norm_rope(u[:, hd * LANES:(hd + 1) * LANES], gq_ref[...])
        k_ref[:, sl] = norm_rope(u[:, width + hd * LANES:width + (hd + 1) * LANES], gk_ref[...])


def _proj_odd(h, gmix, w, wvt, gq, gk, cos_t, sin_a, sin_b):
    t = h.shape[0]
    tm = TM_PROJ
    row = lambda wd: pl.BlockSpec((tm, wd), lambda i: (i, 0))
    width = C_HEADS * 2 * C_DH
    consts = (gmix, w, wvt, gq, gk)
    return pl.pallas_call(
        _proj_odd_kernel,
        grid=(t // tm,),
        in_specs=[row(D_MODEL)] + [_const_spec(c.shape) for c in consts] + [row(LANES)] * 3,
        out_specs=[row(width), row(width), pl.BlockSpec((width, tm), lambda i: (0, i))],
        out_shape=[jax.ShapeDtypeStruct((t, width), BF16), jax.ShapeDtypeStruct((t, width), BF16),
                   jax.ShapeDtypeStruct((width, t), BF16)],
        compiler_params=_cparams(("parallel",)),
        name="proj_odd",
    )(h, *consts, cos_t, sin_a, sin_b)


def _out_router_kernel(h_ref, y_ref, wo_ref, gff_ref, wr_ref, h1_ref, xn_ref, idx_ref, gate_ref):
    h1 = h_ref[...] + jnp.dot(y_ref[...], wo_ref[...], preferred_element_type=F32)
    h1_ref[...] = h1
    xn = h1 * _rms_scale(h1) * gff_ref[...]
    xn_ref[...] = xn
    logits = [jnp.sum(xn * wr_ref[e:e + 1, :], axis=-1, keepdims=True) for e in range(N_EXPERTS)]
    m1 = functools.reduce(jnp.maximum, logits)
    idx1 = functools.reduce(jnp.minimum,
                            [jnp.where(logits[e] == m1, e, N_EXPERTS) for e in range(N_EXPERTS)])
    rest = [jnp.where(idx1 == e, -jnp.inf, logits[e]) for e in range(N_EXPERTS)]
    m2 = functools.reduce(jnp.maximum, rest)
    idx2 = functools.reduce(jnp.minimum,
                            [jnp.where(rest[e] == m2, e, N_EXPERTS) for e in range(N_EXPERTS)])
    e2 = jnp.exp(m2 - m1)
    g1 = 1.0 / (1.0 + e2)
    g2 = e2 * g1
    lane = _lane_iota()
    idx_ref[...] = jnp.where(lane == 0, idx1, jnp.where(lane == 1, idx2, 0))
    gate_ref[...] = jnp.where(lane == 0, g1, jnp.where(lane == 1, g2, 0.0))


def _out_router(h, y, wo, gff, wr):
    t = h.shape[0]
    tm = TM_PROJ
    row = lambda w: pl.BlockSpec((tm, w), lambda i: (i, 0))
    return pl.pallas_call(
        _out_router_kernel,
        grid=(t // tm,),
        in_specs=[row(D_MODEL), row(y.shape[1]), _const_spec(wo.shape), _const_spec(gff.shape),
                  _const_spec(wr.shape)],
        out_specs=[row(D_MODEL), row(D_MODEL), row(LANES), row(LANES)],
        out_shape=[jax.ShapeDtypeStruct((t, D_MODEL), F32), jax.ShapeDtypeStruct((t, D_MODEL), F32),
                   jax.ShapeDtypeStruct((t, LANES), jnp.int32), jax.ShapeDtypeStruct((t, LANES), F32)],
        compiler_params=_cparams(("parallel",)),
        name="out_router",
    )(h, y, wo, gff, wr)


def _row_copy(src_ref, src_row, dst_ref, dst_row, sem):
    return pltpu.make_async_copy(src_ref.at[pl.ds(src_row, 1)], dst_ref.at[pl.ds(dst_row, 1)], sem)


def _dispatch_kernel(row_ref, x_ref, xs_ref, sem):
    tm = x_ref.shape[0]
    base = pl.program_id(0) * tm

    def issue(r, carry):
        for k in range(2):
            _row_copy(x_ref, r, xs_ref, row_ref[2 * (base + r) + k], sem).start()
        return carry

    lax.fori_loop(0, tm, issue, 0, unroll=DMA_UNROLL)
    for k in range(2):
        pltpu.make_async_copy(x_ref, xs_ref.at[pl.ds(0, tm)], sem).wait()


def _dispatch(row_of_pair, xn):
    t, d = xn.shape
    tm = TM_DISPATCH
    grid_spec = pltpu.PrefetchScalarGridSpec(
        num_scalar_prefetch=1,
        grid=(t // tm,),
        in_specs=[pl.BlockSpec((tm, d), lambda i, rows: (i, 0))],
        out_specs=pl.BlockSpec(memory_space=pl.ANY),
        scratch_shapes=[pltpu.SemaphoreType.DMA(())],
    )
    return pl.pallas_call(
        _dispatch_kernel,
        grid_spec=grid_spec,
        out_shape=jax.ShapeDtypeStruct((2 * t, d), xn.dtype),
        compiler_params=_cparams(("arbitrary",)),
        name="moe_dispatch",
    )(row_of_pair, xn)


def _moe_kernel(vt_ref, ve_ref, lo_ref, hi_ref, first_ref, x_ref, wg_ref, wu_ref, wd_ref, o_ref):
    v = pl.program_id(0)
    c = pl.program_id(1)
    tr = x_ref.shape[0]

    @pl.when((c == 0) & (first_ref[v] == 1))
    def _():
        o_ref[...] = jnp.zeros_like(o_ref)

    @pl.when(lo_ref[v] < hi_ref[v])
    def _():
        x = x_ref[...].astype(BF16)
        g = jnp.dot(x, wg_ref[0].astype(BF16), preferred_element_type=F32)
        u = jnp.dot(x, wu_ref[0].astype(BF16), preferred_element_type=F32)
        a = (_silu(g) * u).astype(BF16)
        y = jnp.dot(a, wd_ref[0].astype(BF16), preferred_element_type=F32)
        rows = vt_ref[v] * tr + lax.broadcasted_iota(jnp.int32, (tr, 1), 0)
        mine = (rows >= lo_ref[v]) & (rows < hi_ref[v])
        o_ref[...] += jnp.where(mine, y, 0.0)


def _moe(visits, xs, wg, wu, wd):
    r = xs.shape[0]
    tr, fc = TR_MOE, FC_MOE
    nc = FF_EXPERT // fc
    n_visits = r // tr + N_EXPERTS - 1

    def chunk(v, c, lo, hi):
        return jnp.where(lo[v] < hi[v], c, nc - 1)

    tile = lambda v, c, vt, ve, lo, hi, first: (vt[v], 0)
    wcol = lambda v, c, vt, ve, lo, hi, first: (ve[v], 0, chunk(v, c, lo, hi))
    wrow = lambda v, c, vt, ve, lo, hi, first: (ve[v], chunk(v, c, lo, hi), 0)
    grid_spec = pltpu.PrefetchScalarGridSpec(
        num_scalar_prefetch=5,
        grid=(n_visits, nc),
        in_specs=[pl.BlockSpec((tr, D_MODEL), tile),
                  pl.BlockSpec((1, D_MODEL, fc), wcol),
                  pl.BlockSpec((1, D_MODEL, fc), wcol),
                  pl.BlockSpec((1, fc, D_MODEL), wrow)],
        out_specs=pl.BlockSpec((tr, D_MODEL), tile),
    )
    return pl.pallas_call(
        _moe_kernel,
        grid_spec=grid_spec,
        out_shape=jax.ShapeDtypeStruct((r, D_MODEL), F32),
        compiler_params=_cparams(("arbitrary", "arbitrary")),
        name="moe_grouped",
    )(*visits, xs, wg, wu, wd)


def _combine_kernel(row_ref, h_ref, gate_ref, ys_ref, o_ref, buf, sem):
    tm = h_ref.shape[0]
    base = pl.program_id(0) * tm

    def issue(r, carry):
        for k in range(2):
            _row_copy(ys_ref, row_ref[2 * (base + r) + k], buf, k * tm + r, sem).start()
        return carry

    lax.fori_loop(0, tm, issue, 0, unroll=DMA_UNROLL)
    pltpu.make_async_copy(ys_ref.at[pl.ds(0, 2 * tm)], buf, sem).wait()
    gate = gate_ref[...]
    o_ref[...] = h_ref[...] + gate[:, 0:1] * buf[0:tm, :] + gate[:, 1:2] * buf[tm:2 * tm, :]


def _combine(row_of_pair, h1, gates, ys):
    t, d = h1.shape
    tm = TM_DISPATCH
    grid_spec = pltpu.PrefetchScalarGridSpec(
        num_scalar_prefetch=1,
        grid=(t // tm,),
        in_specs=[pl.BlockSpec((tm, d), lambda i, rows: (i, 0)),
                  pl.BlockSpec((tm, LANES), lambda i, rows: (i, 0)),
                  pl.BlockSpec(memory_space=pl.ANY)],
        out_specs=pl.BlockSpec((tm, d), lambda i, rows: (i, 0)),
        scratch_shapes=[pltpu.VMEM((2 * tm, d), F32), pltpu.SemaphoreType.DMA(())],
    )
    return pl.pallas_call(
        _combine_kernel,
        grid_spec=grid_spec,
        out_shape=jax.ShapeDtypeStruct((t, d), F32),
        compiler_params=_cparams(("arbitrary",)),
        name="moe_combine",
    )(row_of_pair, h1, gates, ys)


def _route_plan(idx):
    n_pairs = idx.size
    tr = TR_MOE
    n_tiles = n_pairs // tr
    n_visits = n_tiles + N_EXPERTS - 1
    e_flat = idx.reshape(-1)
    onehot = (e_flat[:, None] == jnp.arange(N_EXPERTS, dtype=jnp.int32)[None, :]).astype(jnp.int32)
    csum = jnp.cumsum(onehot, axis=0)
    counts = csum[-1]
    ends = jnp.cumsum(counts)
    starts = ends - counts
    pos = jnp.sum((csum - onehot) * onehot, axis=1)
    row_of_pair = (starts[e_flat] + pos).astype(jnp.int32)
    nv = jnp.where(counts > 0, (ends - 1) // tr - starts // tr + 1, 0)
    vend = jnp.cumsum(nv)
    vid = jnp.arange(n_visits, dtype=jnp.int32)
    live = vid < vend[-1]
    ve = jnp.minimum(jnp.sum((vid[:, None] >= vend[None, :]).astype(jnp.int32), axis=1), N_EXPERTS - 1)
    vt = jnp.where(live, starts[ve] // tr + vid - (vend - nv)[ve], n_tiles - 1)
    lo = jnp.where(live, jnp.maximum(starts[ve], vt * tr), 0)
    hi = jnp.where(live, jnp.minimum(ends[ve], (vt + 1) * tr), 0)
    last_e = jnp.max(jnp.where(counts > 0, jnp.arange(N_EXPERTS, dtype=jnp.int32), 0))
    ve = jnp.where(live, ve, last_e)
    first = (live & (lo == vt * tr)).astype(jnp.int32)
    i32 = lambda a: a.astype(jnp.int32)
    return row_of_pair, (i32(vt), i32(ve), i32(lo), i32(hi), first)


def _rope_tables(positions, dim, theta):
    inv = 1.0 / (theta ** (jnp.arange(0, dim, 2, dtype=F32) / dim))
    ang = positions.astype(F32).reshape(-1, 1) * inv
    return jnp.cos(ang), jnp.sin(ang)


def _tables_a(positions):
    cos, sin = _rope_tables(positions, A_ROPE, A_THETA)
    t = cos.shape[0]
    one, z16, z32 = jnp.ones((t, A_NOPE), F32), jnp.zeros((t, 16), F32), jnp.zeros((t, 32), F32)
    z64 = jnp.zeros((t, 64), F32)
    return (jnp.concatenate([one, cos, cos, z32], 1), jnp.concatenate([z64, -sin, z16, z32], 1),
            jnp.concatenate([z64, z16, sin, z32], 1))


def _tables_c(positions):
    cos, sin = _rope_tables(positions, C_ROPE, ROPE_THETA)
    t = cos.shape[0]
    one, z8, z48 = jnp.ones((t, 48), F32), jnp.zeros((t, 8), F32), jnp.zeros((t, 48), F32)
    two = lambda a: jnp.concatenate([a, a], 1)
    return (two(jnp.concatenate([cos, cos, one], 1)), two(jnp.concatenate([-sin, z8, z48], 1)),
            two(jnp.concatenate([z8, sin, z48], 1)))


def _band_bias(rel_table):
    nh = rel_table.shape[0]
    nk, nq = BAND_KEYS, TQ_BAND
    length = BAND_PROFILE
    n_lo = nk - 1 - B_LEFT_CHUNKS * CHUNK - B_REL_CLIP
    n_hi = length - n_lo - rel_table.shape[1]
    prof = jnp.concatenate([jnp.broadcast_to(rel_table[:, :1], (nh, n_lo)), rel_table,
                            jnp.broadcast_to(rel_table[:, -1:], (nh, n_hi))], axis=1)
    return pl.pallas_call(
        _band_bias_kernel,
        grid=(nh,),
        in_specs=[pl.BlockSpec((None, 1, length), lambda h: (h, 0, 0))],
        out_specs=pl.BlockSpec((None, nk, nq), lambda h: (h // 2, 0, h % 2)),
        out_shape=jax.ShapeDtypeStruct((nh // 2, nk, 2 * nq), F32),
        compiler_params=_cparams(("parallel",)),
        name="band_bias",
    )(prof.astype(F32).reshape(nh, 1, length))


def _band_bias_kernel(p_ref, o_ref):
    nk, nq = o_ref.shape
    length = p_ref.shape[1]
    x = jnp.broadcast_to(p_ref[...] * LOG2E, (8, length))
    sub = lax.broadcasted_iota(jnp.int32, (8, nq), 0)
    qchunk = lax.broadcasted_iota(jnp.int32, (8, nq), 1) // CHUNK

    def body(jb, carry):
        rolled = pltpu.roll(x, jb * 8 + (length - (nk - 1)), 1, stride=1, stride_axis=0)
        cdiff = qchunk + B_LEFT_CHUNKS - (jb * 8 + sub) // CHUNK
        allowed = (cdiff >= 0) & (cdiff <= B_LEFT_CHUNKS)
        o_ref[pl.ds(pl.multiple_of(jb * 8, 8), 8), :] = jnp.where(allowed, rolled[:, :nq], NEG)
        return carry

    lax.fori_loop(0, nk // 8, body, 0)


def _pad_last(a, width):
    return jnp.pad(a, [(0, 0)] * (a.ndim - 1) + [(0, width - a.shape[-1])])


def kernel(x, positions, ev_norm_mix, ev_w_in, ev_g_cq, ev_g_ckv, ev_w_uq, ev_w_ukv, ev_a_qnorm, ev_a_knorm, ev_b_qnorm, ev_b_knorm, ev_b_rel_bias, ev_w_out, ev_norm_ff, ev_ff_gate, ev_ff_up, ev_ff_down, od_norm_mix, od_w_qkv, od_c_qnorm, od_c_knorm, od_lam_q1, od_lam_k1, od_lam_q2, od_lam_k2, od_c_subnorm, od_w_out, od_norm_ff, od_router, od_ex_gate, od_ex_up, od_ex_down):
    batch, seq, d = x.shape
    t = batch * seq
    depth = ev_w_in.shape[0] + od_w_qkv.shape[0]
    h = x.reshape(t, d)
    row = lambda a: a.reshape(1, -1).astype(F32)
    tab_a = _tables_a(positions)
    tab_c = _tables_c(positions)

    for layer in range(depth):
        i = layer // 2
        if layer % 2 == 0:
            w = ev_w_in[i]
            o1, o2, o3 = A_Q_LORA, A_Q_LORA + A_KV_LORA, A_Q_LORA + A_KV_LORA + A_ROPE
            zc = lambda n: jnp.zeros((d, n), F32)
            bw = B_HEADS * B_DH
            win = jnp.concatenate([w[:, :o2], zc(64), w[:, o2:o3], zc(32), w[:, o3:o3 + 2 * bw]], 1).astype(BF16)
            wvbt = w[:, o3 + 2 * bw:].T.astype(BF16)
            wuq = _pad_last(ev_w_uq[i], LANES).reshape(A_Q_LORA, A_HEADS * LANES).astype(BF16)
            wuk = _pad_last(ev_w_ukv[i][:, :, :A_NOPE], LANES).reshape(A_KV_LORA, A_HEADS * LANES).astype(BF16)
            wuvt = ev_w_ukv[i][:, :, A_NOPE:].reshape(A_KV_LORA, A_HEADS * A_V).T.astype(BF16)
            sc_a = (A_NOPE + A_ROPE) ** -0.5 * LOG2E
            gq = jnp.tile(_pad_last(ev_a_qnorm[i], LANES), A_HEADS).reshape(1, -1) * sc_a
            gk = jnp.tile(_pad_last(ev_a_knorm[i][:A_NOPE], LANES), A_HEADS).reshape(1, -1)
            gkr = jnp.concatenate([jnp.zeros((A_NOPE,), F32), ev_a_knorm[i][A_NOPE:],
                                   jnp.zeros((32,), F32)]).reshape(1, -1)
            gbq = jnp.tile(ev_b_qnorm[i], B_HEADS).reshape(1, -1) * (B_DH ** -0.5 * LOG2E)
            gbk = jnp.tile(ev_b_knorm[i], B_HEADS).reshape(1, -1)
            qa, ka, vat, qb, kb, vbt = _proj_even(
                h, row(ev_norm_mix[i]), win, wvbt, row(ev_g_cq[i]), row(ev_g_ckv[i]), wuq, wuk, wuvt,
                gq, gk, gkr, gbq, gbk, *tab_a)
            ya = _attn_causal(_attn_mla_kernel, qa, ka, vat, (), batch=batch, seq=seq,
                              n_blocks=A_HEADS // 2, qk_width=2 * LANES)
            yb = _attn_band(qb, kb, vbt, _band_bias(ev_b_rel_bias[i]), batch=batch, seq=seq)
            wo = ev_w_out[i].astype(BF16)
            na = A_HEADS * A_V
            padf = lambda a: _pad_last(a, FF_DENSE_PAD).astype(BF16)
            wd = jnp.pad(ev_ff_down[i], ((0, FF_DENSE_PAD - FF_DENSE), (0, 0))).astype(BF16)
            h = _out_ffn(h, ya, yb, wo[:na], wo[na:], row(ev_norm_ff[i]),
                         padf(ev_ff_gate[i]), padf(ev_ff_up[i]), wd)
        else:
            lam_init = 0.8 - 0.6 * math.exp(-0.3 * layer)
            gq = jnp.tile(od_c_qnorm[i], 2).reshape(1, -1) * (C_DH ** -0.5 * LOG2E)
            gk = jnp.tile(od_c_knorm[i], 2).reshape(1, -1)
            c_mix = C_HEADS * 2 * C_DH
            wqkv = od_w_qkv[i]
            q, k, v = _proj_odd(h, row(od_norm_mix[i]), wqkv[:, :2 * c_mix].astype(BF16),
                                wqkv[:, 2 * c_mix:].T.astype(BF16), gq, gk, *tab_c)
            extras = (row(od_lam_q1[i]), row(od_lam_k1[i]), row(od_lam_q2[i]), row(od_lam_k2[i]),
                      row(od_c_subnorm[i]))
            yc = _attn_causal(functools.partial(_attn_diff_kernel, lam_init=lam_init), q, k, v, extras,
                              batch=batch, seq=seq, n_blocks=C_HEADS, qk_width=LANES)
            h1, xn, idx, gates = _out_router(h, yc, od_w_out[i].astype(BF16), row(od_norm_ff[i]),
                                             od_router[i].T.astype(F32))
            row_of_pair, visits = _route_plan(idx[:, :2])
            xs = _dispatch(row_of_pair, xn)
            ys = _moe(visits, xs, od_ex_gate[i], od_ex_up[i], od_ex_down[i])
            h = _combine(row_of_pair, h1, gates, ys)
    return h.reshape(batch, seq, d)
```

```python
import functools
import math

import numpy as np
import jax
import jax.numpy as jnp
from jax import lax
from jax.experimental import pallas as pl
from jax.experimental.pallas import tpu as pltpu

F32 = jnp.float32
BF16 = jnp.bfloat16

D_MODEL = 1024
CHUNK = 64
EPS = 1e-6
LOG2E = 1.4426950408889634
NEG = -1e30

A_HEADS = 8
A_NOPE = 64
A_ROPE = 32
A_V = 64
A_Q_LORA = 256
A_KV_LORA = 128
A_THETA = 10000.0
B_HEADS = 8
B_DH = 64
B_LEFT_CHUNKS = 8
B_REL_CLIP = 128
C_HEADS = 8
C_DH = 64
C_ROPE = C_DH // 4
ROPE_THETA = 500000.0
FF_DENSE = 2752
N_EXPERTS = 8
FF_EXPERT = 3584

LANES = 128
FF_DENSE_PAD = 2816
VMEM_LIMIT = 56 * 1024 * 1024

TM_PROJ = 512
TQ_CAUSAL = 256
TQ_BAND = 256
BAND_KEYS = TQ_BAND + B_LEFT_CHUNKS * CHUNK
BAND_PROFILE = 1024
TM_DISPATCH = 256
DMA_UNROLL = 8
TR_MOE = 1024
FC_MOE = 512


def _cparams(sem):
    return pltpu.CompilerParams(dimension_semantics=sem, vmem_limit_bytes=VMEM_LIMIT)


def _const_spec(shape):
    nd = len(shape)
    return pl.BlockSpec(shape, lambda *_: (0,) * nd, pipeline_mode=pl.Buffered(1))


def _rms_scale(x):
    return lax.rsqrt(jnp.mean(x * x, axis=-1, keepdims=True) + EPS)


def _lane_iota():
    return lax.broadcasted_iota(jnp.int32, (1, LANES), 1)


def _dot_nt(a, b):
    return lax.dot_general(a, b, (((1,), (1,)), ((), ())), preferred_element_type=F32)


def _proj_even_kernel(h_ref, gmix_ref, win_ref, wvbt_ref, gcq_ref, gckv_ref, wuq_ref, wuk_ref, wuvt_ref,
                      gq_ref, gk_ref, gkr_ref, gbq_ref, gbk_ref, cos_ref, sa_ref, sb_ref,
                      qa_ref, ka_ref, vat_ref, qb_ref, kb_ref, vbt_ref):
    x = h_ref[...]
    hn = (x * _rms_scale(x) * gmix_ref[...]).astype(BF16)
    u = jnp.dot(hn, win_ref[...], preferred_element_type=F32)
    vbt_ref[...] = _dot_nt(wvbt_ref[...], hn).astype(BF16)
    lane = _lane_iota()
    lo = lane < 64
    cos_t = cos_ref[...]
    sin_a = sa_ref[...]
    sin_b = sb_ref[...]

    def rope(xg):
        return (xg * cos_t + pltpu.roll(xg, LANES - A_ROPE // 2, 1) * sin_a
                + pltpu.roll(xg, A_ROPE // 2, 1) * sin_b)

    cq = u[:, 0:A_Q_LORA]
    cqn = (cq * _rms_scale(cq) * gcq_ref[...]).astype(BF16)
    q = jnp.dot(cqn, wuq_ref[...], preferred_element_type=F32)
    ckv = u[:, A_Q_LORA:A_Q_LORA + A_KV_LORA]
    ckvn = (ckv * _rms_scale(ckv) * gckv_ref[...]).astype(BF16)
    kn = jnp.dot(ckvn, wuk_ref[...], preferred_element_type=F32)
    vat_ref[...] = _dot_nt(wuvt_ref[...], ckvn).astype(BF16)

    kr = u[:, 384:512]
    kr_s = lax.rsqrt(jnp.sum(kr * kr, axis=-1, keepdims=True) * (1.0 / A_ROPE) + EPS)
    krot = rope(kr * kr_s * gkr_ref[...])

    for hd in range(A_HEADS):
        sl = slice(hd * LANES, (hd + 1) * LANES)
        blk = q[:, sl]
        sq = blk * blk
        tot = jnp.sum(sq, axis=-1, keepdims=True)
        ssn = jnp.sum(jnp.where(lo, sq, 0.0), axis=-1, keepdims=True)
        sc = jnp.where(lo, lax.rsqrt(ssn * (1.0 / A_NOPE) + EPS),
                       lax.rsqrt((tot - ssn) * (1.0 / A_ROPE) + EPS))
        qa_ref[:, sl] = rope(blk * sc * gq_ref[:, sl]).astype(BF16)
        kb_ = kn[:, sl]
        ks = lax.rsqrt(jnp.sum(kb_ * kb_, axis=-1, keepdims=True) * (1.0 / A_NOPE) + EPS)
        ka_ref[:, sl] = (kb_ * ks * gk_ref[:, sl] + krot).astype(BF16)

    def pair_norm(blk, g):
        sq = blk * blk
        tot = jnp.sum(sq, axis=-1, keepdims=True)
        s_lo = jnp.sum(jnp.where(lo, sq, 0.0), axis=-1, keepdims=True)
        sc = jnp.where(lo, lax.rsqrt(s_lo * (1.0 / B_DH) + EPS),
                       lax.rsqrt((tot - s_lo) * (1.0 / B_DH) + EPS))
        return (blk * sc * g).astype(BF16)

    for p in range(B_HEADS // 2):
        sl = slice(p * LANES, (p + 1) * LANES)
        qb_ref[:, sl] = pair_norm(u[:, 512 + p * LANES:512 + (p + 1) * LANES], gbq_ref[:, sl])
        kb_ref[:, sl] = pair_norm(u[:, 1024 + p * LANES:1024 + (p + 1) * LANES], gbk_ref[:, sl])


def _proj_even(h, gmix, win, wvbt, gcq, gckv, wuq, wuk, wuvt, gq, gk, gkr, gbq, gbk, cos_t, sin_a, sin_b):
    t = h.shape[0]
    tm = TM_PROJ
    row = lambda w: pl.BlockSpec((tm, w), lambda i: (i, 0))
    col = lambda w: pl.BlockSpec((w, tm), lambda i: (0, i))
    rshape = lambda w: jax.ShapeDtypeStruct((t, w), BF16)
    cshape = lambda w: jax.ShapeDtypeStruct((w, t), BF16)
    consts = (gmix, win, wvbt, gcq, gckv, wuq, wuk, wuvt, gq, gk, gkr, gbq, gbk)
    return pl.pallas_call(
        _proj_even_kernel,
        grid=(t // tm,),
        in_specs=[row(D_MODEL)] + [_const_spec(c.shape) for c in consts] + [row(LANES)] * 3,
        out_specs=[row(1024), row(1024), col(512), row(512), row(512), col(512)],
        out_shape=[rshape(1024), rshape(1024), cshape(512), rshape(512), rshape(512), cshape(512)],
        compiler_params=_cparams(("parallel",)),
        name="proj_even",
    )(h, *consts, cos_t, sin_a, sin_b)


def _causal_scores(q_ref, k_ref, s_ref, c, *, split):
    tq = TQ_CAUSAL
    q = q_ref[c * tq:(c + 1) * tq, :]
    zero = jnp.zeros_like(q[:, :LANES])
    if split:
        q2 = jnp.concatenate([jnp.concatenate([q[:, :LANES], zero], axis=1),
                              jnp.concatenate([zero, q[:, LANES:]], axis=1)], axis=0)
    else:
        lo = _lane_iota() < 64
        q2 = jnp.concatenate([jnp.where(lo, q, zero), jnp.where(lo, zero, q)], axis=0)
    kchunk = lax.broadcasted_iota(jnp.int32, (tq, 2 * tq), 0) // CHUNK
    qchunk = (lax.broadcasted_iota(jnp.int32, (tq, 2 * tq), 1) % tq) // CHUNK
    n_past = c * tq
    s_d = jnp.where(kchunk <= qchunk, _dot_nt(k_ref[n_past:n_past + tq, :], q2), NEG)
    s_ref[n_past:n_past + tq, :] = s_d
    mx = jnp.max(s_d, axis=0, keepdims=True)
    if c:
        s_p = _dot_nt(k_ref[0:n_past, :], q2)
        s_ref[0:n_past, :] = s_p
        mx = jnp.maximum(mx, jnp.max(s_p, axis=0, keepdims=True))
    return mx


def _causal_values(vt_ref, s_ref, c, mx):
    n_keys = (c + 1) * TQ_CAUSAL
    p = jnp.exp2(s_ref[0:n_keys, :] - mx)
    o = jnp.dot(vt_ref[:, 0:n_keys], p.astype(BF16), preferred_element_type=F32)
    return o * (1.0 / jnp.sum(p, axis=0, keepdims=True))


def _causal_tiles(q_ref, k_ref, vt_ref, s_buf, *, split):
    tq = TQ_CAUSAL
    n = q_ref.shape[0] // tq
    mx = _causal_scores(q_ref, k_ref, s_buf.at[0], 0, split=split)
    for c in range(n):
        mx_next = None
        if c + 1 < n:
            mx_next = _causal_scores(q_ref, k_ref, s_buf.at[(c + 1) % 2], c + 1, split=split)
        o = _causal_values(vt_ref, s_buf.at[c % 2], c, mx)
        yield c, o[:, :tq], o[:, tq:]
        mx = mx_next


def _row_iota(n):
    return lax.broadcasted_iota(jnp.int32, (LANES, n), 0)


def _attn_mla_kernel(q_ref, k_ref, vt_ref, o_ref, s_buf):
    tq = TQ_CAUSAL
    for c, o0, o1 in _causal_tiles(q_ref, k_ref, vt_ref, s_buf, split=True):
        o_ref[c * tq:(c + 1) * tq, :] = jnp.where(_row_iota(tq) < 64, o0, o1).T.astype(o_ref.dtype)


def _attn_diff_kernel(q_ref, k_ref, vt_ref, lq1_ref, lk1_ref, lq2_ref, lk2_ref, gsub_ref, o_ref, s_buf,
                      *, lam_init):
    tq = TQ_CAUSAL
    lam = (jnp.exp(jnp.sum(lq1_ref[...] * lk1_ref[...], axis=-1, keepdims=True))
           - jnp.exp(jnp.sum(lq2_ref[...] * lk2_ref[...], axis=-1, keepdims=True)) + lam_init)
    for c, o0, o1 in _causal_tiles(q_ref, k_ref, vt_ref, s_buf, split=False):
        o = (o0 - lam * o1).T
        o = o * _rms_scale(o) * gsub_ref[...] * (1.0 - lam_init)
        o_ref[c * tq:(c + 1) * tq, :] = o.astype(o_ref.dtype)


def _attn_causal(kernel, q, k, vt, extras, *, batch, seq, n_blocks, qk_width):
    in_specs = [pl.BlockSpec((seq, qk_width), lambda b, p: (b, p)),
                pl.BlockSpec((seq, qk_width), lambda b, p: (b, p)),
                pl.BlockSpec((LANES, seq), lambda b, p: (p, b))]
    in_specs += [_const_spec(e.shape) for e in extras]
    return pl.pallas_call(
        kernel,
        grid=(batch, n_blocks),
        in_specs=in_specs,
        out_specs=pl.BlockSpec((seq, LANES), lambda b, p: (b, p)),
        out_shape=jax.ShapeDtypeStruct((batch * seq, n_blocks * LANES), BF16),
        scratch_shapes=[pltpu.VMEM((2, seq, 2 * TQ_CAUSAL), F32)],
        compiler_params=_cparams(("parallel", "parallel")),
        name=kernel.func.__name__ if isinstance(kernel, functools.partial) else kernel.__name__,
    )(q, k, vt, *extras)


def _attn_band_kernel(q_ref, k_ref, vt_ref, bias_ref, o_ref, s_buf):
    tq = TQ_BAND
    n = q_ref.shape[0] // tq
    lo = _lane_iota() < 64

    def window(c):
        end = (c + 1) * tq
        return max(0, end - BAND_KEYS), end

    def scores(c):
        start, end = window(c)
        q = q_ref[c * tq:(c + 1) * tq, :]
        zero = jnp.zeros_like(q)
        q2 = jnp.concatenate([jnp.where(lo, q, zero), jnp.where(lo, zero, q)], axis=0)
        s = _dot_nt(k_ref[start:end, :], q2) + bias_ref[BAND_KEYS - (end - start):, :]
        s_buf[c % 2, 0:end - start, :] = s
        return jnp.max(s, axis=0, keepdims=True)

    mx = scores(0)
    for c in range(n):
        mx_next = scores(c + 1) if c + 1 < n else None
        start, end = window(c)
        p = jnp.exp2(s_buf[c % 2, 0:end - start, :] - mx)
        o = jnp.dot(vt_ref[:, start:end], p.astype(BF16), preferred_element_type=F32)
        o = o * (1.0 / jnp.sum(p, axis=0, keepdims=True))
        o_ref[c * tq:(c + 1) * tq, :] = jnp.where(_row_iota(tq) < 64, o[:, :tq], o[:, tq:]).T.astype(o_ref.dtype)
        mx = mx_next


def _attn_band(q, k, vt, bias, *, batch, seq):
    npair = B_HEADS // 2
    return pl.pallas_call(
        _attn_band_kernel,
        grid=(batch, npair),
        in_specs=[pl.BlockSpec((seq, LANES), lambda b, p: (b, p)),
                  pl.BlockSpec((seq, LANES), lambda b, p: (b, p)),
                  pl.BlockSpec((LANES, seq), lambda b, p: (p, b)),
                  pl.BlockSpec((None, BAND_KEYS, 2 * TQ_BAND), lambda b, p: (p, 0, 0))],
        out_specs=pl.BlockSpec((seq, LANES), lambda b, p: (b, p)),
        out_shape=jax.ShapeDtypeStruct((batch * seq, npair * LANES), BF16),
        scratch_shapes=[pltpu.VMEM((2, BAND_KEYS, 2 * TQ_BAND), F32)],
        compiler_params=_cparams(("parallel", "parallel")),
        name="attn_band",
    )(q, k, vt, bias)


def _silu(g):
    return g / (1.0 + jnp.exp(-g))


def _out_ffn_kernel(h_ref, ya_ref, yb_ref, woa_ref, wob_ref, gff_ref, wg_ref, wu_ref, wd_ref, o_ref):
    h1 = (h_ref[...] + jnp.dot(ya_ref[...], woa_ref[...], preferred_element_type=F32)
          + jnp.dot(yb_ref[...], wob_ref[...], preferred_element_type=F32))
    xn = (h1 * _rms_scale(h1) * gff_ref[...]).astype(BF16)
    g = jnp.dot(xn, wg_ref[...], preferred_element_type=F32)
    u = jnp.dot(xn, wu_ref[...], preferred_element_type=F32)
    a = (_silu(g) * u).astype(BF16)
    o_ref[...] = h1 + jnp.dot(a, wd_ref[...], preferred_element_type=F32)


def _out_ffn(h, ya, yb, woa, wob, gff, wg, wu, wd):
    t = h.shape[0]
    tm = TM_PROJ
    row = lambda w: pl.BlockSpec((tm, w), lambda i: (i, 0))
    return pl.pallas_call(
        _out_ffn_kernel,
        grid=(t // tm,),
        in_specs=[row(D_MODEL), row(ya.shape[1]), row(yb.shape[1])]
        + [_const_spec(w.shape) for w in (woa, wob, gff, wg, wu, wd)],
        out_specs=row(D_MODEL),
        out_shape=jax.ShapeDtypeStruct((t, D_MODEL), F32),
        compiler_params=_cparams(("parallel",)),
        name="out_ffn",
    )(h, ya, yb, woa, wob, gff, wg, wu, wd)


def _proj_odd_kernel(h_ref, gmix_ref, w_ref, wvt_ref, gq_ref, gk_ref, cos_ref, sa_ref, sb_ref,
                     q_ref, k_ref, vt_ref):
    x = h_ref[...]
    hn = (x * _rms_scale(x) * gmix_ref[...]).astype(BF16)
    u = jnp.dot(hn, w_ref[...], preferred_element_type=F32)
    vt_ref[...] = _dot_nt(wvt_ref[...], hn).astype(BF16)
    lo = _lane_iota() < 64
    cos_t = cos_ref[...]
    sin_a = sa_ref[...]
    sin_b = sb_ref[...]
    width = C_HEADS * 2 * C_DH

    def norm_rope(blk, g):
        sq = blk * blk
        tot = jnp.sum(sq, axis=-1, keepdims=True)
        s_lo = jnp.sum(jnp.where(lo, sq, 0.0), axis=-1, keepdims=True)
        sc = jnp.where(lo, lax.rsqrt(s_lo * (1.0 / C_DH) + EPS),
                       lax.rsqrt((tot - s_lo) * (1.0 / C_DH) + EPS))
        xg = blk * sc * g
        return (xg * cos_t + pltpu.roll(xg, LANES - C_ROPE // 2, 1) * sin_a
                + pltpu.roll(xg, C_ROPE // 2, 1) * sin_b).astype(BF16)

    for hd in range(C_HEADS):
        sl = slice(hd * LANES, (hd + 1) * LANES)
        q_ref[:, sl] = norm_rope(u[:, hd * LANES:(hd + 1) * LANES], gq_ref[...])
        k_ref[:, sl] = norm_rope(u[:, width + hd * LANES:width + (hd + 1) * LANES], gk_ref[...])


def _proj_odd(h, gmix, w, wvt, gq, gk, cos_t, sin_a, sin_b):
    t = h.shape[0]
    tm = TM_PROJ
    row = lambda wd: pl.BlockSpec((tm, wd), lambda i: (i, 0))
    width = C_HEADS * 2 * C_DH
    consts = (gmix, w, wvt, gq, gk)
    return pl.pallas_call(
        _proj_odd_kernel,
        grid=(t // tm,),
        in_specs=[row(D_MODEL)] + [_const_spec(c.shape) for c in consts] + [row(LANES)] * 3,
        out_specs=[row(width), row(width), pl.BlockSpec((width, tm), lambda i: (0, i))],
        out_shape=[jax.ShapeDtypeStruct((t, width), BF16), jax.ShapeDtypeStruct((t, width), BF16),
                   jax.ShapeDtypeStruct((width, t), BF16)],
        compiler_params=_cparams(("parallel",)),
        name="proj_odd",
    )(h, *consts, cos_t, sin_a, sin_b)


def _out_router_kernel(h_ref, y_ref, wo_ref, gff_ref, wr_ref, h1_ref, xn_ref, idx_ref, gate_ref):
    h1 = h_ref[...] + jnp.dot(y_ref[...], wo_ref[...], preferred_element_type=F32)
    h1_ref[...] = h1
    xn = h1 * _rms_scale(h1) * gff_ref[...]
    xn_ref[...] = xn
    logits = [jnp.sum(xn * wr_ref[e:e + 1, :], axis=-1, keepdims=True) for e in range(N_EXPERTS)]
    m1 = functools.reduce(jnp.maximum, logits)
    idx1 = functools.reduce(jnp.minimum,
                            [jnp.where(logits[e] == m1, e, N_EXPERTS) for e in range(N_EXPERTS)])
    rest = [jnp.where(idx1 == e, -jnp.inf, logits[e]) for e in range(N_EXPERTS)]
    m2 = functools.reduce(jnp.maximum, rest)
    idx2 = functools.reduce(jnp.minimum,
                            [jnp.where(rest[e] == m2, e, N_EXPERTS) for e in range(N_EXPERTS)])
    e2 = jnp.exp(m2 - m1)
    g1 = 1.0 / (1.0 + e2)
    g2 = e2 * g1
    lane = _lane_iota()
    idx_ref[...] = jnp.where(lane == 0, idx1, jnp.where(lane == 1, idx2, 0))
    gate_ref[...] = jnp.where(lane == 0, g1, jnp.where(lane == 1, g2, 0.0))


def _out_router(h, y, wo, gff, wr):
    t = h.shape[0]
    tm = TM_PROJ
    row = lambda w: pl.BlockSpec((tm, w), lambda i: (i, 0))
    return pl.pallas_call(
        _out_router_kernel,
        grid=(t // tm,),
        in_specs=[row(D_MODEL), row(y.shape[1]), _const_spec(wo.shape), _const_spec(gff.shape),
                  _const_spec(wr.shape)],
        out_specs=[row(D_MODEL), row(D_MODEL), row(LANES), row(LANES)],
        out_shape=[jax.ShapeDtypeStruct((t, D_MODEL), F32), jax.ShapeDtypeStruct((t, D_MODEL), F32),
                   jax.ShapeDtypeStruct((t, LANES), jnp.int32), jax.ShapeDtypeStruct((t, LANES), F32)],
        compiler_params=_cparams(("parallel",)),
        name="out_router",
    )(h, y, wo, gff, wr)


def _row_copy(src_ref, src_row, dst_ref, dst_row, sem):
    return pltpu.make_async_copy(src_ref.at[pl.ds(src_row, 1)], dst_ref.at[pl.ds(dst_row, 1)], sem)


def _dispatch_kernel(row_ref, x_ref, xs_ref, sem):
    tm = x_ref.shape[0]
    base = pl.program_id(0) * tm

    def issue(r, carry):
        for k in range(2):
            _row_copy(x_ref, r, xs_ref, row_ref[2 * (base + r) + k], sem).start()
        return carry

    lax.fori_loop(0, tm, issue, 0, unroll=DMA_UNROLL)
    for k in range(2):
        pltpu.make_async_copy(x_ref, xs_ref.at[pl.ds(0, tm)], sem).wait()


def _dispatch(row_of_pair, xn):
    t, d = xn.shape
    tm = TM_DISPATCH
    grid_spec = pltpu.PrefetchScalarGridSpec(
        num_scalar_prefetch=1,
        grid=(t // tm,),
        in_specs=[pl.BlockSpec((tm, d), lambda i, rows: (i, 0))],
        out_specs=pl.BlockSpec(memory_space=pl.ANY),
        scratch_shapes=[pltpu.SemaphoreType.DMA(())],
    )
    return pl.pallas_call(
        _dispatch_kernel,
        grid_spec=grid_spec,
        out_shape=jax.ShapeDtypeStruct((2 * t, d), xn.dtype),
        compiler_params=_cparams(("arbitrary",)),
        name="moe_dispatch",
    )(row_of_pair, xn)


def _moe_kernel(vt_ref, ve_ref, lo_ref, hi_ref, first_ref, x_ref, wg_ref, wu_ref, wd_ref, o_ref, xb_ref):
    v = pl.program_id(0)
    c = pl.program_id(1)
    tr = x_ref.shape[0]

    @pl.when((c == 0) & (first_ref[v] == 1))
    def _():
        o_ref[...] = jnp.zeros_like(o_ref)

    @pl.when(c == 0)
    def _():
        xb_ref[...] = x_ref[...].astype(BF16)

    @pl.when(lo_ref[v] < hi_ref[v])
    def _():
        x = xb_ref[...]
        g = jnp.dot(x, wg_ref[0].astype(BF16), preferred_element_type=F32)
        u = jnp.dot(x, wu_ref[0].astype(BF16), preferred_element_type=F32)
        a = (_silu(g) * u).astype(BF16)
        y = jnp.dot(a, wd_ref[0].astype(BF16), preferred_element_type=F32)
        rows = vt_ref[v] * tr + lax.broadcasted_iota(jnp.int32, (tr, 1), 0)
        mine = (rows >= lo_ref[v]) & (rows < hi_ref[v])
        o_ref[...] += jnp.where(mine, y, 0.0)


def _moe(visits, xs, wg, wu, wd):
    r = xs.shape[0]
    tr, fc = TR_MOE, FC_MOE
    nc = FF_EXPERT // fc
    n_visits = r // tr + N_EXPERTS - 1

    def chunk(v, c, lo, hi):
        return jnp.where(lo[v] < hi[v], c, nc - 1)

    tile = lambda v, c, vt, ve, lo, hi, first: (vt[v], 0)
    wcol = lambda v, c, vt, ve, lo, hi, first: (ve[v], 0, chunk(v, c, lo, hi))
    wrow = lambda v, c, vt, ve, lo, hi, first: (ve[v], chunk(v, c, lo, hi), 0)
    grid_spec = pltpu.PrefetchScalarGridSpec(
        num_scalar_prefetch=5,
        grid=(n_visits, nc),
        in_specs=[pl.BlockSpec((tr, D_MODEL), tile),
                  pl.BlockSpec((1, D_MODEL, fc), wcol),
                  pl.BlockSpec((1, D_MODEL, fc), wcol),
                  pl.BlockSpec((1, fc, D_MODEL), wrow)],
        out_specs=pl.BlockSpec((tr, D_MODEL), tile),
        scratch_shapes=[pltpu.VMEM((tr, D_MODEL), BF16)],
    )
    return pl.pallas_call(
        _moe_kernel,
        grid_spec=grid_spec,
        out_shape=jax.ShapeDtypeStruct((r, D_MODEL), F32),
        compiler_params=_cparams(("arbitrary", "arbitrary")),
        name="moe_grouped",
    )(*visits, xs, wg, wu, wd)


def _combine_kernel(row_ref, h_ref, gate_ref, ys_ref, o_ref, buf, sem):
    tm = h_ref.shape[0]
    base = pl.program_id(0) * tm

    def issue(r, carry):
        for k in range(2):
            _row_copy(ys_ref, row_ref[2 * (base + r) + k], buf, k * tm + r, sem).start()
        return carry

    lax.fori_loop(0, tm, issue, 0, unroll=DMA_UNROLL)
    pltpu.make_async_copy(ys_ref.at[pl.ds(0, 2 * tm)], buf, sem).wait()
    gate = gate_ref[...]
    o_ref[...] = h_ref[...] + gate[:, 0:1] * buf[0:tm, :] + gate[:, 1:2] * buf[tm:2 * tm, :]


def _combine(row_of_pair, h1, gates, ys):
    t, d = h1.shape
    tm = TM_DISPATCH
    grid_spec = pltpu.PrefetchScalarGridSpec(
        num_scalar_prefetch=1,
        grid=(t // tm,),
        in_specs=[pl.BlockSpec((tm, d), lambda i, rows: (i, 0)),
                  pl.BlockSpec((tm, LANES), lambda i, rows: (i, 0)),
                  pl.BlockSpec(memory_space=pl.ANY)],
        out_specs=pl.BlockSpec((tm, d), lambda i, rows: (i, 0)),
        scratch_shapes=[pltpu.VMEM((2 * tm, d), F32), pltpu.SemaphoreType.DMA(())],
    )
    return pl.pallas_call(
        _combine_kernel,
        grid_spec=grid_spec,
        out_shape=jax.ShapeDtypeStruct((t, d), F32),
        compiler_params=_cparams(("arbitrary",)),
        name="moe_combine",
    )(row_of_pair, h1, gates, ys)


def _route_plan(idx):
    n_pairs = idx.size
    tr = TR_MOE
    n_tiles = n_pairs // tr
    n_visits = n_tiles + N_EXPERTS - 1
    e_flat = idx.reshape(-1)
    onehot = (e_flat[:, None] == jnp.arange(N_EXPERTS, dtype=jnp.int32)[None, :]).astype(jnp.int32)
    csum = jnp.cumsum(onehot, axis=0)
    counts = csum[-1]
    ends = jnp.cumsum(counts)
    starts = ends - counts
    pos = jnp.sum((csum - onehot) * onehot, axis=1)
    row_of_pair = (starts[e_flat] + pos).astype(jnp.int32)
    nv = jnp.where(counts > 0, (ends - 1) // tr - starts // tr + 1, 0)
    vend = jnp.cumsum(nv)
    vid = jnp.arange(n_visits, dtype=jnp.int32)
    live = vid < vend[-1]
    ve = jnp.minimum(jnp.sum((vid[:, None] >= vend[None, :]).astype(jnp.int32), axis=1), N_EXPERTS - 1)
    vt = jnp.where(live, starts[ve] // tr + vid - (vend - nv)[ve], n_tiles - 1)
    lo = jnp.where(live, jnp.maximum(starts[ve], vt * tr), 0)
    hi = jnp.where(live, jnp.minimum(ends[ve], (vt + 1) * tr), 0)
    last_e = jnp.max(jnp.where(counts > 0, jnp.arange(N_EXPERTS, dtype=jnp.int32), 0))
    ve = jnp.where(live, ve, last_e)
    first = (live & (lo == vt * tr)).astype(jnp.int32)
    i32 = lambda a: a.astype(jnp.int32)
    return row_of_pair, (i32(vt), i32(ve), i32(lo), i32(hi), first)


def _rope_tables(positions, dim, theta):
    inv = 1.0 / (theta ** (jnp.arange(0, dim, 2, dtype=F32) / dim))
    ang = positions.astype(F32).reshape(-1, 1) * inv
    return jnp.cos(ang), jnp.sin(ang)


def _tables_a(positions):
    cos, sin = _rope_tables(positions, A_ROPE, A_THETA)
    t = cos.shape[0]
    one, z16, z32 = jnp.ones((t, A_NOPE), F32), jnp.zeros((t, 16), F32), jnp.zeros((t, 32), F32)
    z64 = jnp.zeros((t, 64), F32)
    return (jnp.concatenate([one, cos, cos, z32], 1), jnp.concatenate([z64, -sin, z16, z32], 1),
            jnp.concatenate([z64, z16, sin, z32], 1))


def _tables_c(positions):
    cos, sin = _rope_tables(positions, C_ROPE, ROPE_THETA)
    t = cos.shape[0]
    one, z8, z48 = jnp.ones((t, 48), F32), jnp.zeros((t, 8), F32), jnp.zeros((t, 48), F32)
    two = lambda a: jnp.concatenate([a, a], 1)
    return (two(jnp.concatenate([cos, cos, one], 1)), two(jnp.concatenate([-sin, z8, z48], 1)),
            two(jnp.concatenate([z8, sin, z48], 1)))


def _band_bias(rel_table):
    nh = rel_table.shape[0]
    nk, nq = BAND_KEYS, TQ_BAND
    length = BAND_PROFILE
    n_lo = nk - 1 - B_LEFT_CHUNKS * CHUNK - B_REL_CLIP
    n_hi = length - n_lo - rel_table.shape[1]
    prof = jnp.concatenate([jnp.broadcast_to(rel_table[:, :1], (nh, n_lo)), rel_table,
                            jnp.broadcast_to(rel_table[:, -1:], (nh, n_hi))], axis=1)
    return pl.pallas_call(
        _band_bias_kernel,
        grid=(nh,),
        in_specs=[pl.BlockSpec((None, 1, length), lambda h: (h, 0, 0))],
        out_specs=pl.BlockSpec((None, nk, nq), lambda h: (h // 2, 0, h % 2)),
        out_shape=jax.ShapeDtypeStruct((nh // 2, nk, 2 * nq), F32),
        compiler_params=_cparams(("parallel",)),
        name="band_bias",
    )(prof.astype(F32).reshape(nh, 1, length))


def _band_bias_kernel(p_ref, o_ref):
    nk, nq = o_ref.shape
    length = p_ref.shape[1]
    x = jnp.broadcast_to(p_ref[...] * LOG2E, (8, length))
    sub = lax.broadcasted_iota(jnp.int32, (8, nq), 0)
    qchunk = lax.broadcasted_iota(jnp.int32, (8, nq), 1) // CHUNK

    def body(jb, carry):
        rolled = pltpu.roll(x, jb * 8 + (length - (nk - 1)), 1, stride=1, stride_axis=0)
        cdiff = qchunk + B_LEFT_CHUNKS - (jb * 8 + sub) // CHUNK
        allowed = (cdiff >= 0) & (cdiff <= B_LEFT_CHUNKS)
        o_ref[pl.ds(pl.multiple_of(jb * 8, 8), 8), :] = jnp.where(allowed, rolled[:, :nq], NEG)
        return carry

    lax.fori_loop(0, nk // 8, body, 0)


def _pad_last(a, width):
    return jnp.pad(a, [(0, 0)] * (a.ndim - 1) + [(0, width - a.shape[-1])])


def kernel(x, positions, ev_norm_mix, ev_w_in, ev_g_cq, ev_g_ckv, ev_w_uq, ev_w_ukv, ev_a_qnorm, ev_a_knorm, ev_b_qnorm, ev_b_knorm, ev_b_rel_bias, ev_w_out, ev_norm_ff, ev_ff_gate, ev_ff_up, ev_ff_down, od_norm_mix, od_w_qkv, od_c_qnorm, od_c_knorm, od_lam_q1, od_lam_k1, od_lam_q2, od_lam_k2, od_c_subnorm, od_w_out, od_norm_ff, od_router, od_ex_gate, od_ex_up, od_ex_down):
    batch, seq, d = x.shape
    t = batch * seq
    depth = ev_w_in.shape[0] + od_w_qkv.shape[0]
    h = x.reshape(t, d)
    row = lambda a: a.reshape(1, -1).astype(F32)
    tab_a = _tables_a(positions)
    tab_c = _tables_c(positions)

    for layer in range(depth):
        i = layer // 2
        if layer % 2 == 0:
            w = ev_w_in[i]
            o1, o2, o3 = A_Q_LORA, A_Q_LORA + A_KV_LORA, A_Q_LORA + A_KV_LORA + A_ROPE
            zc = lambda n: jnp.zeros((d, n), F32)
            bw = B_HEADS * B_DH
            win = jnp.concatenate([w[:, :o2], zc(64), w[:, o2:o3], zc(32), w[:, o3:o3 + 2 * bw]], 1).astype(BF16)
            wvbt = w[:, o3 + 2 * bw:].T.astype(BF16)
            wuq = _pad_last(ev_w_uq[i], LANES).reshape(A_Q_LORA, A_HEADS * LANES).astype(BF16)
            wuk = _pad_last(ev_w_ukv[i][:, :, :A_NOPE], LANES).reshape(A_KV_LORA, A_HEADS * LANES).astype(BF16)
            wuvt = ev_w_ukv[i][:, :, A_NOPE:].reshape(A_KV_LORA, A_HEADS * A_V).T.astype(BF16)
            sc_a = (A_NOPE + A_ROPE) ** -0.5 * LOG2E
            gq = jnp.tile(_pad_last(ev_a_qnorm[i], LANES), A_HEADS).reshape(1, -1) * sc_a
            gk = jnp.tile(_pad_last(ev_a_knorm[i][:A_NOPE], LANES), A_HEADS).reshape(1, -1)
            gkr = jnp.concatenate([jnp.zeros((A_NOPE,), F32), ev_a_knorm[i][A_NOPE:],
                                   jnp.zeros((32,), F32)]).reshape(1, -1)
            gbq = jnp.tile(ev_b_qnorm[i], B_HEADS).reshape(1, -1) * (B_DH ** -0.5 * LOG2E)
            gbk = jnp.tile(ev_b_knorm[i], B_HEADS).reshape(1, -1)
            qa, ka, vat, qb, kb, vbt = _proj_even(
                h, row(ev_norm_mix[i]), win, wvbt, row(ev_g_cq[i]), row(ev_g_ckv[i]), wuq, wuk, wuvt,
                gq, gk, gkr, gbq, gbk, *tab_a)
            ya = _attn_causal(_attn_mla_kernel, qa, ka, vat, (), batch=batch, seq=seq,
                              n_blocks=A_HEADS // 2, qk_width=2 * LANES)
            yb = _attn_band(qb, kb, vbt, _band_bias(ev_b_rel_bias[i]), batch=batch, seq=seq)
            wo = ev_w_out[i].astype(BF16)
            na = A_HEADS * A_V
            padf = lambda a: _pad_last(a, FF_DENSE_PAD).astype(BF16)
            wd = jnp.pad(ev_ff_down[i], ((0, FF_DENSE_PAD - FF_DENSE), (0, 0))).astype(BF16)
            h = _out_ffn(h, ya, yb, wo[:na], wo[na:], row(ev_norm_ff[i]),
                         padf(ev_ff_gate[i]), padf(ev_ff_up[i]), wd)
        else:
            lam_init = 0.8 - 0.6 * math.exp(-0.3 * layer)
            gq = jnp.tile(od_c_qnorm[i], 2).reshape(1, -1) * (C_DH ** -0.5 * LOG2E)
            gk = jnp.tile(od_c_knorm[i], 2).reshape(1, -1)
            c_mix = C_HEADS * 2 * C_DH
            wqkv = od_w_qkv[i]
            q, k, v = _proj_odd(h, row(od_norm_mix[i]), wqkv[:, :2 * c_mix].astype(BF16),
                                wqkv[:, 2 * c_mix:].T.astype(BF16), gq, gk, *tab_c)
            extras = (row(od_lam_q1[i]), row(od_lam_k1[i]), row(od_lam_q2[i]), row(od_lam_k2[i]),
                      row(od_c_subnorm[i]))
            yc = _attn_causal(functools.partial(_attn_diff_kernel, lam_init=lam_init), q, k, v, extras,
                              batch=batch, seq=seq, n_blocks=C_HEADS, qk_width=LANES)
            h1, xn, idx, gates = _out_router(h, yc, od_w_out[i].astype(BF16), row(od_norm_ff[i]),
                                             od_router[i].T.astype(F32))
            row_of_pair, visits = _route_plan(idx[:, :2])
            xs = _dispatch(row_of_pair, xn)
            ys = _moe(visits, xs, od_ex_gate[i], od_ex_up[i], od_ex_down[i])
            h = _combine(row_of_pair, h1, gates, ys)
    return h.reshape(batch, seq, d)
```

```python
import functools
import math

import numpy as np
import jax
import jax.numpy as jnp
from jax import lax
from jax.experimental import pallas as pl
from jax.experimental.pallas import tpu as pltpu

F32 = jnp.float32
BF16 = jnp.bfloat16

D_MODEL = 1024
CHUNK = 64
EPS = 1e-6
LOG2E = 1.4426950408889634
NEG = -1e30

A_HEADS = 8
A_NOPE = 64
A_ROPE = 32
A_V = 64
A_Q_LORA = 256
A_KV_LORA = 128
A_THETA = 10000.0
B_HEADS = 8
B_DH = 64
B_LEFT_CHUNKS = 8
B_REL_CLIP = 128
C_HEADS = 8
C_DH = 64
C_ROPE = C_DH // 4
ROPE_THETA = 500000.0
FF_DENSE = 2752
N_EXPERTS = 8
FF_EXPERT = 3584

LANES = 128
FF_DENSE_PAD = 2816
VMEM_LIMIT = 56 * 1024 * 1024

TM_PROJ = 512
TQ_CAUSAL = 256
TQ_BAND = 256
BAND_KEYS = TQ_BAND + B_LEFT_CHUNKS * CHUNK
BAND_PROFILE = 1024
TM_DISPATCH = 256
DMA_UNROLL = 8
TR_MOE = 1024
FC_MOE = 512


def _cparams(sem):
    return pltpu.CompilerParams(dimension_semantics=sem, vmem_limit_bytes=VMEM_LIMIT)


def _const_spec(shape):
    nd = len(shape)
    return pl.BlockSpec(shape, lambda *_: (0,) * nd, pipeline_mode=pl.Buffered(1))


def _rms_scale(x):
    return lax.rsqrt(jnp.mean(x * x, axis=-1, keepdims=True) + EPS)


def _lane_iota():
    return lax.broadcasted_iota(jnp.int32, (1, LANES), 1)


def _dot_nt(a, b):
    return lax.dot_general(a, b, (((1,), (1,)), ((), ())), preferred_element_type=F32)


def _rope_tables(cs, half, starts):
    tm = cs.shape[0]
    base = jnp.concatenate([cs, jnp.zeros((tm, LANES - 2 * half), F32)], axis=1)
    shifted = lambda s: base if s % LANES == 0 else pltpu.roll(base, s % LANES, 1)
    lane = _lane_iota()
    cos_t = jnp.ones((tm, LANES), F32)
    sin_a = jnp.zeros((tm, LANES), F32)
    sin_b = jnp.zeros((tm, LANES), F32)
    for s0 in starts:
        first = (lane >= s0) & (lane < s0 + half)
        second = (lane >= s0 + half) & (lane < s0 + 2 * half)
        at_s0 = shifted(s0)
        cos_t = jnp.where(first, at_s0, jnp.where(second, shifted(s0 + half), cos_t))
        sin_a = jnp.where(first, -shifted(s0 - half), sin_a)
        sin_b = jnp.where(second, at_s0, sin_b)
    return cos_t, sin_a, sin_b


def _gained_rope(tables, half, g3_ref, tab_ref):
    for j in range(3):
        tab_ref[j] = tables[j] * g3_ref[j:j + 1, :]

    def rope(x):
        return (x * tab_ref[0] + pltpu.roll(x, LANES - half, 1) * tab_ref[1]
                + pltpu.roll(x, half, 1) * tab_ref[2])

    return rope


def _proj_even_kernel(h_ref, gmix_ref, win_ref, wvbt_ref, gcq_ref, gckv_ref, wuq_ref, wuk_ref, wuvt_ref,
                      gq3_ref, gk_ref, gkr3_ref, gbq_ref, gbk_ref, cs_ref,
                      qa_ref, ka_ref, vat_ref, qb_ref, kb_ref, vbt_ref, tabq_ref, tabk_ref):
    x = h_ref[...]
    hn = (x * _rms_scale(x) * gmix_ref[...]).astype(BF16)
    u = jnp.dot(hn, win_ref[...], preferred_element_type=F32)
    vbt_ref[...] = _dot_nt(wvbt_ref[...], hn).astype(BF16)
    lane = _lane_iota()
    lo = lane < 64
    tables = _rope_tables(cs_ref[...], A_ROPE // 2, (A_NOPE,))
    rope_q = _gained_rope(tables, A_ROPE // 2, gq3_ref, tabq_ref)
    rope_k = _gained_rope(tables, A_ROPE // 2, gkr3_ref, tabk_ref)

    cq = u[:, 0:A_Q_LORA]
    cqn = (cq * _rms_scale(cq) * gcq_ref[...]).astype(BF16)
    q = jnp.dot(cqn, wuq_ref[...], preferred_element_type=F32)
    ckv = u[:, A_Q_LORA:A_Q_LORA + A_KV_LORA]
    ckvn = (ckv * _rms_scale(ckv) * gckv_ref[...]).astype(BF16)
    kn = jnp.dot(ckvn, wuk_ref[...], preferred_element_type=F32)
    vat_ref[...] = _dot_nt(wuvt_ref[...], ckvn).astype(BF16)

    kr = u[:, 384:512]
    kr_s = lax.rsqrt(jnp.sum(kr * kr, axis=-1, keepdims=True) * (1.0 / A_ROPE) + EPS)
    krot = rope_k(kr) * kr_s

    for hd in range(A_HEADS):
        sl = slice(hd * LANES, (hd + 1) * LANES)
        blk = q[:, sl]
        sq = blk * blk
        tot = jnp.sum(sq, axis=-1, keepdims=True)
        ssn = jnp.sum(jnp.where(lo, sq, 0.0), axis=-1, keepdims=True)
        sc = jnp.where(lo, lax.rsqrt(ssn * (1.0 / A_NOPE) + EPS),
                       lax.rsqrt((tot - ssn) * (1.0 / A_ROPE) + EPS))
        qa_ref[:, sl] = (rope_q(blk) * sc).astype(BF16)
        kb_ = kn[:, sl]
        ks = lax.rsqrt(jnp.sum(kb_ * kb_, axis=-1, keepdims=True) * (1.0 / A_NOPE) + EPS)
        ka_ref[:, sl] = (kb_ * ks * gk_ref[:, sl] + krot).astype(BF16)

    def pair_norm(blk, g):
        sq = blk * blk
        tot = jnp.sum(sq, axis=-1, keepdims=True)
        s_lo = jnp.sum(jnp.where(lo, sq, 0.0), axis=-1, keepdims=True)
        sc = jnp.where(lo, lax.rsqrt(s_lo * (1.0 / B_DH) + EPS),
                       lax.rsqrt((tot - s_lo) * (1.0 / B_DH) + EPS))
        return (blk * sc * g).astype(BF16)

    for p in range(B_HEADS // 2):
        sl = slice(p * LANES, (p + 1) * LANES)
        qb_ref[:, sl] = pair_norm(u[:, 512 + p * LANES:512 + (p + 1) * LANES], gbq_ref[:, sl])
        kb_ref[:, sl] = pair_norm(u[:, 1024 + p * LANES:1024 + (p + 1) * LANES], gbk_ref[:, sl])


def _proj_even(h, gmix, win, wvbt, gcq, gckv, wuq, wuk, wuvt, gq3, gk, gkr3, gbq, gbk, cs):
    t = h.shape[0]
    tm = TM_PROJ
    row = lambda w: pl.BlockSpec((tm, w), lambda i: (i, 0))
    col = lambda w: pl.BlockSpec((w, tm), lambda i: (0, i))
    rshape = lambda w: jax.ShapeDtypeStruct((t, w), BF16)
    cshape = lambda w: jax.ShapeDtypeStruct((w, t), BF16)
    consts = (gmix, win, wvbt, gcq, gckv, wuq, wuk, wuvt, gq3, gk, gkr3, gbq, gbk)
    return pl.pallas_call(
        _proj_even_kernel,
        grid=(t // tm,),
        in_specs=[row(D_MODEL)] + [_const_spec(c.shape) for c in consts] + [row(cs.shape[1])],
        out_specs=[row(1024), row(1024), col(512), row(512), row(512), col(512)],
        out_shape=[rshape(1024), rshape(1024), cshape(512), rshape(512), rshape(512), cshape(512)],
        scratch_shapes=[pltpu.VMEM((3, tm, LANES), F32)] * 2,
        compiler_params=_cparams(("parallel",)),
        name="proj_even",
    )(h, *consts, cs)


def _causal_scores(q_ref, k_ref, s_ref, c, *, split):
    tq = TQ_CAUSAL
    q = q_ref[c * tq:(c + 1) * tq, :]
    zero = jnp.zeros_like(q[:, :LANES])
    if split:
        q2 = jnp.concatenate([jnp.concatenate([q[:, :LANES], zero], axis=1),
                              jnp.concatenate([zero, q[:, LANES:]], axis=1)], axis=0)
    else:
        lo = _lane_iota() < 64
        q2 = jnp.concatenate([jnp.where(lo, q, zero), jnp.where(lo, zero, q)], axis=0)
    kchunk = lax.broadcasted_iota(jnp.int32, (tq, 2 * tq), 0) // CHUNK
    qchunk = (lax.broadcasted_iota(jnp.int32, (tq, 2 * tq), 1) % tq) // CHUNK
    n_past = c * tq
    s_d = jnp.where(kchunk <= qchunk, _dot_nt(k_ref[n_past:n_past + tq, :], q2), NEG)
    s_ref[n_past:n_past + tq, :] = s_d
    mx = jnp.max(s_d, axis=0, keepdims=True)
    if c:
        s_p = _dot_nt(k_ref[0:n_past, :], q2)
        s_ref[0:n_past, :] = s_p
        mx = jnp.maximum(mx, jnp.max(s_p, axis=0, keepdims=True))
    return mx


def _causal_values(vt_ref, s_ref, c, mx):
    n_keys = (c + 1) * TQ_CAUSAL
    p = jnp.exp2(s_ref[0:n_keys, :] - mx)
    o = jnp.dot(vt_ref[:, 0:n_keys], p.astype(BF16), preferred_element_type=F32)
    return o * (1.0 / jnp.sum(p, axis=0, keepdims=True))


def _causal_tiles(q_ref, k_ref, vt_ref, s_buf, *, split):
    tq = TQ_CAUSAL
    n = q_ref.shape[0] // tq
    mx = _causal_scores(q_ref, k_ref, s_buf.at[0], 0, split=split)
    for c in range(n):
        mx_next = None
        if c + 1 < n:
            mx_next = _causal_scores(q_ref, k_ref, s_buf.at[(c + 1) % 2], c + 1, split=split)
        o = _causal_values(vt_ref, s_buf.at[c % 2], c, mx)
        yield c, o[:, :tq], o[:, tq:]
        mx = mx_next


def _row_iota(n):
    return lax.broadcasted_iota(jnp.int32, (LANES, n), 0)


def _attn_mla_kernel(q_ref, k_ref, vt_ref, o_ref, s_buf):
    tq = TQ_CAUSAL
    for c, o0, o1 in _causal_tiles(q_ref, k_ref, vt_ref, s_buf, split=True):
        o_ref[c * tq:(c + 1) * tq, :] = jnp.where(_row_iota(tq) < 64, o0, o1).T.astype(o_ref.dtype)


def _attn_diff_kernel(q_ref, k_ref, vt_ref, lq1_ref, lk1_ref, lq2_ref, lk2_ref, gsub_ref, o_ref, s_buf,
                      *, lam_init):
    tq = TQ_CAUSAL
    lam = (jnp.exp(jnp.sum(lq1_ref[...] * lk1_ref[...], axis=-1, keepdims=True))
           - jnp.exp(jnp.sum(lq2_ref[...] * lk2_ref[...], axis=-1, keepdims=True)) + lam_init)
    for c, o0, o1 in _causal_tiles(q_ref, k_ref, vt_ref, s_buf, split=False):
        o = (o0 - lam * o1).T
        o = o * _rms_scale(o) * gsub_ref[...] * (1.0 - lam_init)
        o_ref[c * tq:(c + 1) * tq, :] = o.astype(o_ref.dtype)


def _attn_causal(kernel, q, k, vt, extras, *, batch, seq, n_blocks, qk_width):
    in_specs = [pl.BlockSpec((seq, qk_width), lambda b, p: (b, p)),
                pl.BlockSpec((seq, qk_width), lambda b, p: (b, p)),
                pl.BlockSpec((LANES, seq), lambda b, p: (p, b))]
    in_specs += [_const_spec(e.shape) for e in extras]
    return pl.pallas_call(
        kernel,
        grid=(batch, n_blocks),
        in_specs=in_specs,
        out_specs=pl.BlockSpec((seq, LANES), lambda b, p: (b, p)),
        out_shape=jax.ShapeDtypeStruct((batch * seq, n_blocks * LANES), BF16),
        scratch_shapes=[pltpu.VMEM((2, seq, 2 * TQ_CAUSAL), F32)],
        compiler_params=_cparams(("parallel", "parallel")),
        name=kernel.func.__name__ if isinstance(kernel, functools.partial) else kernel.__name__,
    )(q, k, vt, *extras)


def _attn_band_kernel(q_ref, k_ref, vt_ref, bias_ref, o_ref, s_buf):
    tq = TQ_BAND
    n = q_ref.shape[0] // tq
    lo = _lane_iota() < 64

    def window(c):
        end = (c + 1) * tq
        return max(0, end - BAND_KEYS), end

    def scores(c):
        start, end = window(c)
        q = q_ref[c * tq:(c + 1) * tq, :]
        zero = jnp.zeros_like(q)
        q2 = jnp.concatenate([jnp.where(lo, q, zero), jnp.where(lo, zero, q)], axis=0)
        s = _dot_nt(k_ref[start:end, :], q2) + bias_ref[BAND_KEYS - (end - start):, :]
        s_buf[c % 2, 0:end - start, :] = s
        return jnp.max(s, axis=0, keepdims=True)

    mx = scores(0)
    for c in range(n):
        mx_next = scores(c + 1) if c + 1 < n else None
        start, end = window(c)
        p = jnp.exp2(s_buf[c % 2, 0:end - start, :] - mx)
        o = jnp.dot(vt_ref[:, start:end], p.astype(BF16), preferred_element_type=F32)
        o = o * (1.0 / jnp.sum(p, axis=0, keepdims=True))
        o_ref[c * tq:(c + 1) * tq, :] = jnp.where(_row_iota(tq) < 64, o[:, :tq], o[:, tq:]).T.astype(o_ref.dtype)
        mx = mx_next


def _attn_band(q, k, vt, bias, *, batch, seq):
    npair = B_HEADS // 2
    return pl.pallas_call(
        _attn_band_kernel,
        grid=(batch, npair),
        in_specs=[pl.BlockSpec((seq, LANES), lambda b, p: (b, p)),
                  pl.BlockSpec((seq, LANES), lambda b, p: (b, p)),
                  pl.BlockSpec((LANES, seq), lambda b, p: (p, b)),
                  pl.BlockSpec((None, BAND_KEYS, 2 * TQ_BAND), lambda b, p: (p, 0, 0))],
        out_specs=pl.BlockSpec((seq, LANES), lambda b, p: (b, p)),
        out_shape=jax.ShapeDtypeStruct((batch * seq, npair * LANES), BF16),
        scratch_shapes=[pltpu.VMEM((2, BAND_KEYS, 2 * TQ_BAND), F32)],
        compiler_params=_cparams(("parallel", "parallel")),
        name="attn_band",
    )(q, k, vt, bias)


def _silu(g):
    return g / (1.0 + jnp.exp(-g))


def _out_ffn_kernel(h_ref, ya_ref, yb_ref, woa_ref, wob_ref, gff_ref, wg_ref, wu_ref, wd_ref, o_ref):
    h1 = (h_ref[...] + jnp.dot(ya_ref[...], woa_ref[...], preferred_element_type=F32)
          + jnp.dot(yb_ref[...], wob_ref[...], preferred_element_type=F32))
    xn = (h1 * _rms_scale(h1) * gff_ref[...]).astype(BF16)
    g = jnp.dot(xn, wg_ref[...], preferred_element_type=F32)
    u = jnp.dot(xn, wu_ref[...], preferred_element_type=F32)
    a = (_silu(g) * u).astype(BF16)
    o_ref[...] = h1 + jnp.dot(a, wd_ref[...], preferred_element_type=F32)


def _out_ffn(h, ya, yb, woa, wob, gff, wg, wu, wd):
    t = h.shape[0]
    tm = TM_PROJ
    row = lambda w: pl.BlockSpec((tm, w), lambda i: (i, 0))
    return pl.pallas_call(
        _out_ffn_kernel,
        grid=(t // tm,),
        in_specs=[row(D_MODEL), row(ya.shape[1]), row(yb.shape[1])]
        + [_const_spec(w.shape) for w in (woa, wob, gff, wg, wu, wd)],
        out_specs=row(D_MODEL),
        out_shape=jax.ShapeDtypeStruct((t, D_MODEL), F32),
        compiler_params=_cparams(("parallel",)),
        name="out_ffn",
    )(h, ya, yb, woa, wob, gff, wg, wu, wd)


def _proj_odd_kernel(h_ref, gmix_ref, w_ref, wvt_ref, gq3_ref, gk3_ref, cs_ref, q_ref, k_ref, vt_ref,
                     tabq_ref, tabk_ref):
    x = h_ref[...]
    hn = (x * _rms_scale(x) * gmix_ref[...]).astype(BF16)
    u = jnp.dot(hn, w_ref[...], preferred_element_type=F32)
    vt_ref[...] = _dot_nt(wvt_ref[...], hn).astype(BF16)
    lo = _lane_iota() < 64
    tables = _rope_tables(cs_ref[...], C_ROPE // 2, (0, C_DH))
    rope_q = _gained_rope(tables, C_ROPE // 2, gq3_ref, tabq_ref)
    rope_k = _gained_rope(tables, C_ROPE // 2, gk3_ref, tabk_ref)
    width = C_HEADS * 2 * C_DH

    def norm_rope(blk, rope):
        sq = blk * blk
        tot = jnp.sum(sq, axis=-1, keepdims=True)
        s_lo = jnp.sum(jnp.where(lo, sq, 0.0), axis=-1, keepdims=True)
        sc = jnp.where(lo, lax.rsqrt(s_lo * (1.0 / C_DH) + EPS),
                       lax.rsqrt((tot - s_lo) * (1.0 / C_DH) + EPS))
        return (rope(blk) * sc).astype(BF16)

    for hd in range(C_HEADS):
        sl = slice(hd * LANES, (hd + 1) * LANES)
        q_ref[:, sl] = norm_rope(u[:, hd * LANES:(hd + 1) * LANES], rope_q)
        k_ref[:, sl] = norm_rope(u[:, width + hd * LANES:width + (hd + 1) * LANES], rope_k)


def _proj_odd(h, gmix, w, wvt, gq3, gk3, cs):
    t = h.shape[0]
    tm = TM_PROJ
    row = lambda wd: pl.BlockSpec((tm, wd), lambda i: (i, 0))
    width = C_HEADS * 2 * C_DH
    consts = (gmix, w, wvt, gq3, gk3)
    return pl.pallas_call(
        _proj_odd_kernel,
        grid=(t // tm,),
        in_specs=[row(D_MODEL)] + [_const_spec(c.shape) for c in consts] + [row(cs.shape[1])],
        out_specs=[row(width), row(width), pl.BlockSpec((width, tm), lambda i: (0, i))],
        out_shape=[jax.ShapeDtypeStruct((t, width), BF16), jax.ShapeDtypeStruct((t, width), BF16),
                   jax.ShapeDtypeStruct((width, t), BF16)],
        scratch_shapes=[pltpu.VMEM((3, tm, LANES), F32)] * 2,
        compiler_params=_cparams(("parallel",)),
        name="proj_odd",
    )(h, *consts, cs)


def _out_router_kernel(h_ref, y_ref, wo_ref, gff_ref, wr_ref, h1_ref, xn_ref, idx_ref, gate_ref):
    h1 = h_ref[...] + jnp.dot(y_ref[...], wo_ref[...], preferred_element_type=F32)
    h1_ref[...] = h1
    xn = h1 * _rms_scale(h1) * gff_ref[...]
    xn_ref[...] = xn
    logits = [jnp.sum(xn * wr_ref[e:e + 1, :], axis=-1, keepdims=True) for e in range(N_EXPERTS)]
    m1 = functools.reduce(jnp.maximum, logits)
    idx1 = functools.reduce(jnp.minimum,
                            [jnp.where(logits[e] == m1, e, N_EXPERTS) for e in range(N_EXPERTS)])
    rest = [jnp.where(idx1 == e, -jnp.inf, logits[e]) for e in range(N_EXPERTS)]
    m2 = functools.reduce(jnp.maximum, rest)
    idx2 = functools.reduce(jnp.minimum,
                            [jnp.where(rest[e] == m2, e, N_EXPERTS) for e in range(N_EXPERTS)])
    e2 = jnp.exp(m2 - m1)
    g1 = 1.0 / (1.0 + e2)
    g2 = e2 * g1
    lane = _lane_iota()
    idx_ref[...] = jnp.where(lane == 0, idx1, jnp.where(lane == 1, idx2, 0))
    gate_ref[...] = jnp.where(lane == 0, g1, jnp.where(lane == 1, g2, 0.0))


def _out_router(h, y, wo, gff, wr):
    t = h.shape[0]
    tm = TM_PROJ
    row = lambda w: pl.BlockSpec((tm, w), lambda i: (i, 0))
    return pl.pallas_call(
        _out_router_kernel,
        grid=(t // tm,),
        in_specs=[row(D_MODEL), row(y.shape[1]), _const_spec(wo.shape), _const_spec(gff.shape),
                  _const_spec(wr.shape)],
        out_specs=[row(D_MODEL), row(D_MODEL), row(LANES), row(LANES)],
        out_shape=[jax.ShapeDtypeStruct((t, D_MODEL), F32), jax.ShapeDtypeStruct((t, D_MODEL), F32),
                   jax.ShapeDtypeStruct((t, LANES), jnp.int32), jax.ShapeDtypeStruct((t, LANES), F32)],
        compiler_params=_cparams(("parallel",)),
        name="out_router",
    )(h, y, wo, gff, wr)


def _row_copy(src_ref, src_row, dst_ref, dst_row, sem):
    return pltpu.make_async_copy(src_ref.at[pl.ds(src_row, 1)], dst_ref.at[pl.ds(dst_row, 1)], sem)


def _dispatch_kernel(row_ref, x_ref, xs_ref, sem):
    tm = x_ref.shape[0]
    base = pl.program_id(0) * tm

    def issue(r, carry):
        for k in range(2):
            _row_copy(x_ref, r, xs_ref, row_ref[2 * (base + r) + k], sem).start()
        return carry

    lax.fori_loop(0, tm, issue, 0, unroll=DMA_UNROLL)
    for k in range(2):
        pltpu.make_async_copy(x_ref, xs_ref.at[pl.ds(0, tm)], sem).wait()


def _dispatch(row_of_pair, xn):
    t, d = xn.shape
    tm = TM_DISPATCH
    grid_spec = pltpu.PrefetchScalarGridSpec(
        num_scalar_prefetch=1,
        grid=(t // tm,),
        in_specs=[pl.BlockSpec((tm, d), lambda i, rows: (i, 0))],
        out_specs=pl.BlockSpec(memory_space=pl.ANY),
        scratch_shapes=[pltpu.SemaphoreType.DMA(())],
    )
    return pl.pallas_call(
        _dispatch_kernel,
        grid_spec=grid_spec,
        out_shape=jax.ShapeDtypeStruct((2 * t, d), xn.dtype),
        compiler_params=_cparams(("arbitrary",)),
        name="moe_dispatch",
    )(row_of_pair, xn)


def _moe_kernel(vt_ref, ve_ref, lo_ref, hi_ref, first_ref, x_ref, wg_ref, wu_ref, wd_ref, o_ref, xb_ref):
    v = pl.program_id(0)
    c = pl.program_id(1)
    tr = x_ref.shape[0]

    @pl.when((c == 0) & (first_ref[v] == 1))
    def _():
        o_ref[...] = jnp.zeros_like(o_ref)

    @pl.when(c == 0)
    def _():
        xb_ref[...] = x_ref[...].astype(BF16)

    @pl.when(lo_ref[v] < hi_ref[v])
    def _():
        x = xb_ref[...]
        g = jnp.dot(x, wg_ref[0].astype(BF16), preferred_element_type=F32)
        u = jnp.dot(x, wu_ref[0].astype(BF16), preferred_element_type=F32)
        a = (_silu(g) * u).astype(BF16)
        y = jnp.dot(a, wd_ref[0].astype(BF16), preferred_element_type=F32)
        rows = vt_ref[v] * tr + lax.broadcasted_iota(jnp.int32, (tr, 1), 0)
        mine = (rows >= lo_ref[v]) & (rows < hi_ref[v])
        o_ref[...] += jnp.where(mine, y, 0.0)


def _moe(visits, xs, wg, wu, wd):
    r = xs.shape[0]
    tr, fc = TR_MOE, FC_MOE
    nc = FF_EXPERT // fc
    n_visits = r // tr + N_EXPERTS - 1

    def chunk(v, c, lo, hi):
        return jnp.where(lo[v] < hi[v], c, nc - 1)

    tile = lambda v, c, vt, ve, lo, hi, first: (vt[v], 0)
    wcol = lambda v, c, vt, ve, lo, hi, first: (ve[v], 0, chunk(v, c, lo, hi))
    wrow = lambda v, c, vt, ve, lo, hi, first: (ve[v], chunk(v, c, lo, hi), 0)
    grid_spec = pltpu.PrefetchScalarGridSpec(
        num_scalar_prefetch=5,
        grid=(n_visits, nc),
        in_specs=[pl.BlockSpec((tr, D_MODEL), tile),
                  pl.BlockSpec((1, D_MODEL, fc), wcol),
                  pl.BlockSpec((1, D_MODEL, fc), wcol),
                  pl.BlockSpec((1, fc, D_MODEL), wrow)],
        out_specs=pl.BlockSpec((tr, D_MODEL), tile),
        scratch_shapes=[pltpu.VMEM((tr, D_MODEL), BF16)],
    )
    return pl.pallas_call(
        _moe_kernel,
        grid_spec=grid_spec,
        out_shape=jax.ShapeDtypeStruct((r, D_MODEL), F32),
        compiler_params=_cparams(("arbitrary", "arbitrary")),
        name="moe_grouped",
    )(*visits, xs, wg, wu, wd)


def _combine_kernel(row_ref, h_ref, gate_ref, ys_ref, o_ref, buf, sem):
    tm = h_ref.shape[0]
    base = pl.program_id(0) * tm

    def issue(r, carry):
        for k in range(2):
            _row_copy(ys_ref, row_ref[2 * (base + r) + k], buf, k * tm + r, sem).start()
        return carry

    lax.fori_loop(0, tm, issue, 0, unroll=DMA_UNROLL)
    pltpu.make_async_copy(ys_ref.at[pl.ds(0, 2 * tm)], buf, sem).wait()
    gate = gate_ref[...]
    o_ref[...] = h_ref[...] + gate[:, 0:1] * buf[0:tm, :] + gate[:, 1:2] * buf[tm:2 * tm, :]


def _combine(row_of_pair, h1, gates, ys):
    t, d = h1.shape
    tm = TM_DISPATCH
    grid_spec = pltpu.PrefetchScalarGridSpec(
        num_scalar_prefetch=1,
        grid=(t // tm,),
        in_specs=[pl.BlockSpec((tm, d), lambda i, rows: (i, 0)),
                  pl.BlockSpec((tm, LANES), lambda i, rows: (i, 0)),
                  pl.BlockSpec(memory_space=pl.ANY)],
        out_specs=pl.BlockSpec((tm, d), lambda i, rows: (i, 0)),
        scratch_shapes=[pltpu.VMEM((2 * tm, d), F32), pltpu.SemaphoreType.DMA(())],
    )
    return pl.pallas_call(
        _combine_kernel,
        grid_spec=grid_spec,
        out_shape=jax.ShapeDtypeStruct((t, d), F32),
        compiler_params=_cparams(("arbitrary",)),
        name="moe_combine",
    )(row_of_pair, h1, gates, ys)


def _route_plan(idx):
    n_pairs = idx.size
    tr = TR_MOE
    n_tiles = n_pairs // tr
    n_visits = n_tiles + N_EXPERTS - 1
    e_flat = idx.reshape(-1)
    onehot = (e_flat[:, None] == jnp.arange(N_EXPERTS, dtype=jnp.int32)[None, :]).astype(jnp.int32)
    csum = jnp.cumsum(onehot, axis=0)
    counts = csum[-1]
    ends = jnp.cumsum(counts)
    starts = ends - counts
    pos = jnp.sum((csum - onehot) * onehot, axis=1)
    row_of_pair = (starts[e_flat] + pos).astype(jnp.int32)
    nv = jnp.where(counts > 0, (ends - 1) // tr - starts // tr + 1, 0)
    vend = jnp.cumsum(nv)
    vid = jnp.arange(n_visits, dtype=jnp.int32)
    live = vid < vend[-1]
    ve = jnp.minimum(jnp.sum((vid[:, None] >= vend[None, :]).astype(jnp.int32), axis=1), N_EXPERTS - 1)
    vt = jnp.where(live, starts[ve] // tr + vid - (vend - nv)[ve], n_tiles - 1)
    lo = jnp.where(live, jnp.maximum(starts[ve], vt * tr), 0)
    hi = jnp.where(live, jnp.minimum(ends[ve], (vt + 1) * tr), 0)
    last_e = jnp.max(jnp.where(counts > 0, jnp.arange(N_EXPERTS, dtype=jnp.int32), 0))
    ve = jnp.where(live, ve, last_e)
    first = (live & (lo == vt * tr)).astype(jnp.int32)
    i32 = lambda a: a.astype(jnp.int32)
    return row_of_pair, (i32(vt), i32(ve), i32(lo), i32(hi), first)


def _cos_sin(positions, dim, theta):
    inv = 1.0 / (theta ** (jnp.arange(0, dim, 2, dtype=F32) / dim))
    ang = positions.astype(F32).reshape(-1, 1) * inv
    return jnp.concatenate([jnp.cos(ang), jnp.sin(ang)], axis=1)


def _gain3(g, half):
    return jnp.stack([g, jnp.roll(g, -half), jnp.roll(g, half)]).astype(F32)


def _band_bias(rel_table):
    nh = rel_table.shape[0]
    nk, nq = BAND_KEYS, TQ_BAND
    length = BAND_PROFILE
    n_lo = nk - 1 - B_LEFT_CHUNKS * CHUNK - B_REL_CLIP
    n_hi = length - n_lo - rel_table.shape[1]
    prof = jnp.concatenate([jnp.broadcast_to(rel_table[:, :1], (nh, n_lo)), rel_table,
                            jnp.broadcast_to(rel_table[:, -1:], (nh, n_hi))], axis=1)
    return pl.pallas_call(
        _band_bias_kernel,
        grid=(nh,),
        in_specs=[pl.BlockSpec((None, 1, length), lambda h: (h, 0, 0))],
        out_specs=pl.BlockSpec((None, nk, nq), lambda h: (h // 2, 0, h % 2)),
        out_shape=jax.ShapeDtypeStruct((nh // 2, nk, 2 * nq), F32),
        compiler_params=_cparams(("parallel",)),
        name="band_bias",
    )(prof.astype(F32).reshape(nh, 1, length))


def _band_bias_kernel(p_ref, o_ref):
    nk, nq = o_ref.shape
    length = p_ref.shape[1]
    x = jnp.broadcast_to(p_ref[...] * LOG2E, (8, length))
    sub = lax.broadcasted_iota(jnp.int32, (8, nq), 0)
    qchunk = lax.broadcasted_iota(jnp.int32, (8, nq), 1) // CHUNK

    def body(jb, carry):
        rolled = pltpu.roll(x, jb * 8 + (length - (nk - 1)), 1, stride=1, stride_axis=0)
        cdiff = qchunk + B_LEFT_CHUNKS - (jb * 8 + sub) // CHUNK
        allowed = (cdiff >= 0) & (cdiff <= B_LEFT_CHUNKS)
        o_ref[pl.ds(pl.multiple_of(jb * 8, 8), 8), :] = jnp.where(allowed, rolled[:, :nq], NEG)
        return carry

    lax.fori_loop(0, nk // 8, body, 0, unroll=8)


def _pad_last(a, width):
    return jnp.pad(a, [(0, 0)] * (a.ndim - 1) + [(0, width - a.shape[-1])])


def kernel(x, positions, ev_norm_mix, ev_w_in, ev_g_cq, ev_g_ckv, ev_w_uq, ev_w_ukv, ev_a_qnorm, ev_a_knorm, ev_b_qnorm, ev_b_knorm, ev_b_rel_bias, ev_w_out, ev_norm_ff, ev_ff_gate, ev_ff_up, ev_ff_down, od_norm_mix, od_w_qkv, od_c_qnorm, od_c_knorm, od_lam_q1, od_lam_k1, od_lam_q2, od_lam_k2, od_c_subnorm, od_w_out, od_norm_ff, od_router, od_ex_gate, od_ex_up, od_ex_down):
    batch, seq, d = x.shape
    t = batch * seq
    depth = ev_w_in.shape[0] + od_w_qkv.shape[0]
    h = x.reshape(t, d)
    row = lambda a: a.reshape(1, -1).astype(F32)
    cs_a = _cos_sin(positions, A_ROPE, A_THETA)
    cs_c = _cos_sin(positions, C_ROPE, ROPE_THETA)

    for layer in range(depth):
        i = layer // 2
        if layer % 2 == 0:
            w = ev_w_in[i]
            o1, o2, o3 = A_Q_LORA, A_Q_LORA + A_KV_LORA, A_Q_LORA + A_KV_LORA + A_ROPE
            zc = lambda n: jnp.zeros((d, n), F32)
            bw = B_HEADS * B_DH
            win = jnp.concatenate([w[:, :o2], zc(64), w[:, o2:o3], zc(32), w[:, o3:o3 + 2 * bw]], 1).astype(BF16)
            wvbt = w[:, o3 + 2 * bw:].T.astype(BF16)
            wuq = _pad_last(ev_w_uq[i], LANES).reshape(A_Q_LORA, A_HEADS * LANES).astype(BF16)
            wuk = _pad_last(ev_w_ukv[i][:, :, :A_NOPE], LANES).reshape(A_KV_LORA, A_HEADS * LANES).astype(BF16)
            wuvt = ev_w_ukv[i][:, :, A_NOPE:].reshape(A_KV_LORA, A_HEADS * A_V).T.astype(BF16)
            sc_a = (A_NOPE + A_ROPE) ** -0.5 * LOG2E
            gq3 = _gain3(_pad_last(ev_a_qnorm[i], LANES) * sc_a, A_ROPE // 2)
            gk = jnp.tile(_pad_last(ev_a_knorm[i][:A_NOPE], LANES), A_HEADS).reshape(1, -1)
            gkr3 = _gain3(jnp.concatenate([jnp.zeros((A_NOPE,), F32), ev_a_knorm[i][A_NOPE:],
                                           jnp.zeros((32,), F32)]), A_ROPE // 2)
            gbq = jnp.tile(ev_b_qnorm[i], B_HEADS).reshape(1, -1) * (B_DH ** -0.5 * LOG2E)
            gbk = jnp.tile(ev_b_knorm[i], B_HEADS).reshape(1, -1)
            qa, ka, vat, qb, kb, vbt = _proj_even(
                h, row(ev_norm_mix[i]), win, wvbt, row(ev_g_cq[i]), row(ev_g_ckv[i]), wuq, wuk, wuvt,
                gq3, gk, gkr3, gbq, gbk, cs_a)
            ya = _attn_causal(_attn_mla_kernel, qa, ka, vat, (), batch=batch, seq=seq,
                              n_blocks=A_HEADS // 2, qk_width=2 * LANES)
            yb = _attn_band(qb, kb, vbt, _band_bias(ev_b_rel_bias[i]), batch=batch, seq=seq)
            wo = ev_w_out[i].astype(BF16)
            na = A_HEADS * A_V
            padf = lambda a: _pad_last(a, FF_DENSE_PAD).astype(BF16)
            wd = jnp.pad(ev_ff_down[i], ((0, FF_DENSE_PAD - FF_DENSE), (0, 0))).astype(BF16)
            h = _out_ffn(h, ya, yb, wo[:na], wo[na:], row(ev_norm_ff[i]),
                         padf(ev_ff_gate[i]), padf(ev_ff_up[i]), wd)
        else:
            lam_init = 0.8 - 0.6 * math.exp(-0.3 * layer)
            gq3 = _gain3(jnp.tile(od_c_qnorm[i], 2) * (C_DH ** -0.5 * LOG2E), C_ROPE // 2)
            gk3 = _gain3(jnp.tile(od_c_knorm[i], 2), C_ROPE // 2)
            c_mix = C_HEADS * 2 * C_DH
            wqkv = od_w_qkv[i]
            q, k, v = _proj_odd(h, row(od_norm_mix[i]), wqkv[:, :2 * c_mix].astype(BF16),
                                wqkv[:, 2 * c_mix:].T.astype(BF16), gq3, gk3, cs_c)
            extras = (row(od_lam_q1[i]), row(od_lam_k1[i]), row(od_lam_q2[i]), row(od_lam_k2[i]),
                      row(od_c_subnorm[i]))
            yc = _attn_causal(functools.partial(_attn_diff_kernel, lam_init=lam_init), q, k, v, extras,
                              batch=batch, seq=seq, n_blocks=C_HEADS, qk_width=LANES)
            h1, xn, idx, gates = _out_router(h, yc, od_w_out[i].astype(BF16), row(od_norm_ff[i]),
                                             od_router[i].T.astype(F32))
            row_of_pair, visits = _route_plan(idx[:, :2])
            xs = _dispatch(row_of_pair, xn)
            ys = _moe(visits, xs, od_ex_gate[i], od_ex_up[i], od_ex_down[i])
            h = _combine(row_of_pair, h1, gates, ys)
    return h.reshape(batch, seq, d)
```

```python
import functools
import math

import numpy as np
import jax
import jax.numpy as jnp
from jax import lax
from jax.experimental import pallas as pl
from jax.experimental.pallas import tpu as pltpu

F32 = jnp.float32
BF16 = jnp.bfloat16

D_MODEL = 1024
CHUNK = 64
EPS = 1e-6
LOG2E = 1.4426950408889634
NEG = -1e30

A_HEADS = 8
A_NOPE = 64
A_ROPE = 32
A_V = 64
A_Q_LORA = 256
A_KV_LORA = 128
A_THETA = 10000.0
B_HEADS = 8
B_DH = 64
B_LEFT_CHUNKS = 8
B_REL_CLIP = 128
C_HEADS = 8
C_DH = 64
C_ROPE = C_DH // 4
ROPE_THETA = 500000.0
FF_DENSE = 2752
N_EXPERTS = 8
FF_EXPERT = 3584

LANES = 128
FF_DENSE_PAD = 2816
VMEM_LIMIT = 56 * 1024 * 1024

TM_PROJ = 512
PROJ_PARTS = 2
TQ_CAUSAL = 256
TQ_BAND = 256
BAND_KEYS = TQ_BAND + B_LEFT_CHUNKS * CHUNK
BAND_PROFILE = 1024
TM_DISPATCH = 256
DMA_UNROLL = 8
TR_MOE = 1024
MOE_SUB = 256
FC_MOE = 512


def _cparams(sem):
    return pltpu.CompilerParams(dimension_semantics=sem, vmem_limit_bytes=VMEM_LIMIT)


def _const_spec(shape):
    nd = len(shape)
    return pl.BlockSpec(shape, lambda *_: (0,) * nd, pipeline_mode=pl.Buffered(1))


def _rms_scale(x):
    return lax.rsqrt(jnp.mean(x * x, axis=-1, keepdims=True) + EPS)


def _lane_iota():
    return lax.broadcasted_iota(jnp.int32, (1, LANES), 1)


def _dot_nt(a, b):
    return lax.dot_general(a, b, (((1,), (1,)), ((), ())), preferred_element_type=F32)


def _rope_tables(cs, half, starts):
    tm = cs.shape[0]
    base = jnp.concatenate([cs, jnp.zeros((tm, LANES - 2 * half), F32)], axis=1)
    shifted = lambda s: base if s % LANES == 0 else pltpu.roll(base, s % LANES, 1)
    lane = _lane_iota()
    cos_t = jnp.ones((tm, LANES), F32)
    sin_a = jnp.zeros((tm, LANES), F32)
    sin_b = jnp.zeros((tm, LANES), F32)
    for s0 in starts:
        first = (lane >= s0) & (lane < s0 + half)
        second = (lane >= s0 + half) & (lane < s0 + 2 * half)
        at_s0 = shifted(s0)
        cos_t = jnp.where(first, at_s0, jnp.where(second, shifted(s0 + half), cos_t))
        sin_a = jnp.where(first, -shifted(s0 - half), sin_a)
        sin_b = jnp.where(second, at_s0, sin_b)
    return cos_t, sin_a, sin_b


def _store_gained(tables, g3_ref, tab_ref):
    for j in range(3):
        tab_ref[j] = tables[j] * g3_ref[j:j + 1, :]


def _rope(x, half, tab_ref, rows=slice(None)):
    return (x * tab_ref[0, rows, :] + pltpu.roll(x, LANES - half, 1) * tab_ref[1, rows, :]
            + pltpu.roll(x, half, 1) * tab_ref[2, rows, :])


def _proj_even_kernel(h_ref, gmix_ref, win_ref, wvbt_ref, gcq_ref, gckv_ref, wuq_ref, wuk_ref, wuvt_ref,
                      gq3_ref, gk_ref, gkr3_ref, gbq_ref, gbk_ref, cs_ref,
                      qa_ref, ka_ref, vat_ref, qb_ref, kb_ref, vbt_ref, tabq_ref, tabk_ref):
    x = h_ref[...]
    hn = (x * _rms_scale(x) * gmix_ref[...]).astype(BF16)
    u = jnp.dot(hn, win_ref[...], preferred_element_type=F32)
    vbt_ref[...] = _dot_nt(wvbt_ref[...], hn).astype(BF16)
    lane = _lane_iota()
    lo = lane < 64
    tables = _rope_tables(cs_ref[...], A_ROPE // 2, (A_NOPE,))
    _store_gained(tables, gq3_ref, tabq_ref)
    _store_gained(tables, gkr3_ref, tabk_ref)
    rope_q = lambda v: _rope(v, A_ROPE // 2, tabq_ref)
    rope_k = lambda v: _rope(v, A_ROPE // 2, tabk_ref)

    cq = u[:, 0:A_Q_LORA]
    cqn = (cq * _rms_scale(cq) * gcq_ref[...]).astype(BF16)
    q = jnp.dot(cqn, wuq_ref[...], preferred_element_type=F32)
    ckv = u[:, A_Q_LORA:A_Q_LORA + A_KV_LORA]
    ckvn = (ckv * _rms_scale(ckv) * gckv_ref[...]).astype(BF16)
    kn = jnp.dot(ckvn, wuk_ref[...], preferred_element_type=F32)
    vat_ref[...] = _dot_nt(wuvt_ref[...], ckvn).astype(BF16)

    kr = u[:, 384:512]
    kr_s = lax.rsqrt(jnp.sum(kr * kr, axis=-1, keepdims=True) * (1.0 / A_ROPE) + EPS)
    krot = rope_k(kr) * kr_s

    for hd in range(A_HEADS):
        sl = slice(hd * LANES, (hd + 1) * LANES)
        blk = q[:, sl]
        sq = blk * blk
        tot = jnp.sum(sq, axis=-1, keepdims=True)
        ssn = jnp.sum(jnp.where(lo, sq, 0.0), axis=-1, keepdims=True)
        sc = jnp.where(lo, lax.rsqrt(ssn * (1.0 / A_NOPE) + EPS),
                       lax.rsqrt((tot - ssn) * (1.0 / A_ROPE) + EPS))
        qa_ref[:, sl] = (rope_q(blk) * sc).astype(BF16)
        kb_ = kn[:, sl]
        ks = lax.rsqrt(jnp.sum(kb_ * kb_, axis=-1, keepdims=True) * (1.0 / A_NOPE) + EPS)
        ka_ref[:, sl] = (kb_ * ks * gk_ref[:, sl] + krot).astype(BF16)

    def pair_norm(blk, g):
        sq = blk * blk
        tot = jnp.sum(sq, axis=-1, keepdims=True)
        s_lo = jnp.sum(jnp.where(lo, sq, 0.0), axis=-1, keepdims=True)
        sc = jnp.where(lo, lax.rsqrt(s_lo * (1.0 / B_DH) + EPS),
                       lax.rsqrt((tot - s_lo) * (1.0 / B_DH) + EPS))
        return (blk * sc * g).astype(BF16)

    for p in range(B_HEADS // 2):
        sl = slice(p * LANES, (p + 1) * LANES)
        qb_ref[:, sl] = pair_norm(u[:, 512 + p * LANES:512 + (p + 1) * LANES], gbq_ref[:, sl])
        kb_ref[:, sl] = pair_norm(u[:, 1024 + p * LANES:1024 + (p + 1) * LANES], gbk_ref[:, sl])


def _proj_even(h, gmix, win, wvbt, gcq, gckv, wuq, wuk, wuvt, gq3, gk, gkr3, gbq, gbk, cs):
    t = h.shape[0]
    tm = TM_PROJ
    row = lambda w: pl.BlockSpec((tm, w), lambda i: (i, 0))
    col = lambda w: pl.BlockSpec((w, tm), lambda i: (0, i))
    rshape = lambda w: jax.ShapeDtypeStruct((t, w), BF16)
    cshape = lambda w: jax.ShapeDtypeStruct((w, t), BF16)
    consts = (gmix, win, wvbt, gcq, gckv, wuq, wuk, wuvt, gq3, gk, gkr3, gbq, gbk)
    return pl.pallas_call(
        _proj_even_kernel,
        grid=(t // tm,),
        in_specs=[row(D_MODEL)] + [_const_spec(c.shape) for c in consts] + [row(cs.shape[1])],
        out_specs=[row(1024), row(1024), col(512), row(512), row(512), col(512)],
        out_shape=[rshape(1024), rshape(1024), cshape(512), rshape(512), rshape(512), cshape(512)],
        scratch_shapes=[pltpu.VMEM((3, tm, LANES), F32)] * 2,
        compiler_params=_cparams(("parallel",)),
        name="proj_even",
    )(h, *consts, cs)


def _causal_scores(q_ref, k_ref, s_ref, c, *, split):
    tq = TQ_CAUSAL
    q = q_ref[c * tq:(c + 1) * tq, :]
    zero = jnp.zeros_like(q[:, :LANES])
    if split:
        q2 = jnp.concatenate([jnp.concatenate([q[:, :LANES], zero], axis=1),
                              jnp.concatenate([zero, q[:, LANES:]], axis=1)], axis=0)
    else:
        lo = _lane_iota() < 64
        q2 = jnp.concatenate([jnp.where(lo, q, zero), jnp.where(lo, zero, q)], axis=0)
    kchunk = lax.broadcasted_iota(jnp.int32, (tq, 2 * tq), 0) // CHUNK
    qchunk = (lax.broadcasted_iota(jnp.int32, (tq, 2 * tq), 1) % tq) // CHUNK
    n_past = c * tq
    s_d = jnp.where(kchunk <= qchunk, _dot_nt(k_ref[n_past:n_past + tq, :], q2), NEG)
    s_ref[n_past:n_past + tq, :] = s_d
    mx = jnp.max(s_d, axis=0, keepdims=True)
    if c:
        s_p = _dot_nt(k_ref[0:n_past, :], q2)
        s_ref[0:n_past, :] = s_p
        mx = jnp.maximum(mx, jnp.max(s_p, axis=0, keepdims=True))
    return mx


def _causal_values(vt_ref, s_ref, c, mx):
    n_keys = (c + 1) * TQ_CAUSAL
    p = jnp.exp2(s_ref[0:n_keys, :] - mx)
    o = jnp.dot(vt_ref[:, 0:n_keys], p.astype(BF16), preferred_element_type=F32)
    return o * (1.0 / jnp.sum(p, axis=0, keepdims=True))


def _causal_tiles(q_ref, k_ref, vt_ref, s_buf, *, split):
    tq = TQ_CAUSAL
    n = q_ref.shape[0] // tq
    mx = _causal_scores(q_ref, k_ref, s_buf.at[0], 0, split=split)
    for c in range(n):
        mx_next = None
        if c + 1 < n:
            mx_next = _causal_scores(q_ref, k_ref, s_buf.at[(c + 1) % 2], c + 1, split=split)
        o = _causal_values(vt_ref, s_buf.at[c % 2], c, mx)
        yield c, o[:, :tq], o[:, tq:]
        mx = mx_next


def _row_iota(n):
    return lax.broadcasted_iota(jnp.int32, (LANES, n), 0)


def _attn_mla_kernel(q_ref, k_ref, vt_ref, o_ref, s_buf):
    tq = TQ_CAUSAL
    for c, o0, o1 in _causal_tiles(q_ref, k_ref, vt_ref, s_buf, split=True):
        o_ref[c * tq:(c + 1) * tq, :] = jnp.where(_row_iota(tq) < 64, o0, o1).T.astype(o_ref.dtype)


def _attn_diff_kernel(q_ref, k_ref, vt_ref, lq1_ref, lk1_ref, lq2_ref, lk2_ref, gsub_ref, o_ref, s_buf,
                      *, lam_init):
    tq = TQ_CAUSAL
    lam = (jnp.exp(jnp.sum(lq1_ref[...] * lk1_ref[...], axis=-1, keepdims=True))
           - jnp.exp(jnp.sum(lq2_ref[...] * lk2_ref[...], axis=-1, keepdims=True)) + lam_init)
    for c, o0, o1 in _causal_tiles(q_ref, k_ref, vt_ref, s_buf, split=False):
        o = (o0 - lam * o1).T
        o = o * _rms_scale(o) * gsub_ref[...] * (1.0 - lam_init)
        o_ref[c * tq:(c + 1) * tq, :] = o.astype(o_ref.dtype)


def _attn_causal(kernel, q, k, vt, extras, *, batch, seq, n_blocks, qk_width):
    in_specs = [pl.BlockSpec((seq, qk_width), lambda b, p: (b, p)),
                pl.BlockSpec((seq, qk_width), lambda b, p: (b, p)),
                pl.BlockSpec((LANES, seq), lambda b, p: (p, b))]
    in_specs += [_const_spec(e.shape) for e in extras]
    return pl.pallas_call(
        kernel,
        grid=(batch, n_blocks),
        in_specs=in_specs,
        out_specs=pl.BlockSpec((seq, LANES), lambda b, p: (b, p)),
        out_shape=jax.ShapeDtypeStruct((batch * seq, n_blocks * LANES), BF16),
        scratch_shapes=[pltpu.VMEM((2, seq, 2 * TQ_CAUSAL), F32)],
        compiler_params=_cparams(("parallel", "parallel")),
        name=kernel.func.__name__ if isinstance(kernel, functools.partial) else kernel.__name__,
    )(q, k, vt, *extras)


def _attn_band_kernel(q_ref, k_ref, vt_ref, bias_ref, o_ref, s_buf):
    tq = TQ_BAND
    n = q_ref.shape[0] // tq
    lo = _lane_iota() < 64

    def window(c):
        end = (c + 1) * tq
        return max(0, end - BAND_KEYS), end

    def scores(c):
        start, end = window(c)
        q = q_ref[c * tq:(c + 1) * tq, :]
        zero = jnp.zeros_like(q)
        q2 = jnp.concatenate([jnp.where(lo, q, zero), jnp.where(lo, zero, q)], axis=0)
        s = _dot_nt(k_ref[start:end, :], q2) + bias_ref[BAND_KEYS - (end - start):, :]
        s_buf[c % 2, 0:end - start, :] = s
        return jnp.max(s, axis=0, keepdims=True)

    mx = scores(0)
    for c in range(n):
        mx_next = scores(c + 1) if c + 1 < n else None
        start, end = window(c)
        p = jnp.exp2(s_buf[c % 2, 0:end - start, :] - mx)
        o = jnp.dot(vt_ref[:, start:end], p.astype(BF16), preferred_element_type=F32)
        o = o * (1.0 / jnp.sum(p, axis=0, keepdims=True))
        o_ref[c * tq:(c + 1) * tq, :] = jnp.where(_row_iota(tq) < 64, o[:, :tq], o[:, tq:]).T.astype(o_ref.dtype)
        mx = mx_next


def _attn_band(q, k, vt, bias, *, batch, seq):
    npair = B_HEADS // 2
    return pl.pallas_call(
        _attn_band_kernel,
        grid=(batch, npair),
        in_specs=[pl.BlockSpec((seq, LANES), lambda b, p: (b, p)),
                  pl.BlockSpec((seq, LANES), lambda b, p: (b, p)),
                  pl.BlockSpec((LANES, seq), lambda b, p: (p, b)),
                  pl.BlockSpec((None, BAND_KEYS, 2 * TQ_BAND), lambda b, p: (p, 0, 0))],
        out_specs=pl.BlockSpec((seq, LANES), lambda b, p: (b, p)),
        out_shape=jax.ShapeDtypeStruct((batch * seq, npair * LANES), BF16),
        scratch_shapes=[pltpu.VMEM((2, BAND_KEYS, 2 * TQ_BAND), F32)],
        compiler_params=_cparams(("parallel", "parallel")),
        name="attn_band",
    )(q, k, vt, bias)


def _silu(g):
    return g / (1.0 + jnp.exp(-g))


def _out_ffn_kernel(h_ref, ya_ref, yb_ref, woa_ref, wob_ref, gff_ref, wg_ref, wu_ref, wd_ref, o_ref):
    h1 = (h_ref[...] + jnp.dot(ya_ref[...], woa_ref[...], preferred_element_type=F32)
          + jnp.dot(yb_ref[...], wob_ref[...], preferred_element_type=F32))
    xn = (h1 * _rms_scale(h1) * gff_ref[...]).astype(BF16)
    g = jnp.dot(xn, wg_ref[...], preferred_element_type=F32)
    u = jnp.dot(xn, wu_ref[...], preferred_element_type=F32)
    a = (_silu(g) * u).astype(BF16)
    o_ref[...] = h1 + jnp.dot(a, wd_ref[...], preferred_element_type=F32)


def _out_ffn(h, ya, yb, woa, wob, gff, wg, wu, wd):
    t = h.shape[0]
    tm = TM_PROJ
    row = lambda w: pl.BlockSpec((tm, w), lambda i: (i, 0))
    return pl.pallas_call(
        _out_ffn_kernel,
        grid=(t // tm,),
        in_specs=[row(D_MODEL), row(ya.shape[1]), row(yb.shape[1])]
        + [_const_spec(w.shape) for w in (woa, wob, gff, wg, wu, wd)],
        out_specs=row(D_MODEL),
        out_shape=jax.ShapeDtypeStruct((t, D_MODEL), F32),
        compiler_params=_cparams(("parallel",)),
        name="out_ffn",
    )(h, ya, yb, woa, wob, gff, wg, wu, wd)


def _proj_odd_kernel(h_ref, gmix_ref, w_ref, wvt_ref, gq3_ref, gk3_ref, cs_ref, q_ref, k_ref, vt_ref,
                     tabq_ref, tabk_ref, u_buf):
    n_part = u_buf.shape[0]
    rows_part = u_buf.shape[1]
    lo = _lane_iota() < 64
    width = C_HEADS * 2 * C_DH

    def project(j):
        rows = slice(j * rows_part, (j + 1) * rows_part)
        x = h_ref[rows, :]
        hn = (x * _rms_scale(x) * gmix_ref[...]).astype(BF16)
        u_buf[j] = jnp.dot(hn, w_ref[...], preferred_element_type=F32)
        vt_ref[:, rows] = _dot_nt(wvt_ref[...], hn).astype(BF16)

    project(0)
    tables = _rope_tables(cs_ref[...], C_ROPE // 2, (0, C_DH))
    _store_gained(tables, gq3_ref, tabq_ref)
    _store_gained(tables, gk3_ref, tabk_ref)

    def norm_rope(blk, tab_ref, rows):
        sq = blk * blk
        tot = jnp.sum(sq, axis=-1, keepdims=True)
        s_lo = jnp.sum(jnp.where(lo, sq, 0.0), axis=-1, keepdims=True)
        sc = jnp.where(lo, lax.rsqrt(s_lo * (1.0 / C_DH) + EPS),
                       lax.rsqrt((tot - s_lo) * (1.0 / C_DH) + EPS))
        return (_rope(blk, C_ROPE // 2, tab_ref, rows) * sc).astype(BF16)

    for j in range(n_part):
        if j + 1 < n_part:
            project(j + 1)
        rows = slice(j * rows_part, (j + 1) * rows_part)
        for hd in range(C_HEADS):
            sl = slice(hd * LANES, (hd + 1) * LANES)
            q_ref[rows, sl] = norm_rope(u_buf[j, :, sl], tabq_ref, rows)
            k_ref[rows, sl] = norm_rope(u_buf[j, :, width + hd * LANES:width + (hd + 1) * LANES],
                                        tabk_ref, rows)


def _proj_odd(h, gmix, w, wvt, gq3, gk3, cs):
    t = h.shape[0]
    tm = TM_PROJ
    row = lambda wd: pl.BlockSpec((tm, wd), lambda i: (i, 0))
    width = C_HEADS * 2 * C_DH
    consts = (gmix, w, wvt, gq3, gk3)
    return pl.pallas_call(
        _proj_odd_kernel,
        grid=(t // tm,),
        in_specs=[row(D_MODEL)] + [_const_spec(c.shape) for c in consts] + [row(cs.shape[1])],
        out_specs=[row(width), row(width), pl.BlockSpec((width, tm), lambda i: (0, i))],
        out_shape=[jax.ShapeDtypeStruct((t, width), BF16), jax.ShapeDtypeStruct((t, width), BF16),
                   jax.ShapeDtypeStruct((width, t), BF16)],
        scratch_shapes=[pltpu.VMEM((3, tm, LANES), F32)] * 2
        + [pltpu.VMEM((PROJ_PARTS, tm // PROJ_PARTS, 2 * width), F32)],
        compiler_params=_cparams(("parallel",)),
        name="proj_odd",
    )(h, *consts, cs)


def _out_router_kernel(h_ref, y_ref, wo_ref, gff_ref, wr_ref, h1_ref, xn_ref, idx_ref, gate_ref):
    h1 = h_ref[...] + jnp.dot(y_ref[...], wo_ref[...], preferred_element_type=F32)
    h1_ref[...] = h1
    xn = h1 * _rms_scale(h1) * gff_ref[...]
    xn_ref[...] = xn
    logits = [jnp.sum(xn * wr_ref[e:e + 1, :], axis=-1, keepdims=True) for e in range(N_EXPERTS)]
    m1 = functools.reduce(jnp.maximum, logits)
    idx1 = functools.reduce(jnp.minimum,
                            [jnp.where(logits[e] == m1, e, N_EXPERTS) for e in range(N_EXPERTS)])
    rest = [jnp.where(idx1 == e, -jnp.inf, logits[e]) for e in range(N_EXPERTS)]
    m2 = functools.reduce(jnp.maximum, rest)
    idx2 = functools.reduce(jnp.minimum,
                            [jnp.where(rest[e] == m2, e, N_EXPERTS) for e in range(N_EXPERTS)])
    e2 = jnp.exp(m2 - m1)
    g1 = 1.0 / (1.0 + e2)
    g2 = e2 * g1
    lane = _lane_iota()
    idx_ref[...] = jnp.where(lane == 0, idx1, jnp.where(lane == 1, idx2, 0))
    gate_ref[...] = jnp.where(lane == 0, g1, jnp.where(lane == 1, g2, 0.0))


def _out_router(h, y, wo, gff, wr):
    t = h.shape[0]
    tm = TM_PROJ
    row = lambda w: pl.BlockSpec((tm, w), lambda i: (i, 0))
    return pl.pallas_call(
        _out_router_kernel,
        grid=(t // tm,),
        in_specs=[row(D_MODEL), row(y.shape[1]), _const_spec(wo.shape), _const_spec(gff.shape),
                  _const_spec(wr.shape)],
        out_specs=[row(D_MODEL), row(D_MODEL), row(LANES), row(LANES)],
        out_shape=[jax.ShapeDtypeStruct((t, D_MODEL), F32), jax.ShapeDtypeStruct((t, D_MODEL), F32),
                   jax.ShapeDtypeStruct((t, LANES), jnp.int32), jax.ShapeDtypeStruct((t, LANES), F32)],
        compiler_params=_cparams(("parallel",)),
        name="out_router",
    )(h, y, wo, gff, wr)


def _row_copy(src_ref, src_row, dst_ref, dst_row, sem):
    return pltpu.make_async_copy(src_ref.at[pl.ds(src_row, 1)], dst_ref.at[pl.ds(dst_row, 1)], sem)


def _dispatch_kernel(row_ref, x_ref, xs_ref, sem):
    tm = x_ref.shape[0]
    base = pl.program_id(0) * tm

    def issue(r, carry):
        for k in range(2):
            _row_copy(x_ref, r, xs_ref, row_ref[2 * (base + r) + k], sem).start(priority=k)
        return carry

    lax.fori_loop(0, tm, issue, 0, unroll=DMA_UNROLL)
    for k in range(2):
        pltpu.make_async_copy(x_ref, xs_ref.at[pl.ds(0, tm)], sem).wait()


def _dispatch(row_of_pair, xn):
    t, d = xn.shape
    tm = TM_DISPATCH
    grid_spec = pltpu.PrefetchScalarGridSpec(
        num_scalar_prefetch=1,
        grid=(t // tm,),
        in_specs=[pl.BlockSpec((tm, d), lambda i, rows: (i, 0))],
        out_specs=pl.BlockSpec(memory_space=pl.ANY),
        scratch_shapes=[pltpu.SemaphoreType.DMA(())],
    )
    return pl.pallas_call(
        _dispatch_kernel,
        grid_spec=grid_spec,
        out_shape=jax.ShapeDtypeStruct((2 * t, d), xn.dtype),
        compiler_params=_cparams(("arbitrary",)),
        name="moe_dispatch",
    )(row_of_pair, xn)


def _moe_kernel(vt_ref, ve_ref, lo_ref, hi_ref, first_ref, x_ref, wg_ref, wu_ref, wd_ref, o_ref, xb_ref):
    v = pl.program_id(0)
    c = pl.program_id(1)
    tr = x_ref.shape[0]

    @pl.when((c == 0) & (first_ref[v] == 1))
    def _():
        o_ref[...] = jnp.zeros_like(o_ref)

    @pl.when(c == 0)
    def _():
        xb_ref[...] = x_ref[...].astype(BF16)

    lo = lo_ref[v]
    hi = hi_ref[v]
    base = vt_ref[v] * tr

    def expert_rows(off, n, whole):
        rows = slice(off, off + n)
        x = xb_ref[rows, :]
        g = jnp.dot(x, wg_ref[0].astype(BF16), preferred_element_type=F32)
        u = jnp.dot(x, wu_ref[0].astype(BF16), preferred_element_type=F32)
        a = (_silu(g) * u).astype(BF16)
        y = jnp.dot(a, wd_ref[0].astype(BF16), preferred_element_type=F32)
        if not whole:
            r = base + off + lax.broadcasted_iota(jnp.int32, (n, 1), 0)
            y = jnp.where((r >= lo) & (r < hi), y, 0.0)
        o_ref[rows, :] += y

    whole_tile = (lo == base) & (hi == base + tr)

    @pl.when(whole_tile)
    def _():
        expert_rows(0, tr, True)

    for s in range(tr // MOE_SUB):
        @pl.when(jnp.logical_not(whole_tile) & (base + s * MOE_SUB < hi) & (base + (s + 1) * MOE_SUB > lo))
        def _(s=s):
            expert_rows(s * MOE_SUB, MOE_SUB, False)


def _moe(visits, xs, wg, wu, wd):
    r = xs.shape[0]
    tr, fc = TR_MOE, FC_MOE
    nc = FF_EXPERT // fc
    n_visits = r // tr + N_EXPERTS - 1

    def chunk(v, c, lo, hi):
        return jnp.where(lo[v] < hi[v], c, nc - 1)

    tile = lambda v, c, vt, ve, lo, hi, first: (vt[v], 0)
    wcol = lambda v, c, vt, ve, lo, hi, first: (ve[v], 0, chunk(v, c, lo, hi))
    wrow = lambda v, c, vt, ve, lo, hi, first: (ve[v], chunk(v, c, lo, hi), 0)
    grid_spec = pltpu.PrefetchScalarGridSpec(
        num_scalar_prefetch=5,
        grid=(n_visits, nc),
        in_specs=[pl.BlockSpec((tr, D_MODEL), tile),
                  pl.BlockSpec((1, D_MODEL, fc), wcol),
                  pl.BlockSpec((1, D_MODEL, fc), wcol),
                  pl.BlockSpec((1, fc, D_MODEL), wrow)],
        out_specs=pl.BlockSpec((tr, D_MODEL), tile),
        scratch_shapes=[pltpu.VMEM((tr, D_MODEL), BF16)],
    )
    return pl.pallas_call(
        _moe_kernel,
        grid_spec=grid_spec,
        out_shape=jax.ShapeDtypeStruct((r, D_MODEL), F32),
        compiler_params=_cparams(("arbitrary", "arbitrary")),
        name="moe_grouped",
    )(*visits, xs, wg, wu, wd)


def _combine_kernel(row_ref, h_ref, gate_ref, ys_ref, o_ref, buf, sem):
    tm = h_ref.shape[0]
    base = pl.program_id(0) * tm

    def issue(r, carry):
        for k in range(2):
            _row_copy(ys_ref, row_ref[2 * (base + r) + k], buf, k * tm + r, sem).start(priority=k)
        return carry

    lax.fori_loop(0, tm, issue, 0, unroll=DMA_UNROLL)
    pltpu.make_async_copy(ys_ref.at[pl.ds(0, 2 * tm)], buf, sem).wait()
    gate = gate_ref[...]
    o_ref[...] = h_ref[...] + gate[:, 0:1] * buf[0:tm, :] + gate[:, 1:2] * buf[tm:2 * tm, :]


def _combine(row_of_pair, h1, gates, ys):
    t, d = h1.shape
    tm = TM_DISPATCH
    grid_spec = pltpu.PrefetchScalarGridSpec(
        num_scalar_prefetch=1,
        grid=(t // tm,),
        in_specs=[pl.BlockSpec((tm, d), lambda i, rows: (i, 0)),
                  pl.BlockSpec((tm, LANES), lambda i, rows: (i, 0)),
                  pl.BlockSpec(memory_space=pl.ANY)],
        out_specs=pl.BlockSpec((tm, d), lambda i, rows: (i, 0)),
        scratch_shapes=[pltpu.VMEM((2 * tm, d), F32), pltpu.SemaphoreType.DMA(())],
    )
    return pl.pallas_call(
        _combine_kernel,
        grid_spec=grid_spec,
        out_shape=jax.ShapeDtypeStruct((t, d), F32),
        compiler_params=_cparams(("arbitrary",)),
        name="moe_combine",
    )(row_of_pair, h1, gates, ys)


def _route_plan(idx):
    n_pairs = idx.size
    tr = TR_MOE
    n_tiles = n_pairs // tr
    n_visits = n_tiles + N_EXPERTS - 1
    e_flat = idx.reshape(-1)
    onehot = (e_flat[:, None] == jnp.arange(N_EXPERTS, dtype=jnp.int32)[None, :]).astype(jnp.int32)
    csum = jnp.cumsum(onehot, axis=0)
    counts = csum[-1]
    ends = jnp.cumsum(counts)
    starts = ends - counts
    pos = jnp.sum((csum - onehot) * onehot, axis=1)
    row_of_pair = (starts[e_flat] + pos).astype(jnp.int32)
    nv = jnp.where(counts > 0, (ends - 1) // tr - starts // tr + 1, 0)
    vend = jnp.cumsum(nv)
    vid = jnp.arange(n_visits, dtype=jnp.int32)
    live = vid < vend[-1]
    ve = jnp.minimum(jnp.sum((vid[:, None] >= vend[None, :]).astype(jnp.int32), axis=1), N_EXPERTS - 1)
    vt = jnp.where(live, starts[ve] // tr + vid - (vend - nv)[ve], n_tiles - 1)
    lo = jnp.where(live, jnp.maximum(starts[ve], vt * tr), 0)
    hi = jnp.where(live, jnp.minimum(ends[ve], (vt + 1) * tr), 0)
    last_e = jnp.max(jnp.where(counts > 0, jnp.arange(N_EXPERTS, dtype=jnp.int32), 0))
    ve = jnp.where(live, ve, last_e)
    first = (live & (lo == vt * tr)).astype(jnp.int32)
    i32 = lambda a: a.astype(jnp.int32)
    return row_of_pair, (i32(vt), i32(ve), i32(lo), i32(hi), first)


def _cos_sin(positions, dim, theta):
    inv = 1.0 / (theta ** (jnp.arange(0, dim, 2, dtype=F32) / dim))
    ang = positions.astype(F32).reshape(-1, 1) * inv
    return jnp.concatenate([jnp.cos(ang), jnp.sin(ang)], axis=1)


def _gain3(g, half):
    return jnp.stack([g, jnp.roll(g, -half), jnp.roll(g, half)]).astype(F32)


def _band_bias(rel_table):
    nh = rel_table.shape[0]
    nk, nq = BAND_KEYS, TQ_BAND
    length = BAND_PROFILE
    n_lo = nk - 1 - B_LEFT_CHUNKS * CHUNK - B_REL_CLIP
    n_hi = length - n_lo - rel_table.shape[1]
    prof = jnp.concatenate([jnp.broadcast_to(rel_table[:, :1], (nh, n_lo)), rel_table,
                            jnp.broadcast_to(rel_table[:, -1:], (nh, n_hi))], axis=1)
    return pl.pallas_call(
        _band_bias_kernel,
        grid=(nh,),
        in_specs=[pl.BlockSpec((None, 1, length), lambda h: (h, 0, 0))],
        out_specs=pl.BlockSpec((None, nk, nq), lambda h: (h // 2, 0, h % 2)),
        out_shape=jax.ShapeDtypeStruct((nh // 2, nk, 2 * nq), F32),
        compiler_params=_cparams(("parallel",)),
        name="band_bias",
    )(prof.astype(F32).reshape(nh, 1, length))


def _band_bias_kernel(p_ref, o_ref):
    nk, nq = o_ref.shape
    length = p_ref.shape[1]
    x = jnp.broadcast_to(p_ref[...] * LOG2E, (8, length))
    sub = lax.broadcasted_iota(jnp.int32, (8, nq), 0)
    qchunk = lax.broadcasted_iota(jnp.int32, (8, nq), 1) // CHUNK

    def body(jb, carry):
        rolled = pltpu.roll(x, jb * 8 + (length - (nk - 1)), 1, stride=1, stride_axis=0)
        cdiff = qchunk + B_LEFT_CHUNKS - (jb * 8 + sub) // CHUNK
        allowed = (cdiff >= 0) & (cdiff <= B_LEFT_CHUNKS)
        o_ref[pl.ds(pl.multiple_of(jb * 8, 8), 8), :] = jnp.where(allowed, rolled[:, :nq], NEG)
        return carry

    lax.fori_loop(0, nk // 8, body, 0, unroll=8)


def _pad_last(a, width):
    return jnp.pad(a, [(0, 0)] * (a.ndim - 1) + [(0, width - a.shape[-1])])


def kernel(x, positions, ev_norm_mix, ev_w_in, ev_g_cq, ev_g_ckv, ev_w_uq, ev_w_ukv, ev_a_qnorm, ev_a_knorm, ev_b_qnorm, ev_b_knorm, ev_b_rel_bias, ev_w_out, ev_norm_ff, ev_ff_gate, ev_ff_up, ev_ff_down, od_norm_mix, od_w_qkv, od_c_qnorm, od_c_knorm, od_lam_q1, od_lam_k1, od_lam_q2, od_lam_k2, od_c_subnorm, od_w_out, od_norm_ff, od_router, od_ex_gate, od_ex_up, od_ex_down):
    batch, seq, d = x.shape
    t = batch * seq
    depth = ev_w_in.shape[0] + od_w_qkv.shape[0]
    h = x.reshape(t, d)
    row = lambda a: a.reshape(1, -1).astype(F32)
    cs_a = _cos_sin(positions, A_ROPE, A_THETA)
    cs_c = _cos_sin(positions, C_ROPE, ROPE_THETA)

    for layer in range(depth):
        i = layer // 2
        if layer % 2 == 0:
            w = ev_w_in[i]
            o1, o2, o3 = A_Q_LORA, A_Q_LORA + A_KV_LORA, A_Q_LORA + A_KV_LORA + A_ROPE
            zc = lambda n: jnp.zeros((d, n), F32)
            bw = B_HEADS * B_DH
            win = jnp.concatenate([w[:, :o2], zc(64), w[:, o2:o3], zc(32), w[:, o3:o3 + 2 * bw]], 1).astype(BF16)
            wvbt = w[:, o3 + 2 * bw:].T.astype(BF16)
            wuq = _pad_last(ev_w_uq[i], LANES).reshape(A_Q_LORA, A_HEADS * LANES).astype(BF16)
            wuk = _pad_last(ev_w_ukv[i][:, :, :A_NOPE], LANES).reshape(A_KV_LORA, A_HEADS * LANES).astype(BF16)
            wuvt = ev_w_ukv[i][:, :, A_NOPE:].reshape(A_KV_LORA, A_HEADS * A_V).T.astype(BF16)
            sc_a = (A_NOPE + A_ROPE) ** -0.5 * LOG2E
            gq3 = _gain3(_pad_last(ev_a_qnorm[i], LANES) * sc_a, A_ROPE // 2)
            gk = jnp.tile(_pad_last(ev_a_knorm[i][:A_NOPE], LANES), A_HEADS).reshape(1, -1)
            gkr3 = _gain3(jnp.concatenate([jnp.zeros((A_NOPE,), F32), ev_a_knorm[i][A_NOPE:],
                                           jnp.zeros((32,), F32)]), A_ROPE // 2)
            gbq = jnp.tile(ev_b_qnorm[i], B_HEADS).reshape(1, -1) * (B_DH ** -0.5 * LOG2E)
            gbk = jnp.tile(ev_b_knorm[i], B_HEADS).reshape(1, -1)
            qa, ka, vat, qb, kb, vbt = _proj_even(
                h, row(ev_norm_mix[i]), win, wvbt, row(ev_g_cq[i]), row(ev_g_ckv[i]), wuq, wuk, wuvt,
                gq3, gk, gkr3, gbq, gbk, cs_a)
            ya = _attn_causal(_attn_mla_kernel, qa, ka, vat, (), batch=batch, seq=seq,
                              n_blocks=A_HEADS // 2, qk_width=2 * LANES)
            yb = _attn_band(qb, kb, vbt, _band_bias(ev_b_rel_bias[i]), batch=batch, seq=seq)
            wo = ev_w_out[i].astype(BF16)
            na = A_HEADS * A_V
            padf = lambda a: _pad_last(a, FF_DENSE_PAD).astype(BF16)
            wd = jnp.pad(ev_ff_down[i], ((0, FF_DENSE_PAD - FF_DENSE), (0, 0))).astype(BF16)
            h = _out_ffn(h, ya, yb, wo[:na], wo[na:], row(ev_norm_ff[i]),
                         padf(ev_ff_gate[i]), padf(ev_ff_up[i]), wd)
        else:
            lam_init = 0.8 - 0.6 * math.exp(-0.3 * layer)
            gq3 = _gain3(jnp.tile(od_c_qnorm[i], 2) * (C_DH ** -0.5 * LOG2E), C_ROPE // 2)
            gk3 = _gain3(jnp.tile(od_c_knorm[i], 2), C_ROPE // 2)
            c_mix = C_HEADS * 2 * C_DH
            wqkv = od_w_qkv[i]
            q, k, v = _proj_odd(h, row(od_norm_mix[i]), wqkv[:, :2 * c_mix].astype(BF16),
                                wqkv[:, 2 * c_mix:].T.astype(BF16), gq3, gk3, cs_c)
            extras = (row(od_lam_q1[i]), row(od_lam_k1[i]), row(od_lam_q2[i]), row(od_lam_k2[i]),
                      row(od_c_subnorm[i]))
            yc = _attn_causal(functools.partial(_attn_diff_kernel, lam_init=lam_init), q, k, v, extras,
                              batch=batch, seq=seq, n_blocks=C_HEADS, qk_width=LANES)
            h1, xn, idx, gates = _out_router(h, yc, od_w_out[i].astype(BF16), row(od_norm_ff[i]),
                                             od_router[i].T.astype(F32))
            row_of_pair, visits = _route_plan(idx[:, :2])
            xs = _dispatch(row_of_pair, xn)
            ys = _moe(visits, xs, od_ex_gate[i], od_ex_up[i], od_ex_down[i])
            h = _combine(row_of_pair, h1, gates, ys)
    return h.reshape(batch, seq, d)
```

```python
import functools
import math

import numpy as np
import jax
import jax.numpy as jnp
from jax import lax
from jax.experimental import pallas as pl
from jax.experimental.pallas import tpu as pltpu

F32 = jnp.float32
BF16 = jnp.bfloat16

D_MODEL = 1024
CHUNK = 64
EPS = 1e-6
LOG2E = 1.4426950408889634
NEG = -1e30

A_HEADS = 8
A_NOPE = 64
A_ROPE = 32
A_V = 64
A_Q_LORA = 256
A_KV_LORA = 128
A_THETA = 10000.0
B_HEADS = 8
B_DH = 64
B_LEFT_CHUNKS = 8
B_REL_CLIP = 128
C_HEADS = 8
C_DH = 64
C_ROPE = C_DH // 4
ROPE_THETA = 500000.0
FF_DENSE = 2752
N_EXPERTS = 8
FF_EXPERT = 3584

LANES = 128
FF_DENSE_PAD = 2816
VMEM_LIMIT = 56 * 1024 * 1024

TM_PROJ = 512
PROJ_PARTS = 2
TQ_CAUSAL = 256
TQ_BAND = 256
BAND_KEYS = TQ_BAND + B_LEFT_CHUNKS * CHUNK
BAND_PROFILE = 1024
TM_DISPATCH = 512
DMA_UNROLL = 16
TR_MOE = 1024
MOE_SUB = 256
FC_MOE = 512


def _cparams(sem):
    return pltpu.CompilerParams(dimension_semantics=sem, vmem_limit_bytes=VMEM_LIMIT)


def _const_spec(shape):
    nd = len(shape)
    return pl.BlockSpec(shape, lambda *_: (0,) * nd, pipeline_mode=pl.Buffered(1))


def _rms_scale(x):
    return lax.rsqrt(jnp.mean(x * x, axis=-1, keepdims=True) + EPS)


def _lane_iota():
    return lax.broadcasted_iota(jnp.int32, (1, LANES), 1)


def _dot_nt(a, b):
    return lax.dot_general(a, b, (((1,), (1,)), ((), ())), preferred_element_type=F32)


def _rope_tables(cs, half, starts):
    tm = cs.shape[0]
    base = jnp.concatenate([cs, jnp.zeros((tm, LANES - 2 * half), F32)], axis=1)
    shifted = lambda s: base if s % LANES == 0 else pltpu.roll(base, s % LANES, 1)
    lane = _lane_iota()
    cos_t = jnp.ones((tm, LANES), F32)
    sin_a = jnp.zeros((tm, LANES), F32)
    sin_b = jnp.zeros((tm, LANES), F32)
    for s0 in starts:
        first = (lane >= s0) & (lane < s0 + half)
        second = (lane >= s0 + half) & (lane < s0 + 2 * half)
        at_s0 = shifted(s0)
        cos_t = jnp.where(first, at_s0, jnp.where(second, shifted(s0 + half), cos_t))
        sin_a = jnp.where(first, -shifted(s0 - half), sin_a)
        sin_b = jnp.where(second, at_s0, sin_b)
    return cos_t, sin_a, sin_b


def _store_gained(tables, g3_ref, tab_ref):
    for j in range(3):
        tab_ref[j] = tables[j] * g3_ref[j:j + 1, :]


def _rope(x, half, tab_ref, rows=slice(None)):
    return (x * tab_ref[0, rows, :] + pltpu.roll(x, LANES - half, 1) * tab_ref[1, rows, :]
            + pltpu.roll(x, half, 1) * tab_ref[2, rows, :])


def _proj_even_kernel(h_ref, gmix_ref, win_ref, wvbt_ref, gcq_ref, gckv_ref, wuq_ref, wuk_ref, wuvt_ref,
                      gq3_ref, gk_ref, gkr3_ref, gbq_ref, gbk_ref, cs_ref,
                      qa_ref, ka_ref, vat_ref, qb_ref, kb_ref, vbt_ref, tabq_ref, tabk_ref,
                      u_buf, q_buf, kn_buf):
    n_part = u_buf.shape[0]
    rows_part = u_buf.shape[1]
    lo = _lane_iota() < 64

    def project(j):
        rows = slice(j * rows_part, (j + 1) * rows_part)
        x = h_ref[rows, :]
        hn = (x * _rms_scale(x) * gmix_ref[...]).astype(BF16)
        u = jnp.dot(hn, win_ref[...], preferred_element_type=F32)
        u_buf[j] = u
        vbt_ref[:, rows] = _dot_nt(wvbt_ref[...], hn).astype(BF16)
        cq = u[:, 0:A_Q_LORA]
        cqn = (cq * _rms_scale(cq) * gcq_ref[...]).astype(BF16)
        q_buf[j] = jnp.dot(cqn, wuq_ref[...], preferred_element_type=F32)
        ckv = u[:, A_Q_LORA:A_Q_LORA + A_KV_LORA]
        ckvn = (ckv * _rms_scale(ckv) * gckv_ref[...]).astype(BF16)
        kn_buf[j] = jnp.dot(ckvn, wuk_ref[...], preferred_element_type=F32)
        vat_ref[:, rows] = _dot_nt(wuvt_ref[...], ckvn).astype(BF16)

    project(0)
    tables = _rope_tables(cs_ref[...], A_ROPE // 2, (A_NOPE,))
    _store_gained(tables, gq3_ref, tabq_ref)
    _store_gained(tables, gkr3_ref, tabk_ref)

    def pair_norm(blk, g):
        sq = blk * blk
        tot = jnp.sum(sq, axis=-1, keepdims=True)
        s_lo = jnp.sum(jnp.where(lo, sq, 0.0), axis=-1, keepdims=True)
        sc = jnp.where(lo, lax.rsqrt(s_lo * (1.0 / B_DH) + EPS),
                       lax.rsqrt((tot - s_lo) * (1.0 / B_DH) + EPS))
        return (blk * sc * g).astype(BF16)

    for j in range(n_part):
        if j + 1 < n_part:
            project(j + 1)
        rows = slice(j * rows_part, (j + 1) * rows_part)
        kr = u_buf[j, :, 384:512]
        kr_s = lax.rsqrt(jnp.sum(kr * kr, axis=-1, keepdims=True) * (1.0 / A_ROPE) + EPS)
        krot = _rope(kr, A_ROPE // 2, tabk_ref, rows) * kr_s
        for hd in range(A_HEADS):
            sl = slice(hd * LANES, (hd + 1) * LANES)
            blk = q_buf[j, :, sl]
            sq = blk * blk
            tot = jnp.sum(sq, axis=-1, keepdims=True)
            ssn = jnp.sum(jnp.where(lo, sq, 0.0), axis=-1, keepdims=True)
            sc = jnp.where(lo, lax.rsqrt(ssn * (1.0 / A_NOPE) + EPS),
                           lax.rsqrt((tot - ssn) * (1.0 / A_ROPE) + EPS))
            qa_ref[rows, sl] = (_rope(blk, A_ROPE // 2, tabq_ref, rows) * sc).astype(BF16)
            kb_ = kn_buf[j, :, sl]
            ks = lax.rsqrt(jnp.sum(kb_ * kb_, axis=-1, keepdims=True) * (1.0 / A_NOPE) + EPS)
            ka_ref[rows, sl] = (kb_ * ks * gk_ref[:, sl] + krot).astype(BF16)
        for p in range(B_HEADS // 2):
            sl = slice(p * LANES, (p + 1) * LANES)
            qb_ref[rows, sl] = pair_norm(u_buf[j, :, 512 + p * LANES:512 + (p + 1) * LANES], gbq_ref[:, sl])
            kb_ref[rows, sl] = pair_norm(u_buf[j, :, 1024 + p * LANES:1024 + (p + 1) * LANES], gbk_ref[:, sl])


def _proj_even(h, gmix, win, wvbt, gcq, gckv, wuq, wuk, wuvt, gq3, gk, gkr3, gbq, gbk, cs):
    t = h.shape[0]
    tm = TM_PROJ
    row = lambda w: pl.BlockSpec((tm, w), lambda i: (i, 0))
    col = lambda w: pl.BlockSpec((w, tm), lambda i: (0, i))
    rshape = lambda w: jax.ShapeDtypeStruct((t, w), BF16)
    cshape = lambda w: jax.ShapeDtypeStruct((w, t), BF16)
    consts = (gmix, win, wvbt, gcq, gckv, wuq, wuk, wuvt, gq3, gk, gkr3, gbq, gbk)
    return pl.pallas_call(
        _proj_even_kernel,
        grid=(t // tm,),
        in_specs=[row(D_MODEL)] + [_const_spec(c.shape) for c in consts] + [row(cs.shape[1])],
        out_specs=[row(1024), row(1024), col(512), row(512), row(512), col(512)],
        out_shape=[rshape(1024), rshape(1024), cshape(512), rshape(512), rshape(512), cshape(512)],
        scratch_shapes=[pltpu.VMEM((3, tm, LANES), F32)] * 2
        + [pltpu.VMEM((PROJ_PARTS, tm // PROJ_PARTS, w), F32) for w in (win.shape[1], 1024, 1024)],
        compiler_params=_cparams(("parallel",)),
        name="proj_even",
    )(h, *consts, cs)


def _causal_scores(q_ref, k_ref, s_ref, c, *, split):
    tq = TQ_CAUSAL
    q = q_ref[c * tq:(c + 1) * tq, :]
    zero = jnp.zeros_like(q[:, :LANES])
    if split:
        q2 = jnp.concatenate([jnp.concatenate([q[:, :LANES], zero], axis=1),
                              jnp.concatenate([zero, q[:, LANES:]], axis=1)], axis=0)
    else:
        lo = _lane_iota() < 64
        q2 = jnp.concatenate([jnp.where(lo, q, zero), jnp.where(lo, zero, q)], axis=0)
    kchunk = lax.broadcasted_iota(jnp.int32, (tq, 2 * tq), 0) // CHUNK
    qchunk = (lax.broadcasted_iota(jnp.int32, (tq, 2 * tq), 1) % tq) // CHUNK
    n_past = c * tq
    s_d = jnp.where(kchunk <= qchunk, _dot_nt(k_ref[n_past:n_past + tq, :], q2), NEG)
    s_ref[n_past:n_past + tq, :] = s_d
    mx = jnp.max(s_d, axis=0, keepdims=True)
    if c:
        s_p = _dot_nt(k_ref[0:n_past, :], q2)
        s_ref[0:n_past, :] = s_p
        mx = jnp.maximum(mx, jnp.max(s_p, axis=0, keepdims=True))
    return mx


def _causal_values(vt_ref, s_ref, c, mx):
    n_keys = (c + 1) * TQ_CAUSAL
    p = jnp.exp2(s_ref[0:n_keys, :] - mx)
    o = jnp.dot(vt_ref[:, 0:n_keys], p.astype(BF16), preferred_element_type=F32)
    return o * (1.0 / jnp.sum(p, axis=0, keepdims=True))


def _causal_tiles(q_ref, k_ref, vt_ref, s_buf, *, split):
    tq = TQ_CAUSAL
    n = q_ref.shape[0] // tq
    mx = _causal_scores(q_ref, k_ref, s_buf.at[0], 0, split=split)
    for c in range(n):
        mx_next = None
        if c + 1 < n:
            mx_next = _causal_scores(q_ref, k_ref, s_buf.at[(c + 1) % 2], c + 1, split=split)
        o = _causal_values(vt_ref, s_buf.at[c % 2], c, mx)
        yield c, o[:, :tq], o[:, tq:]
        mx = mx_next


def _row_iota(n):
    return lax.broadcasted_iota(jnp.int32, (LANES, n), 0)


def _attn_mla_kernel(q_ref, k_ref, vt_ref, o_ref, s_buf):
    tq = TQ_CAUSAL
    for c, o0, o1 in _causal_tiles(q_ref, k_ref, vt_ref, s_buf, split=True):
        o_ref[c * tq:(c + 1) * tq, :] = jnp.where(_row_iota(tq) < 64, o0, o1).T.astype(o_ref.dtype)


def _attn_diff_kernel(q_ref, k_ref, vt_ref, lq1_ref, lk1_ref, lq2_ref, lk2_ref, gsub_ref, o_ref, s_buf,
                      *, lam_init):
    tq = TQ_CAUSAL
    lam = (jnp.exp(jnp.sum(lq1_ref[...] * lk1_ref[...], axis=-1, keepdims=True))
           - jnp.exp(jnp.sum(lq2_ref[...] * lk2_ref[...], axis=-1, keepdims=True)) + lam_init)
    for c, o0, o1 in _causal_tiles(q_ref, k_ref, vt_ref, s_buf, split=False):
        o = (o0 - lam * o1).T
        o = o * _rms_scale(o) * gsub_ref[...] * (1.0 - lam_init)
        o_ref[c * tq:(c + 1) * tq, :] = o.astype(o_ref.dtype)


def _attn_causal(kernel, q, k, vt, extras, *, batch, seq, n_blocks, qk_width):
    in_specs = [pl.BlockSpec((seq, qk_width), lambda b, p: (b, p)),
                pl.BlockSpec((seq, qk_width), lambda b, p: (b, p)),
                pl.BlockSpec((LANES, seq), lambda b, p: (p, b))]
    in_specs += [_const_spec(e.shape) for e in extras]
    return pl.pallas_call(
        kernel,
        grid=(batch, n_blocks),
        in_specs=in_specs,
        out_specs=pl.BlockSpec((seq, LANES), lambda b, p: (b, p)),
        out_shape=jax.ShapeDtypeStruct((batch * seq, n_blocks * LANES), BF16),
        scratch_shapes=[pltpu.VMEM((2, seq, 2 * TQ_CAUSAL), F32)],
        compiler_params=_cparams(("parallel", "parallel")),
        name=kernel.func.__name__ if isinstance(kernel, functools.partial) else kernel.__name__,
    )(q, k, vt, *extras)


def _attn_band_kernel(q_ref, k_ref, vt_ref, bias_ref, o_ref, s_buf):
    tq = TQ_BAND
    n = q_ref.shape[0] // tq
    lo = _lane_iota() < 64

    def window(c):
        end = (c + 1) * tq
        return max(0, end - BAND_KEYS), end

    def scores(c):
        start, end = window(c)
        q = q_ref[c * tq:(c + 1) * tq, :]
        zero = jnp.zeros_like(q)
        q2 = jnp.concatenate([jnp.where(lo, q, zero), jnp.where(lo, zero, q)], axis=0)
        s = _dot_nt(k_ref[start:end, :], q2) + bias_ref[BAND_KEYS - (end - start):, :]
        s_buf[c % 2, 0:end - start, :] = s
        return jnp.max(s, axis=0, keepdims=True)

    mx = scores(0)
    for c in range(n):
        mx_next = scores(c + 1) if c + 1 < n else None
        start, end = window(c)
        p = jnp.exp2(s_buf[c % 2, 0:end - start, :] - mx)
        o = jnp.dot(vt_ref[:, start:end], p.astype(BF16), preferred_element_type=F32)
        o = o * (1.0 / jnp.sum(p, axis=0, keepdims=True))
        o_ref[c * tq:(c + 1) * tq, :] = jnp.where(_row_iota(tq) < 64, o[:, :tq], o[:, tq:]).T.astype(o_ref.dtype)
        mx = mx_next


def _attn_band(q, k, vt, bias, *, batch, seq):
    npair = B_HEADS // 2
    return pl.pallas_call(
        _attn_band_kernel,
        grid=(batch, npair),
        in_specs=[pl.BlockSpec((seq, LANES), lambda b, p: (b, p)),
                  pl.BlockSpec((seq, LANES), lambda b, p: (b, p)),
                  pl.BlockSpec((LANES, seq), lambda b, p: (p, b)),
                  pl.BlockSpec((None, BAND_KEYS, 2 * TQ_BAND), lambda b, p: (p, 0, 0))],
        out_specs=pl.BlockSpec((seq, LANES), lambda b, p: (b, p)),
        out_shape=jax.ShapeDtypeStruct((batch * seq, npair * LANES), BF16),
        scratch_shapes=[pltpu.VMEM((2, BAND_KEYS, 2 * TQ_BAND), F32)],
        compiler_params=_cparams(("parallel", "parallel")),
        name="attn_band",
    )(q, k, vt, bias)


def _silu(g):
    return g / (1.0 + jnp.exp(-g))


def _out_ffn_kernel(h_ref, ya_ref, yb_ref, woa_ref, wob_ref, gff_ref, wg_ref, wu_ref, wd_ref, o_ref):
    h1 = (h_ref[...] + jnp.dot(ya_ref[...], woa_ref[...], preferred_element_type=F32)
          + jnp.dot(yb_ref[...], wob_ref[...], preferred_element_type=F32))
    xn = (h1 * _rms_scale(h1) * gff_ref[...]).astype(BF16)
    g = jnp.dot(xn, wg_ref[...], preferred_element_type=F32)
    u = jnp.dot(xn, wu_ref[...], preferred_element_type=F32)
    a = (_silu(g) * u).astype(BF16)
    o_ref[...] = h1 + jnp.dot(a, wd_ref[...], preferred_element_type=F32)


def _out_ffn(h, ya, yb, woa, wob, gff, wg, wu, wd):
    t = h.shape[0]
    tm = TM_PROJ
    row = lambda w: pl.BlockSpec((tm, w), lambda i: (i, 0))
    return pl.pallas_call(
        _out_ffn_kernel,
        grid=(t // tm,),
        in_specs=[row(D_MODEL), row(ya.shape[1]), row(yb.shape[1])]
        + [_const_spec(w.shape) for w in (woa, wob, gff, wg, wu, wd)],
        out_specs=row(D_MODEL),
        out_shape=jax.ShapeDtypeStruct((t, D_MODEL), F32),
        compiler_params=_cparams(("parallel",)),
        name="out_ffn",
    )(h, ya, yb, woa, wob, gff, wg, wu, wd)


def _proj_odd_kernel(h_ref, gmix_ref, w_ref, wvt_ref, gq3_ref, gk3_ref, cs_ref, q_ref, k_ref, vt_ref,
                     tabq_ref, tabk_ref, u_buf):
    n_part = u_buf.shape[0]
    rows_part = u_buf.shape[1]
    lo = _lane_iota() < 64
    width = C_HEADS * 2 * C_DH

    def project(j):
        rows = slice(j * rows_part, (j + 1) * rows_part)
        x = h_ref[rows, :]
        hn = (x * _rms_scale(x) * gmix_ref[...]).astype(BF16)
        u_buf[j] = jnp.dot(hn, w_ref[...], preferred_element_type=F32)
        vt_ref[:, rows] = _dot_nt(wvt_ref[...], hn).astype(BF16)

    project(0)
    tables = _rope_tables(cs_ref[...], C_ROPE // 2, (0, C_DH))
    _store_gained(tables, gq3_ref, tabq_ref)
    _store_gained(tables, gk3_ref, tabk_ref)

    def norm_rope(blk, tab_ref, rows):
        sq = blk * blk
        tot = jnp.sum(sq, axis=-1, keepdims=True)
        s_lo = jnp.sum(jnp.where(lo, sq, 0.0), axis=-1, keepdims=True)
        sc = jnp.where(lo, lax.rsqrt(s_lo * (1.0 / C_DH) + EPS),
                       lax.rsqrt((tot - s_lo) * (1.0 / C_DH) + EPS))
        return (_rope(blk, C_ROPE // 2, tab_ref, rows) * sc).astype(BF16)

    for j in range(n_part):
        if j + 1 < n_part:
            project(j + 1)
        rows = slice(j * rows_part, (j + 1) * rows_part)
        for hd in range(C_HEADS):
            sl = slice(hd * LANES, (hd + 1) * LANES)
            q_ref[rows, sl] = norm_rope(u_buf[j, :, sl], tabq_ref, rows)
            k_ref[rows, sl] = norm_rope(u_buf[j, :, width + hd * LANES:width + (hd + 1) * LANES],
                                        tabk_ref, rows)


def _proj_odd(h, gmix, w, wvt, gq3, gk3, cs):
    t = h.shape[0]
    tm = TM_PROJ
    row = lambda wd: pl.BlockSpec((tm, wd), lambda i: (i, 0))
    width = C_HEADS * 2 * C_DH
    consts = (gmix, w, wvt, gq3, gk3)
    return pl.pallas_call(
        _proj_odd_kernel,
        grid=(t // tm,),
        in_specs=[row(D_MODEL)] + [_const_spec(c.shape) for c in consts] + [row(cs.shape[1])],
        out_specs=[row(width), row(width), pl.BlockSpec((width, tm), lambda i: (0, i))],
        out_shape=[jax.ShapeDtypeStruct((t, width), BF16), jax.ShapeDtypeStruct((t, width), BF16),
                   jax.ShapeDtypeStruct((width, t), BF16)],
        scratch_shapes=[pltpu.VMEM((3, tm, LANES), F32)] * 2
        + [pltpu.VMEM((PROJ_PARTS, tm // PROJ_PARTS, 2 * width), F32)],
        compiler_params=_cparams(("parallel",)),
        name="proj_odd",
    )(h, *consts, cs)


def _out_router_kernel(h_ref, y_ref, wo_ref, gff_ref, wr_ref, h1_ref, idx_ref, gate_ref):
    h1 = h_ref[...] + jnp.dot(y_ref[...], wo_ref[...], preferred_element_type=F32)
    h1_ref[...] = h1
    xn = h1 * _rms_scale(h1) * gff_ref[...]
    logits = [jnp.sum(xn * wr_ref[e:e + 1, :], axis=-1, keepdims=True) for e in range(N_EXPERTS)]
    m1 = functools.reduce(jnp.maximum, logits)
    idx1 = functools.reduce(jnp.minimum,
                            [jnp.where(logits[e] == m1, e, N_EXPERTS) for e in range(N_EXPERTS)])
    rest = [jnp.where(idx1 == e, -jnp.inf, logits[e]) for e in range(N_EXPERTS)]
    m2 = functools.reduce(jnp.maximum, rest)
    idx2 = functools.reduce(jnp.minimum,
                            [jnp.where(rest[e] == m2, e, N_EXPERTS) for e in range(N_EXPERTS)])
    e2 = jnp.exp(m2 - m1)
    g1 = 1.0 / (1.0 + e2)
    g2 = e2 * g1
    lane = _lane_iota()
    idx_ref[...] = jnp.where(lane == 0, idx1, jnp.where(lane == 1, idx2, 0))
    gate_ref[...] = jnp.where(lane == 0, g1, jnp.where(lane == 1, g2, 0.0))


def _out_router(h, y, wo, gff, wr):
    t = h.shape[0]
    tm = TM_PROJ
    row = lambda w: pl.BlockSpec((tm, w), lambda i: (i, 0))
    return pl.pallas_call(
        _out_router_kernel,
        grid=(t // tm,),
        in_specs=[row(D_MODEL), row(y.shape[1]), _const_spec(wo.shape), _const_spec(gff.shape),
                  _const_spec(wr.shape)],
        out_specs=[row(D_MODEL), row(LANES), row(LANES)],
        out_shape=[jax.ShapeDtypeStruct((t, D_MODEL), F32),
                   jax.ShapeDtypeStruct((t, LANES), jnp.int32), jax.ShapeDtypeStruct((t, LANES), F32)],
        compiler_params=_cparams(("parallel",)),
        name="out_router",
    )(h, y, wo, gff, wr)


def _row_copy(src_ref, src_row, dst_ref, dst_row, sem):
    return pltpu.make_async_copy(src_ref.at[pl.ds(src_row, 1)], dst_ref.at[pl.ds(dst_row, 1)], sem)


def _dispatch_kernel(row_ref, x_ref, xs_ref, sem):
    tm = x_ref.shape[0]
    base = pl.program_id(0) * tm

    def issue(r, carry):
        for k in range(2):
            _row_copy(x_ref, r, xs_ref, row_ref[2 * (base + r) + k], sem).start(priority=k)
        return carry

    lax.fori_loop(0, tm, issue, 0, unroll=DMA_UNROLL)
    for k in range(2):
        pltpu.make_async_copy(x_ref, xs_ref.at[pl.ds(0, tm)], sem).wait()


def _dispatch(row_of_pair, x):
    t, d = x.shape
    tm = TM_DISPATCH
    grid_spec = pltpu.PrefetchScalarGridSpec(
        num_scalar_prefetch=1,
        grid=(t // tm,),
        in_specs=[pl.BlockSpec((tm, d), lambda i, rows: (i, 0))],
        out_specs=pl.BlockSpec(memory_space=pl.ANY),
        scratch_shapes=[pltpu.SemaphoreType.DMA(())],
    )
    return pl.pallas_call(
        _dispatch_kernel,
        grid_spec=grid_spec,
        out_shape=jax.ShapeDtypeStruct((2 * t, d), x.dtype),
        compiler_params=_cparams(("arbitrary",)),
        name="moe_dispatch",
    )(row_of_pair, x)


def _moe_kernel(vt_ref, ve_ref, lo_ref, hi_ref, first_ref, x_ref, gff_ref, wg_ref, wu_ref, wd_ref, o_ref,
                xb_ref):
    v = pl.program_id(0)
    c = pl.program_id(1)
    tr = x_ref.shape[0]

    @pl.when((c == 0) & (first_ref[v] == 1))
    def _():
        o_ref[...] = jnp.zeros_like(o_ref)

    @pl.when(c == 0)
    def _():
        x = x_ref[...]
        xb_ref[...] = (x * _rms_scale(x) * gff_ref[...]).astype(BF16)

    lo = lo_ref[v]
    hi = hi_ref[v]
    base = vt_ref[v] * tr

    def expert_rows(off, n, whole):
        rows = slice(off, off + n)
        x = xb_ref[rows, :]
        g = jnp.dot(x, wg_ref[0].astype(BF16), preferred_element_type=F32)
        u = jnp.dot(x, wu_ref[0].astype(BF16), preferred_element_type=F32)
        a = (_silu(g) * u).astype(BF16)
        y = jnp.dot(a, wd_ref[0].astype(BF16), preferred_element_type=F32)
        if not whole:
            r = base + off + lax.broadcasted_iota(jnp.int32, (n, 1), 0)
            y = jnp.where((r >= lo) & (r < hi), y, 0.0)
        o_ref[rows, :] += y

    whole_tile = (lo == base) & (hi == base + tr)

    @pl.when(whole_tile)
    def _():
        expert_rows(0, tr, True)

    for s in range(tr // MOE_SUB):
        @pl.when(jnp.logical_not(whole_tile) & (base + s * MOE_SUB < hi) & (base + (s + 1) * MOE_SUB > lo))
        def _(s=s):
            expert_rows(s * MOE_SUB, MOE_SUB, False)


def _moe(visits, xs, gff, wg, wu, wd):
    r = xs.shape[0]
    tr, fc = TR_MOE, FC_MOE
    nc = FF_EXPERT // fc
    n_visits = r // tr + N_EXPERTS - 1

    def chunk(v, c, lo, hi):
        return jnp.where(lo[v] < hi[v], c, nc - 1)

    tile = lambda v, c, vt, ve, lo, hi, first: (vt[v], 0)
    wcol = lambda v, c, vt, ve, lo, hi, first: (ve[v], 0, chunk(v, c, lo, hi))
    wrow = lambda v, c, vt, ve, lo, hi, first: (ve[v], chunk(v, c, lo, hi), 0)
    grid_spec = pltpu.PrefetchScalarGridSpec(
        num_scalar_prefetch=5,
        grid=(n_visits, nc),
        in_specs=[pl.BlockSpec((tr, D_MODEL), tile),
                  pl.BlockSpec((1, D_MODEL), lambda v, c, vt, ve, lo, hi, first: (0, 0)),
                  pl.BlockSpec((1, D_MODEL, fc), wcol),
                  pl.BlockSpec((1, D_MODEL, fc), wcol),
                  pl.BlockSpec((1, fc, D_MODEL), wrow)],
        out_specs=pl.BlockSpec((tr, D_MODEL), tile),
        scratch_shapes=[pltpu.VMEM((tr, D_MODEL), BF16)],
    )
    return pl.pallas_call(
        _moe_kernel,
        grid_spec=grid_spec,
        out_shape=jax.ShapeDtypeStruct((r, D_MODEL), F32),
        compiler_params=_cparams(("arbitrary", "arbitrary")),
        name="moe_grouped",
    )(*visits, xs, gff, wg, wu, wd)


def _combine_kernel(row_ref, h_ref, gate_ref, ys_ref, o_ref, buf, sem):
    tm = h_ref.shape[0]
    base = pl.program_id(0) * tm

    def issue(r, carry):
        for k in range(2):
            _row_copy(ys_ref, row_ref[2 * (base + r) + k], buf, k * tm + r, sem).start(priority=k)
        return carry

    lax.fori_loop(0, tm, issue, 0, unroll=DMA_UNROLL)
    pltpu.make_async_copy(ys_ref.at[pl.ds(0, 2 * tm)], buf, sem).wait()
    gate = gate_ref[...]
    o_ref[...] = h_ref[...] + gate[:, 0:1] * buf[0:tm, :] + gate[:, 1:2] * buf[tm:2 * tm, :]


def _combine(row_of_pair, h1, gates, ys):
    t, d = h1.shape
    tm = TM_DISPATCH
    grid_spec = pltpu.PrefetchScalarGridSpec(
        num_scalar_prefetch=1,
        grid=(t // tm,),
        in_specs=[pl.BlockSpec((tm, d), lambda i, rows: (i, 0)),
                  pl.BlockSpec((tm, LANES), lambda i, rows: (i, 0)),
                  pl.BlockSpec(memory_space=pl.ANY)],
        out_specs=pl.BlockSpec((tm, d), lambda i, rows: (i, 0)),
        scratch_shapes=[pltpu.VMEM((2 * tm, d), F32), pltpu.SemaphoreType.DMA(())],
    )
    return pl.pallas_call(
        _combine_kernel,
        grid_spec=grid_spec,
        out_shape=jax.ShapeDtypeStruct((t, d), F32),
        compiler_params=_cparams(("arbitrary",)),
        name="moe_combine",
    )(row_of_pair, h1, gates, ys)


def _route_plan(idx):
    n_pairs = idx.size
    tr = TR_MOE
    n_tiles = n_pairs // tr
    n_visits = n_tiles + N_EXPERTS - 1
    e_flat = idx.reshape(-1)
    onehot = (e_flat[:, None] == jnp.arange(N_EXPERTS, dtype=jnp.int32)[None, :]).astype(jnp.int32)
    csum = jnp.cumsum(onehot, axis=0)
    counts = csum[-1]
    ends = jnp.cumsum(counts)
    starts = ends - counts
    pos = jnp.sum((csum - onehot) * onehot, axis=1)
    row_of_pair = (starts[e_flat] + pos).astype(jnp.int32)
    nv = jnp.where(counts > 0, (ends - 1) // tr - starts // tr + 1, 0)
    vend = jnp.cumsum(nv)
    vid = jnp.arange(n_visits, dtype=jnp.int32)
    live = vid < vend[-1]
    ve = jnp.minimum(jnp.sum((vid[:, None] >= vend[None, :]).astype(jnp.int32), axis=1), N_EXPERTS - 1)
    vt = jnp.where(live, starts[ve] // tr + vid - (vend - nv)[ve], n_tiles - 1)
    lo = jnp.where(live, jnp.maximum(starts[ve], vt * tr), 0)
    hi = jnp.where(live, jnp.minimum(ends[ve], (vt + 1) * tr), 0)
    last_e = jnp.max(jnp.where(counts > 0, jnp.arange(N_EXPERTS, dtype=jnp.int32), 0))
    ve = jnp.where(live, ve, last_e)
    first = (live & (lo == vt * tr)).astype(jnp.int32)
    i32 = lambda a: a.astype(jnp.int32)
    return row_of_pair, (i32(vt), i32(ve), i32(lo), i32(hi), first)


def _cos_sin(positions, dim, theta):
    inv = 1.0 / (theta ** (jnp.arange(0, dim, 2, dtype=F32) / dim))
    ang = positions.astype(F32).reshape(-1, 1) * inv
    return jnp.concatenate([jnp.cos(ang), jnp.sin(ang)], axis=1)


def _gain3(g, half):
    return jnp.stack([g, jnp.roll(g, -half), jnp.roll(g, half)]).astype(F32)


def _band_bias(rel_table):
    nh = rel_table.shape[0]
    nk, nq = BAND_KEYS, TQ_BAND
    length = BAND_PROFILE
    n_lo = nk - 1 - B_LEFT_CHUNKS * CHUNK - B_REL_CLIP
    n_hi = length - n_lo - rel_table.shape[1]
    prof = jnp.concatenate([jnp.broadcast_to(rel_table[:, :1], (nh, n_lo)), rel_table,
                            jnp.broadcast_to(rel_table[:, -1:], (nh, n_hi))], axis=1)
    return pl.pallas_call(
        _band_bias_kernel,
        grid=(nh,),
        in_specs=[pl.BlockSpec((None, 1, length), lambda h: (h, 0, 0))],
        out_specs=pl.BlockSpec((None, nk, nq), lambda h: (h // 2, 0, h % 2)),
        out_shape=jax.ShapeDtypeStruct((nh // 2, nk, 2 * nq), F32),
        compiler_params=_cparams(("parallel",)),
        name="band_bias",
    )(prof.astype(F32).reshape(nh, 1, length))


def _band_bias_kernel(p_ref, o_ref):
    nk, nq = o_ref.shape
    length = p_ref.shape[1]
    x = jnp.broadcast_to(p_ref[...] * LOG2E, (8, length))
    sub = lax.broadcasted_iota(jnp.int32, (8, nq), 0)
    qchunk = lax.broadcasted_iota(jnp.int32, (8, nq), 1) // CHUNK

    def body(jb, carry):
        rolled = pltpu.roll(x, jb * 8 + (length - (nk - 1)), 1, stride=1, stride_axis=0)
        cdiff = qchunk + B_LEFT_CHUNKS - (jb * 8 + sub) // CHUNK
        allowed = (cdiff >= 0) & (cdiff <= B_LEFT_CHUNKS)
        o_ref[pl.ds(pl.multiple_of(jb * 8, 8), 8), :] = jnp.where(allowed, rolled[:, :nq], NEG)
        return carry

    lax.fori_loop(0, nk // 8, body, 0, unroll=8)


def _pad_last(a, width):
    return jnp.pad(a, [(0, 0)] * (a.ndim - 1) + [(0, width - a.shape[-1])])


def kernel(x, positions, ev_norm_mix, ev_w_in, ev_g_cq, ev_g_ckv, ev_w_uq, ev_w_ukv, ev_a_qnorm, ev_a_knorm, ev_b_qnorm, ev_b_knorm, ev_b_rel_bias, ev_w_out, ev_norm_ff, ev_ff_gate, ev_ff_up, ev_ff_down, od_norm_mix, od_w_qkv, od_c_qnorm, od_c_knorm, od_lam_q1, od_lam_k1, od_lam_q2, od_lam_k2, od_c_subnorm, od_w_out, od_norm_ff, od_router, od_ex_gate, od_ex_up, od_ex_down):
    batch, seq, d = x.shape
    t = batch * seq
    depth = ev_w_in.shape[0] + od_w_qkv.shape[0]
    h = x.reshape(t, d)
    row = lambda a: a.reshape(1, -1).astype(F32)
    cs_a = _cos_sin(positions, A_ROPE, A_THETA)
    cs_c = _cos_sin(positions, C_ROPE, ROPE_THETA)

    for layer in range(depth):
        i = layer // 2
        if layer % 2 == 0:
            w = ev_w_in[i]
            o1, o2, o3 = A_Q_LORA, A_Q_LORA + A_KV_LORA, A_Q_LORA + A_KV_LORA + A_ROPE
            zc = lambda n: jnp.zeros((d, n), F32)
            bw = B_HEADS * B_DH
            win = jnp.concatenate([w[:, :o2], zc(64), w[:, o2:o3], zc(32), w[:, o3:o3 + 2 * bw]], 1).astype(BF16)
            wvbt = w[:, o3 + 2 * bw:].T.astype(BF16)
            wuq = _pad_last(ev_w_uq[i], LANES).reshape(A_Q_LORA, A_HEADS * LANES).astype(BF16)
            wuk = _pad_last(ev_w_ukv[i][:, :, :A_NOPE], LANES).reshape(A_KV_LORA, A_HEADS * LANES).astype(BF16)
            wuvt = ev_w_ukv[i][:, :, A_NOPE:].reshape(A_KV_LORA, A_HEADS * A_V).T.astype(BF16)
            sc_a = (A_NOPE + A_ROPE) ** -0.5 * LOG2E
            gq3 = _gain3(_pad_last(ev_a_qnorm[i], LANES) * sc_a, A_ROPE // 2)
            gk = jnp.tile(_pad_last(ev_a_knorm[i][:A_NOPE], LANES), A_HEADS).reshape(1, -1)
            gkr3 = _gain3(jnp.concatenate([jnp.zeros((A_NOPE,), F32), ev_a_knorm[i][A_NOPE:],
                                           jnp.zeros((32,), F32)]), A_ROPE // 2)
            gbq = jnp.tile(ev_b_qnorm[i], B_HEADS).reshape(1, -1) * (B_DH ** -0.5 * LOG2E)
            gbk = jnp.tile(ev_b_knorm[i], B_HEADS).reshape(1, -1)
            qa, ka, vat, qb, kb, vbt = _proj_even(
                h, row(ev_norm_mix[i]), win, wvbt, row(ev_g_cq[i]), row(ev_g_ckv[i]), wuq, wuk, wuvt,
                gq3, gk, gkr3, gbq, gbk, cs_a)
            ya = _attn_causal(_attn_mla_kernel, qa, ka, vat, (), batch=batch, seq=seq,
                              n_blocks=A_HEADS // 2, qk_width=2 * LANES)
            yb = _attn_band(qb, kb, vbt, _band_bias(ev_b_rel_bias[i]), batch=batch, seq=seq)
            wo = ev_w_out[i].astype(BF16)
            na = A_HEADS * A_V
            padf = lambda a: _pad_last(a, FF_DENSE_PAD).astype(BF16)
            wd = jnp.pad(ev_ff_down[i], ((0, FF_DENSE_PAD - FF_DENSE), (0, 0))).astype(BF16)
            h = _out_ffn(h, ya, yb, wo[:na], wo[na:], row(ev_norm_ff[i]),
                         padf(ev_ff_gate[i]), padf(ev_ff_up[i]), wd)
        else:
            lam_init = 0.8 - 0.6 * math.exp(-0.3 * layer)
            gq3 = _gain3(jnp.tile(od_c_qnorm[i], 2) * (C_DH ** -0.5 * LOG2E), C_ROPE // 2)
            gk3 = _gain3(jnp.tile(od_c_knorm[i], 2), C_ROPE // 2)
            c_mix = C_HEADS * 2 * C_DH
            wqkv = od_w_qkv[i]
            q, k, v = _proj_odd(h, row(od_norm_mix[i]), wqkv[:, :2 * c_mix].astype(BF16),
                                wqkv[:, 2 * c_mix:].T.astype(BF16), gq3, gk3, cs_c)
            extras = (row(od_lam_q1[i]), row(od_lam_k1[i]), row(od_lam_q2[i]), row(od_lam_k2[i]),
                      row(od_c_subnorm[i]))
            yc = _attn_causal(functools.partial(_attn_diff_kernel, lam_init=lam_init), q, k, v, extras,
                              batch=batch, seq=seq, n_blocks=C_HEADS, qk_width=LANES)
            h1, idx, gates = _out_router(h, yc, od_w_out[i].astype(BF16), row(od_norm_ff[i]),
                                         od_router[i].T.astype(F32))
            row_of_pair, visits = _route_plan(idx[:, :2])
            xs = _dispatch(row_of_pair, h1)
            ys = _moe(visits, xs, row(od_norm_ff[i]), od_ex_gate[i], od_ex_up[i], od_ex_down[i])
            h = _combine(row_of_pair, h1, gates, ys)
    return h.reshape(batch, seq, d)
```

```python
import functools
import math

import numpy as np
import jax
import jax.numpy as jnp
from jax import lax
from jax.experimental import pallas as pl
from jax.experimental.pallas import tpu as pltpu

F32 = jnp.float32
BF16 = jnp.bfloat16

D_MODEL = 1024
CHUNK = 64
EPS = 1e-6
LOG2E = 1.4426950408889634
NEG = -1e30

A_HEADS = 8
A_NOPE = 64
A_ROPE = 32
A_V = 64
A_Q_LORA = 256
A_KV_LORA = 128
A_THETA = 10000.0
B_HEADS = 8
B_DH = 64
B_LEFT_CHUNKS = 8
B_REL_CLIP = 128
C_HEADS = 8
C_DH = 64
C_ROPE = C_DH // 4
ROPE_THETA = 500000.0
FF_DENSE = 2752
N_EXPERTS = 8
FF_EXPERT = 3584

LANES = 128
ROW_TILES = D_MODEL // LANES
FF_DENSE_PAD = 2816
VMEM_LIMIT = 56 * 1024 * 1024

TM_PROJ = 512
PROJ_PARTS = 2
TQ_CAUSAL = 256
TQ_BAND = 256
BAND_KEYS = TQ_BAND + B_LEFT_CHUNKS * CHUNK
BAND_PROFILE = 1024
TM_DISPATCH = 512
DMA_UNROLL = 16
TR_MOE = 1024
MOE_SUB = 256
FC_MOE = 512


def _cparams(sem):
    return pltpu.CompilerParams(dimension_semantics=sem, vmem_limit_bytes=VMEM_LIMIT)


def _const_spec(shape):
    nd = len(shape)
    return pl.BlockSpec(shape, lambda *_: (0,) * nd, pipeline_mode=pl.Buffered(1))


def _rms_scale(x):
    return lax.rsqrt(jnp.mean(x * x, axis=-1, keepdims=True) + EPS)


def _lane_iota():
    return lax.broadcasted_iota(jnp.int32, (1, LANES), 1)


def _dot_nt(a, b):
    return lax.dot_general(a, b, (((1,), (1,)), ((), ())), preferred_element_type=F32)


def _rope_tables(cs, half, starts):
    tm = cs.shape[0]
    base = jnp.concatenate([cs, jnp.zeros((tm, LANES - 2 * half), F32)], axis=1)
    shifted = lambda s: base if s % LANES == 0 else pltpu.roll(base, s % LANES, 1)
    lane = _lane_iota()
    cos_t = jnp.ones((tm, LANES), F32)
    sin_a = jnp.zeros((tm, LANES), F32)
    sin_b = jnp.zeros((tm, LANES), F32)
    for s0 in starts:
        first = (lane >= s0) & (lane < s0 + half)
        second = (lane >= s0 + half) & (lane < s0 + 2 * half)
        at_s0 = shifted(s0)
        cos_t = jnp.where(first, at_s0, jnp.where(second, shifted(s0 + half), cos_t))
        sin_a = jnp.where(first, -shifted(s0 - half), sin_a)
        sin_b = jnp.where(second, at_s0, sin_b)
    return cos_t, sin_a, sin_b


def _store_gained(tables, g3_ref, tab_ref):
    for j in range(3):
        tab_ref[j] = tables[j] * g3_ref[j:j + 1, :]


def _rope(x, half, tab_ref, rows=slice(None)):
    return (x * tab_ref[0, rows, :] + pltpu.roll(x, LANES - half, 1) * tab_ref[1, rows, :]
            + pltpu.roll(x, half, 1) * tab_ref[2, rows, :])


def _proj_even_kernel(h_ref, gmix_ref, win_ref, wvbt_ref, gcq_ref, gckv_ref, wuq_ref, wuk_ref, wuvt_ref,
                      gq3_ref, gk_ref, gkr3_ref, gbq_ref, gbk_ref, cs_ref,
                      qa_ref, ka_ref, vat_ref, qb_ref, kb_ref, vbt_ref, tabq_ref, tabk_ref,
                      u_buf, q_buf, kn_buf):
    n_part = u_buf.shape[0]
    rows_part = u_buf.shape[1]
    lo = _lane_iota() < 64

    def project(j):
        rows = slice(j * rows_part, (j + 1) * rows_part)
        x = h_ref[rows, :]
        hn = (x * _rms_scale(x) * gmix_ref[...]).astype(BF16)
        u = jnp.dot(hn, win_ref[...], preferred_element_type=F32)
        u_buf[j] = u
        vbt_ref[:, rows] = _dot_nt(wvbt_ref[...], hn).astype(BF16)
        cq = u[:, 0:A_Q_LORA]
        cqn = (cq * _rms_scale(cq) * gcq_ref[...]).astype(BF16)
        q_buf[j] = jnp.dot(cqn, wuq_ref[...], preferred_element_type=F32)
        ckv = u[:, A_Q_LORA:A_Q_LORA + A_KV_LORA]
        ckvn = (ckv * _rms_scale(ckv) * gckv_ref[...]).astype(BF16)
        kn_buf[j] = jnp.dot(ckvn, wuk_ref[...], preferred_element_type=F32)
        vat_ref[:, rows] = _dot_nt(wuvt_ref[...], ckvn).astype(BF16)

    project(0)
    tables = _rope_tables(cs_ref[...], A_ROPE // 2, (A_NOPE,))
    _store_gained(tables, gq3_ref, tabq_ref)
    _store_gained(tables, gkr3_ref, tabk_ref)

    def pair_norm(blk, g):
        sq = blk * blk
        tot = jnp.sum(sq, axis=-1, keepdims=True)
        s_lo = jnp.sum(jnp.where(lo, sq, 0.0), axis=-1, keepdims=True)
        sc = jnp.where(lo, lax.rsqrt(s_lo * (1.0 / B_DH) + EPS),
                       lax.rsqrt((tot - s_lo) * (1.0 / B_DH) + EPS))
        return (blk * sc * g).astype(BF16)

    for j in range(n_part):
        if j + 1 < n_part:
            project(j + 1)
        rows = slice(j * rows_part, (j + 1) * rows_part)
        kr = u_buf[j, :, 384:512]
        kr_s = lax.rsqrt(jnp.sum(kr * kr, axis=-1, keepdims=True) * (1.0 / A_ROPE) + EPS)
        krot = _rope(kr, A_ROPE // 2, tabk_ref, rows) * kr_s
        for hd in range(A_HEADS):
            sl = slice(hd * LANES, (hd + 1) * LANES)
            blk = q_buf[j, :, sl]
            sq = blk * blk
            tot = jnp.sum(sq, axis=-1, keepdims=True)
            ssn = jnp.sum(jnp.where(lo, sq, 0.0), axis=-1, keepdims=True)
            sc = jnp.where(lo, lax.rsqrt(ssn * (1.0 / A_NOPE) + EPS),
                           lax.rsqrt((tot - ssn) * (1.0 / A_ROPE) + EPS))
            qa_ref[rows, sl] = (_rope(blk, A_ROPE // 2, tabq_ref, rows) * sc).astype(BF16)
            kb_ = kn_buf[j, :, sl]
            ks = lax.rsqrt(jnp.sum(kb_ * kb_, axis=-1, keepdims=True) * (1.0 / A_NOPE) + EPS)
            ka_ref[rows, sl] = (kb_ * ks * gk_ref[:, sl] + krot).astype(BF16)
        for p in range(B_HEADS // 2):
            sl = slice(p * LANES, (p + 1) * LANES)
            qb_ref[rows, sl] = pair_norm(u_buf[j, :, 512 + p * LANES:512 + (p + 1) * LANES], gbq_ref[:, sl])
            kb_ref[rows, sl] = pair_norm(u_buf[j, :, 1024 + p * LANES:1024 + (p + 1) * LANES], gbk_ref[:, sl])


def _proj_even(h, gmix, win, wvbt, gcq, gckv, wuq, wuk, wuvt, gq3, gk, gkr3, gbq, gbk, cs):
    t = h.shape[0]
    tm = TM_PROJ
    row = lambda w: pl.BlockSpec((tm, w), lambda i: (i, 0))
    col = lambda w: pl.BlockSpec((w, tm), lambda i: (0, i))
    rshape = lambda w: jax.ShapeDtypeStruct((t, w), BF16)
    cshape = lambda w: jax.ShapeDtypeStruct((w, t), BF16)
    consts = (gmix, win, wvbt, gcq, gckv, wuq, wuk, wuvt, gq3, gk, gkr3, gbq, gbk)
    return pl.pallas_call(
        _proj_even_kernel,
        grid=(t // tm,),
        in_specs=[row(D_MODEL)] + [_const_spec(c.shape) for c in consts] + [row(cs.shape[1])],
        out_specs=[row(1024), row(1024), col(512), row(512), row(512), col(512)],
        out_shape=[rshape(1024), rshape(1024), cshape(512), rshape(512), rshape(512), cshape(512)],
        scratch_shapes=[pltpu.VMEM((3, tm, LANES), F32)] * 2
        + [pltpu.VMEM((PROJ_PARTS, tm // PROJ_PARTS, w), F32) for w in (win.shape[1], 1024, 1024)],
        compiler_params=_cparams(("parallel",)),
        name="proj_even",
    )(h, *consts, cs)


def _causal_scores(q_ref, k_ref, s_ref, c, *, split):
    tq = TQ_CAUSAL
    q = q_ref[c * tq:(c + 1) * tq, :]
    zero = jnp.zeros_like(q[:, :LANES])
    if split:
        q2 = jnp.concatenate([jnp.concatenate([q[:, :LANES], zero], axis=1),
                              jnp.concatenate([zero, q[:, LANES:]], axis=1)], axis=0)
    else:
        lo = _lane_iota() < 64
        q2 = jnp.concatenate([jnp.where(lo, q, zero), jnp.where(lo, zero, q)], axis=0)
    kchunk = lax.broadcasted_iota(jnp.int32, (tq, 2 * tq), 0) // CHUNK
    qchunk = (lax.broadcasted_iota(jnp.int32, (tq, 2 * tq), 1) % tq) // CHUNK
    n_past = c * tq
    s_d = jnp.where(kchunk <= qchunk, _dot_nt(k_ref[n_past:n_past + tq, :], q2), NEG)
    s_ref[n_past:n_past + tq, :] = s_d
    mx = jnp.max(s_d, axis=0, keepdims=True)
    if c:
        s_p = _dot_nt(k_ref[0:n_past, :], q2)
        s_ref[0:n_past, :] = s_p
        mx = jnp.maximum(mx, jnp.max(s_p, axis=0, keepdims=True))
    return mx


def _causal_values(vt_ref, s_ref, c, mx):
    n_keys = (c + 1) * TQ_CAUSAL
    p = jnp.exp2(s_ref[0:n_keys, :] - mx)
    o = jnp.dot(vt_ref[:, 0:n_keys], p.astype(BF16), preferred_element_type=F32)
    return o * (1.0 / jnp.sum(p, axis=0, keepdims=True))


def _causal_tiles(q_ref, k_ref, vt_ref, s_buf, *, split):
    tq = TQ_CAUSAL
    n = q_ref.shape[0] // tq
    mx = _causal_scores(q_ref, k_ref, s_buf.at[0], 0, split=split)
    for c in range(n):
        mx_next = None
        if c + 1 < n:
            mx_next = _causal_scores(q_ref, k_ref, s_buf.at[(c + 1) % 2], c + 1, split=split)
        o = _causal_values(vt_ref, s_buf.at[c % 2], c, mx)
        yield c, o[:, :tq], o[:, tq:]
        mx = mx_next


def _row_iota(n):
    return lax.broadcasted_iota(jnp.int32, (LANES, n), 0)


def _attn_mla_kernel(q_ref, k_ref, vt_ref, o_ref, s_buf):
    tq = TQ_CAUSAL
    for c, o0, o1 in _causal_tiles(q_ref, k_ref, vt_ref, s_buf, split=True):
        o_ref[c * tq:(c + 1) * tq, :] = jnp.where(_row_iota(tq) < 64, o0, o1).T.astype(o_ref.dtype)


def _attn_diff_kernel(q_ref, k_ref, vt_ref, lq1_ref, lk1_ref, lq2_ref, lk2_ref, gsub_ref, o_ref, s_buf,
                      *, lam_init):
    tq = TQ_CAUSAL
    lam = (jnp.exp(jnp.sum(lq1_ref[...] * lk1_ref[...], axis=-1, keepdims=True))
           - jnp.exp(jnp.sum(lq2_ref[...] * lk2_ref[...], axis=-1, keepdims=True)) + lam_init)
    for c, o0, o1 in _causal_tiles(q_ref, k_ref, vt_ref, s_buf, split=False):
        o = (o0 - lam * o1).T
        o = o * _rms_scale(o) * gsub_ref[...] * (1.0 - lam_init)
        o_ref[c * tq:(c + 1) * tq, :] = o.astype(o_ref.dtype)


def _attn_causal(kernel, q, k, vt, extras, *, batch, seq, n_blocks, qk_width):
    in_specs = [pl.BlockSpec((seq, qk_width), lambda b, p: (b, p)),
                pl.BlockSpec((seq, qk_width), lambda b, p: (b, p)),
                pl.BlockSpec((LANES, seq), lambda b, p: (p, b))]
    in_specs += [_const_spec(e.shape) for e in extras]
    return pl.pallas_call(
        kernel,
        grid=(batch, n_blocks),
        in_specs=in_specs,
        out_specs=pl.BlockSpec((seq, LANES), lambda b, p: (b, p)),
        out_shape=jax.ShapeDtypeStruct((batch * seq, n_blocks * LANES), BF16),
        scratch_shapes=[pltpu.VMEM((2, seq, 2 * TQ_CAUSAL), F32)],
        compiler_params=_cparams(("parallel", "parallel")),
        name=kernel.func.__name__ if isinstance(kernel, functools.partial) else kernel.__name__,
    )(q, k, vt, *extras)


def _attn_band_kernel(q_ref, k_ref, vt_ref, bias_ref, o_ref, s_buf):
    tq = TQ_BAND
    n = q_ref.shape[0] // tq
    lo = _lane_iota() < 64

    def window(c):
        end = (c + 1) * tq
        return max(0, end - BAND_KEYS), end

    def scores(c):
        start, end = window(c)
        q = q_ref[c * tq:(c + 1) * tq, :]
        zero = jnp.zeros_like(q)
        q2 = jnp.concatenate([jnp.where(lo, q, zero), jnp.where(lo, zero, q)], axis=0)
        s = _dot_nt(k_ref[start:end, :], q2) + bias_ref[BAND_KEYS - (end - start):, :]
        s_buf[c % 2, 0:end - start, :] = s
        return jnp.max(s, axis=0, keepdims=True)

    mx = scores(0)
    for c in range(n):
        mx_next = scores(c + 1) if c + 1 < n else None
        start, end = window(c)
        p = jnp.exp2(s_buf[c % 2, 0:end - start, :] - mx)
        o = jnp.dot(vt_ref[:, start:end], p.astype(BF16), preferred_element_type=F32)
        o = o * (1.0 / jnp.sum(p, axis=0, keepdims=True))
        o_ref[c * tq:(c + 1) * tq, :] = jnp.where(_row_iota(tq) < 64, o[:, :tq], o[:, tq:]).T.astype(o_ref.dtype)
        mx = mx_next


def _attn_band(q, k, vt, bias, *, batch, seq):
    npair = B_HEADS // 2
    return pl.pallas_call(
        _attn_band_kernel,
        grid=(batch, npair),
        in_specs=[pl.BlockSpec((seq, LANES), lambda b, p: (b, p)),
                  pl.BlockSpec((seq, LANES), lambda b, p: (b, p)),
                  pl.BlockSpec((LANES, seq), lambda b, p: (p, b)),
                  pl.BlockSpec((None, BAND_KEYS, 2 * TQ_BAND), lambda b, p: (p, 0, 0))],
        out_specs=pl.BlockSpec((seq, LANES), lambda b, p: (b, p)),
        out_shape=jax.ShapeDtypeStruct((batch * seq, npair * LANES), BF16),
        scratch_shapes=[pltpu.VMEM((2, BAND_KEYS, 2 * TQ_BAND), F32)],
        compiler_params=_cparams(("parallel", "parallel")),
        name="attn_band",
    )(q, k, vt, bias)


def _silu(g):
    return g / (1.0 + jnp.exp(-g))


def _out_ffn_kernel(h_ref, ya_ref, yb_ref, woa_ref, wob_ref, gff_ref, wg_ref, wu_ref, wd_ref, o_ref):
    h1 = (h_ref[...] + jnp.dot(ya_ref[...], woa_ref[...], preferred_element_type=F32)
          + jnp.dot(yb_ref[...], wob_ref[...], preferred_element_type=F32))
    xn = (h1 * _rms_scale(h1) * gff_ref[...]).astype(BF16)
    g = jnp.dot(xn, wg_ref[...], preferred_element_type=F32)
    u = jnp.dot(xn, wu_ref[...], preferred_element_type=F32)
    a = (_silu(g) * u).astype(BF16)
    o_ref[...] = h1 + jnp.dot(a, wd_ref[...], preferred_element_type=F32)


def _out_ffn(h, ya, yb, woa, wob, gff, wg, wu, wd):
    t = h.shape[0]
    tm = TM_PROJ
    row = lambda w: pl.BlockSpec((tm, w), lambda i: (i, 0))
    return pl.pallas_call(
        _out_ffn_kernel,
        grid=(t // tm,),
        in_specs=[row(D_MODEL), row(ya.shape[1]), row(yb.shape[1])]
        + [_const_spec(w.shape) for w in (woa, wob, gff, wg, wu, wd)],
        out_specs=row(D_MODEL),
        out_shape=jax.ShapeDtypeStruct((t, D_MODEL), F32),
        compiler_params=_cparams(("parallel",)),
        name="out_ffn",
    )(h, ya, yb, woa, wob, gff, wg, wu, wd)


def _proj_odd_kernel(h_ref, gmix_ref, w_ref, wvt_ref, gq3_ref, gk3_ref, cs_ref, q_ref, k_ref, vt_ref,
                     tabq_ref, tabk_ref, u_buf):
    n_part = u_buf.shape[0]
    rows_part = u_buf.shape[1]
    lo = _lane_iota() < 64
    width = C_HEADS * 2 * C_DH

    def project(j):
        rows = slice(j * rows_part, (j + 1) * rows_part)
        x = h_ref[rows, :]
        hn = (x * _rms_scale(x) * gmix_ref[...]).astype(BF16)
        u_buf[j] = jnp.dot(hn, w_ref[...], preferred_element_type=F32)
        vt_ref[:, rows] = _dot_nt(wvt_ref[...], hn).astype(BF16)

    project(0)
    tables = _rope_tables(cs_ref[...], C_ROPE // 2, (0, C_DH))
    _store_gained(tables, gq3_ref, tabq_ref)
    _store_gained(tables, gk3_ref, tabk_ref)

    def norm_rope(blk, tab_ref, rows):
        sq = blk * blk
        tot = jnp.sum(sq, axis=-1, keepdims=True)
        s_lo = jnp.sum(jnp.where(lo, sq, 0.0), axis=-1, keepdims=True)
        sc = jnp.where(lo, lax.rsqrt(s_lo * (1.0 / C_DH) + EPS),
                       lax.rsqrt((tot - s_lo) * (1.0 / C_DH) + EPS))
        return (_rope(blk, C_ROPE // 2, tab_ref, rows) * sc).astype(BF16)

    for j in range(n_part):
        if j + 1 < n_part:
            project(j + 1)
        rows = slice(j * rows_part, (j + 1) * rows_part)
        for hd in range(C_HEADS):
            sl = slice(hd * LANES, (hd + 1) * LANES)
            q_ref[rows, sl] = norm_rope(u_buf[j, :, sl], tabq_ref, rows)
            k_ref[rows, sl] = norm_rope(u_buf[j, :, width + hd * LANES:width + (hd + 1) * LANES],
                                        tabk_ref, rows)


def _proj_odd(h, gmix, w, wvt, gq3, gk3, cs):
    t = h.shape[0]
    tm = TM_PROJ
    row = lambda wd: pl.BlockSpec((tm, wd), lambda i: (i, 0))
    width = C_HEADS * 2 * C_DH
    consts = (gmix, w, wvt, gq3, gk3)
    return pl.pallas_call(
        _proj_odd_kernel,
        grid=(t // tm,),
        in_specs=[row(D_MODEL)] + [_const_spec(c.shape) for c in consts] + [row(cs.shape[1])],
        out_specs=[row(width), row(width), pl.BlockSpec((width, tm), lambda i: (0, i))],
        out_shape=[jax.ShapeDtypeStruct((t, width), BF16), jax.ShapeDtypeStruct((t, width), BF16),
                   jax.ShapeDtypeStruct((width, t), BF16)],
        scratch_shapes=[pltpu.VMEM((3, tm, LANES), F32)] * 2
        + [pltpu.VMEM((PROJ_PARTS, tm // PROJ_PARTS, 2 * width), F32)],
        compiler_params=_cparams(("parallel",)),
        name="proj_odd",
    )(h, *consts, cs)


def _tiled_rows(ref, start, n):
    return jnp.concatenate([ref[pl.ds(start * ROW_TILES + s, n, stride=ROW_TILES), :]
                            for s in range(ROW_TILES)], axis=1)


def _store_tiled(ref, start, val, accumulate=False):
    for s in range(ROW_TILES):
        idx = pl.ds(start * ROW_TILES + s, val.shape[0], stride=ROW_TILES)
        piece = val[:, s * LANES:(s + 1) * LANES]
        ref[idx, :] = ref[idx, :] + piece if accumulate else piece


def _out_router_kernel(h_ref, y_ref, wo_ref, gff_ref, wr_ref, h1_ref, idx_ref, gate_ref):
    h1 = h_ref[...] + jnp.dot(y_ref[...], wo_ref[...], preferred_element_type=F32)
    _store_tiled(h1_ref, 0, h1)
    xn = h1 * _rms_scale(h1) * gff_ref[...]
    logits = [jnp.sum(xn * wr_ref[e:e + 1, :], axis=-1, keepdims=True) for e in range(N_EXPERTS)]
    m1 = functools.reduce(jnp.maximum, logits)
    idx1 = functools.reduce(jnp.minimum,
                            [jnp.where(logits[e] == m1, e, N_EXPERTS) for e in range(N_EXPERTS)])
    rest = [jnp.where(idx1 == e, -jnp.inf, logits[e]) for e in range(N_EXPERTS)]
    m2 = functools.reduce(jnp.maximum, rest)
    idx2 = functools.reduce(jnp.minimum,
                            [jnp.where(rest[e] == m2, e, N_EXPERTS) for e in range(N_EXPERTS)])
    e2 = jnp.exp(m2 - m1)
    g1 = 1.0 / (1.0 + e2)
    g2 = e2 * g1
    lane = _lane_iota()
    idx_ref[...] = jnp.where(lane == 0, idx1, jnp.where(lane == 1, idx2, 0))
    gate_ref[...] = jnp.where(lane == 0, g1, jnp.where(lane == 1, g2, 0.0))


def _out_router(h, y, wo, gff, wr):
    t = h.shape[0]
    tm = TM_PROJ
    row = lambda w: pl.BlockSpec((tm, w), lambda i: (i, 0))
    return pl.pallas_call(
        _out_router_kernel,
        grid=(t // tm,),
        in_specs=[row(D_MODEL), row(y.shape[1]), _const_spec(wo.shape), _const_spec(gff.shape),
                  _const_spec(wr.shape)],
        out_specs=[pl.BlockSpec((tm * ROW_TILES, LANES), lambda i: (i, 0)), row(LANES), row(LANES)],
        out_shape=[jax.ShapeDtypeStruct((t * ROW_TILES, LANES), F32),
                   jax.ShapeDtypeStruct((t, LANES), jnp.int32), jax.ShapeDtypeStruct((t, LANES), F32)],
        compiler_params=_cparams(("parallel",)),
        name="out_router",
    )(h, y, wo, gff, wr)


def _row_copy(src_ref, src_row, dst_ref, dst_row, sem):
    tile = lambda r: pl.ds(pl.multiple_of(r * ROW_TILES, ROW_TILES), ROW_TILES)
    return pltpu.make_async_copy(src_ref.at[tile(src_row)], dst_ref.at[tile(dst_row)], sem)


def _dispatch_kernel(row_ref, x_ref, xs_ref, sem):
    tm = x_ref.shape[0] // ROW_TILES
    base = pl.program_id(0) * tm

    def issue(r, carry):
        for k in range(2):
            _row_copy(x_ref, r, xs_ref, row_ref[2 * (base + r) + k], sem).start(priority=k)
        return carry

    lax.fori_loop(0, tm, issue, 0, unroll=DMA_UNROLL)
    for k in range(2):
        pltpu.make_async_copy(x_ref, xs_ref.at[pl.ds(0, tm * ROW_TILES)], sem).wait()


def _dispatch(row_of_pair, x):
    t = x.shape[0] // ROW_TILES
    tm = TM_DISPATCH
    grid_spec = pltpu.PrefetchScalarGridSpec(
        num_scalar_prefetch=1,
        grid=(t // tm,),
        in_specs=[pl.BlockSpec((tm * ROW_TILES, LANES), lambda i, rows: (i, 0))],
        out_specs=pl.BlockSpec(memory_space=pl.ANY),
        scratch_shapes=[pltpu.SemaphoreType.DMA(())],
    )
    return pl.pallas_call(
        _dispatch_kernel,
        grid_spec=grid_spec,
        out_shape=jax.ShapeDtypeStruct((2 * t * ROW_TILES, LANES), x.dtype),
        compiler_params=_cparams(("arbitrary",)),
        name="moe_dispatch",
    )(row_of_pair, x)


def _moe_kernel(vt_ref, ve_ref, lo_ref, hi_ref, first_ref, x_ref, gff_ref, wg_ref, wu_ref, wd_ref, o_ref,
                xb_ref, acc_ref):
    v = pl.program_id(0)
    c = pl.program_id(1)
    tr = xb_ref.shape[0]

    lo = lo_ref[v]
    hi = hi_ref[v]
    base = vt_ref[v] * tr
    whole_tile = (lo == base) & (hi == base + tr)

    @pl.when((c == 0) & (first_ref[v] == 1) & jnp.logical_not(whole_tile))
    def _():
        o_ref[...] = jnp.zeros_like(o_ref)

    @pl.when(c == 0)
    def _():
        x = _tiled_rows(x_ref, 0, tr)
        xb_ref[...] = (x * _rms_scale(x) * gff_ref[...]).astype(BF16)
        acc_ref[...] = jnp.zeros_like(acc_ref)

    def expert_rows(off, n, whole):
        rows = slice(off, off + n)
        x = xb_ref[rows, :]
        g = jnp.dot(x, wg_ref[0].astype(BF16), preferred_element_type=F32)
        u = jnp.dot(x, wu_ref[0].astype(BF16), preferred_element_type=F32)
        a = (_silu(g) * u).astype(BF16)
        y = jnp.dot(a, wd_ref[0].astype(BF16), preferred_element_type=F32)
        if not whole:
            r = base + off + lax.broadcasted_iota(jnp.int32, (n, 1), 0)
            y = jnp.where((r >= lo) & (r < hi), y, 0.0)
        acc_ref[rows, :] += y

    @pl.when(whole_tile)
    def _():
        expert_rows(0, tr, True)

    for s in range(tr // MOE_SUB):
        @pl.when(jnp.logical_not(whole_tile) & (base + s * MOE_SUB < hi) & (base + (s + 1) * MOE_SUB > lo))
        def _(s=s):
            expert_rows(s * MOE_SUB, MOE_SUB, False)

    last = c == pl.num_programs(1) - 1

    @pl.when(last & whole_tile)
    def _():
        _store_tiled(o_ref, 0, acc_ref[...])

    @pl.when(last & jnp.logical_not(whole_tile) & (lo < hi))
    def _():
        _store_tiled(o_ref, 0, acc_ref[...], accumulate=True)


def _moe(visits, xs, gff, wg, wu, wd):
    r = xs.shape[0] // ROW_TILES
    tr, fc = TR_MOE, FC_MOE
    nc = FF_EXPERT // fc
    n_visits = r // tr + N_EXPERTS - 1

    def chunk(v, c, lo, hi):
        return jnp.where(lo[v] < hi[v], c, nc - 1)

    tile = lambda v, c, vt, ve, lo, hi, first: (vt[v], 0)
    wcol = lambda v, c, vt, ve, lo, hi, first: (ve[v], 0, chunk(v, c, lo, hi))
    wrow = lambda v, c, vt, ve, lo, hi, first: (ve[v], chunk(v, c, lo, hi), 0)
    grid_spec = pltpu.PrefetchScalarGridSpec(
        num_scalar_prefetch=5,
        grid=(n_visits, nc),
        in_specs=[pl.BlockSpec((tr * ROW_TILES, LANES), tile),
                  pl.BlockSpec((1, D_MODEL), lambda v, c, vt, ve, lo, hi, first: (0, 0)),
                  pl.BlockSpec((1, D_MODEL, fc), wcol),
                  pl.BlockSpec((1, D_MODEL, fc), wcol),
                  pl.BlockSpec((1, fc, D_MODEL), wrow)],
        out_specs=pl.BlockSpec((tr * ROW_TILES, LANES), tile),
        scratch_shapes=[pltpu.VMEM((tr, D_MODEL), BF16), pltpu.VMEM((tr, D_MODEL), F32)],
    )
    return pl.pallas_call(
        _moe_kernel,
        grid_spec=grid_spec,
        out_shape=jax.ShapeDtypeStruct((r * ROW_TILES, LANES), F32),
        compiler_params=_cparams(("arbitrary", "arbitrary")),
        name="moe_grouped",
    )(*visits, xs, gff, wg, wu, wd)


def _combine_kernel(row_ref, h_ref, gate_ref, ys_ref, o_ref, buf, sem):
    tm = gate_ref.shape[0]
    base = pl.program_id(0) * tm

    def issue(r, carry):
        for k in range(2):
            _row_copy(ys_ref, row_ref[2 * (base + r) + k], buf, k * tm + r, sem).start(priority=k)
        return carry

    lax.fori_loop(0, tm, issue, 0, unroll=DMA_UNROLL)
    pltpu.make_async_copy(ys_ref.at[pl.ds(0, 2 * tm * ROW_TILES)], buf, sem).wait()
    gate = gate_ref[...]
    o_ref[...] = (_tiled_rows(h_ref, 0, tm) + gate[:, 0:1] * _tiled_rows(buf, 0, tm)
                  + gate[:, 1:2] * _tiled_rows(buf, tm, tm))


def _combine(row_of_pair, h1, gates, ys):
    t = gates.shape[0]
    d = D_MODEL
    tm = TM_DISPATCH
    grid_spec = pltpu.PrefetchScalarGridSpec(
        num_scalar_prefetch=1,
        grid=(t // tm,),
        in_specs=[pl.BlockSpec((tm * ROW_TILES, LANES), lambda i, rows: (i, 0)),
                  pl.BlockSpec((tm, LANES), lambda i, rows: (i, 0)),
                  pl.BlockSpec(memory_space=pl.ANY)],
        out_specs=pl.BlockSpec((tm, d), lambda i, rows: (i, 0)),
        scratch_shapes=[pltpu.VMEM((2 * tm * ROW_TILES, LANES), F32), pltpu.SemaphoreType.DMA(())],
    )
    return pl.pallas_call(
        _combine_kernel,
        grid_spec=grid_spec,
        out_shape=jax.ShapeDtypeStruct((t, d), F32),
        compiler_params=_cparams(("arbitrary",)),
        name="moe_combine",
    )(row_of_pair, h1, gates, ys)


def _route_plan(idx):
    n_pairs = idx.size
    tr = TR_MOE
    n_tiles = n_pairs // tr
    n_visits = n_tiles + N_EXPERTS - 1
    e_flat = idx.reshape(-1)
    onehot = (e_flat[:, None] == jnp.arange(N_EXPERTS, dtype=jnp.int32)[None, :]).astype(jnp.int32)
    csum = jnp.cumsum(onehot, axis=0)
    counts = csum[-1]
    ends = jnp.cumsum(counts)
    starts = ends - counts
    pos = jnp.sum((csum - onehot) * onehot, axis=1)
    row_of_pair = (starts[e_flat] + pos).astype(jnp.int32)
    nv = jnp.where(counts > 0, (ends - 1) // tr - starts // tr + 1, 0)
    vend = jnp.cumsum(nv)
    vid = jnp.arange(n_visits, dtype=jnp.int32)
    live = vid < vend[-1]
    ve = jnp.minimum(jnp.sum((vid[:, None] >= vend[None, :]).astype(jnp.int32), axis=1), N_EXPERTS - 1)
    vt = jnp.where(live, starts[ve] // tr + vid - (vend - nv)[ve], n_tiles - 1)
    lo = jnp.where(live, jnp.maximum(starts[ve], vt * tr), 0)
    hi = jnp.where(live, jnp.minimum(ends[ve], (vt + 1) * tr), 0)
    last_e = jnp.max(jnp.where(counts > 0, jnp.arange(N_EXPERTS, dtype=jnp.int32), 0))
    ve = jnp.where(live, ve, last_e)
    first = (live & (lo == vt * tr)).astype(jnp.int32)
    i32 = lambda a: a.astype(jnp.int32)
    return row_of_pair, (i32(vt), i32(ve), i32(lo), i32(hi), first)


def _cos_sin(positions, dim, theta):
    inv = 1.0 / (theta ** (jnp.arange(0, dim, 2, dtype=F32) / dim))
    ang = positions.astype(F32).reshape(-1, 1) * inv
    return jnp.concatenate([jnp.cos(ang), jnp.sin(ang)], axis=1)


def _gain3(g, half):
    return jnp.stack([g, jnp.roll(g, -half), jnp.roll(g, half)]).astype(F32)


def _band_bias(rel_table):
    nh = rel_table.shape[0]
    nk, nq = BAND_KEYS, TQ_BAND
    length = BAND_PROFILE
    n_lo = nk - 1 - B_LEFT_CHUNKS * CHUNK - B_REL_CLIP
    n_hi = length - n_lo - rel_table.shape[1]
    prof = jnp.concatenate([jnp.broadcast_to(rel_table[:, :1], (nh, n_lo)), rel_table,
                            jnp.broadcast_to(rel_table[:, -1:], (nh, n_hi))], axis=1)
    return pl.pallas_call(
        _band_bias_kernel,
        grid=(nh,),
        in_specs=[pl.BlockSpec((None, 1, length), lambda h: (h, 0, 0))],
        out_specs=pl.BlockSpec((None, nk, nq), lambda h: (h // 2, 0, h % 2)),
        out_shape=jax.ShapeDtypeStruct((nh // 2, nk, 2 * nq), F32),
        compiler_params=_cparams(("parallel",)),
        name="band_bias",
    )(prof.astype(F32).reshape(nh, 1, length))


def _band_bias_kernel(p_ref, o_ref):
    nk, nq = o_ref.shape
    length = p_ref.shape[1]
    x = jnp.broadcast_to(p_ref[...] * LOG2E, (8, length))
    sub = lax.broadcasted_iota(jnp.int32, (8, nq), 0)
    qchunk = lax.broadcasted_iota(jnp.int32, (8, nq), 1) // CHUNK

    def body(jb, carry):
        rolled = pltpu.roll(x, jb * 8 + (length - (nk - 1)), 1, stride=1, stride_axis=0)
        cdiff = qchunk + B_LEFT_CHUNKS - (jb * 8 + sub) // CHUNK
        allowed = (cdiff >= 0) & (cdiff <= B_LEFT_CHUNKS)
        o_ref[pl.ds(pl.multiple_of(jb * 8, 8), 8), :] = jnp.where(allowed, rolled[:, :nq], NEG)
        return carry

    lax.fori_loop(0, nk // 8, body, 0, unroll=8)


def _pad_last(a, width):
    return jnp.pad(a, [(0, 0)] * (a.ndim - 1) + [(0, width - a.shape[-1])])


def kernel(x, positions, ev_norm_mix, ev_w_in, ev_g_cq, ev_g_ckv, ev_w_uq, ev_w_ukv, ev_a_qnorm, ev_a_knorm, ev_b_qnorm, ev_b_knorm, ev_b_rel_bias, ev_w_out, ev_norm_ff, ev_ff_gate, ev_ff_up, ev_ff_down, od_norm_mix, od_w_qkv, od_c_qnorm, od_c_knorm, od_lam_q1, od_lam_k1, od_lam_q2, od_lam_k2, od_c_subnorm, od_w_out, od_norm_ff, od_router, od_ex_gate, od_ex_up, od_ex_down):
    batch, seq, d = x.shape
    t = batch * seq
    depth = ev_w_in.shape[0] + od_w_qkv.shape[0]
    h = x.reshape(t, d)
    row = lambda a: a.reshape(1, -1).astype(F32)
    cs_a = _cos_sin(positions, A_ROPE, A_THETA)
    cs_c = _cos_sin(positions, C_ROPE, ROPE_THETA)

    for layer in range(depth):
        i = layer // 2
        if layer % 2 == 0:
            w = ev_w_in[i]
            o1, o2, o3 = A_Q_LORA, A_Q_LORA + A_KV_LORA, A_Q_LORA + A_KV_LORA + A_ROPE
            zc = lambda n: jnp.zeros((d, n), F32)
            bw = B_HEADS * B_DH
            win = jnp.concatenate([w[:, :o2], zc(64), w[:, o2:o3], zc(32), w[:, o3:o3 + 2 * bw]], 1).astype(BF16)
            wvbt = w[:, o3 + 2 * bw:].T.astype(BF16)
            wuq = _pad_last(ev_w_uq[i], LANES).reshape(A_Q_LORA, A_HEADS * LANES).astype(BF16)
            wuk = _pad_last(ev_w_ukv[i][:, :, :A_NOPE], LANES).reshape(A_KV_LORA, A_HEADS * LANES).astype(BF16)
            wuvt = ev_w_ukv[i][:, :, A_NOPE:].reshape(A_KV_LORA, A_HEADS * A_V).T.astype(BF16)
            sc_a = (A_NOPE + A_ROPE) ** -0.5 * LOG2E
            gq3 = _gain3(_pad_last(ev_a_qnorm[i], LANES) * sc_a, A_ROPE // 2)
            gk = jnp.tile(_pad_last(ev_a_knorm[i][:A_NOPE], LANES), A_HEADS).reshape(1, -1)
            gkr3 = _gain3(jnp.concatenate([jnp.zeros((A_NOPE,), F32), ev_a_knorm[i][A_NOPE:],
                                           jnp.zeros((32,), F32)]), A_ROPE // 2)
            gbq = jnp.tile(ev_b_qnorm[i], B_HEADS).reshape(1, -1) * (B_DH ** -0.5 * LOG2E)
            gbk = jnp.tile(ev_b_knorm[i], B_HEADS).reshape(1, -1)
            qa, ka, vat, qb, kb, vbt = _proj_even(
                h, row(ev_norm_mix[i]), win, wvbt, row(ev_g_cq[i]), row(ev_g_ckv[i]), wuq, wuk, wuvt,
                gq3, gk, gkr3, gbq, gbk, cs_a)
            ya = _attn_causal(_attn_mla_kernel, qa, ka, vat, (), batch=batch, seq=seq,
                              n_blocks=A_HEADS // 2, qk_width=2 * LANES)
            yb = _attn_band(qb, kb, vbt, _band_bias(ev_b_rel_bias[i]), batch=batch, seq=seq)
            wo = ev_w_out[i].astype(BF16)
            na = A_HEADS * A_V
            padf = lambda a: _pad_last(a, FF_DENSE_PAD).astype(BF16)
            wd = jnp.pad(ev_ff_down[i], ((0, FF_DENSE_PAD - FF_DENSE), (0, 0))).astype(BF16)
            h = _out_ffn(h, ya, yb, wo[:na], wo[na:], row(ev_norm_ff[i]),
                         padf(ev_ff_gate[i]), padf(ev_ff_up[i]), wd)
        else:
            lam_init = 0.8 - 0.6 * math.exp(-0.3 * layer)
            gq3 = _gain3(jnp.tile(od_c_qnorm[i], 2) * (C_DH ** -0.5 * LOG2E), C_ROPE // 2)
            gk3 = _gain3(jnp.tile(od_c_knorm[i], 2), C_ROPE // 2)
            c_mix = C_HEADS * 2 * C_DH
            wqkv = od_w_qkv[i]
            q, k, v = _proj_odd(h, row(od_norm_mix[i]), wqkv[:, :2 * c_mix].astype(BF16),
                                wqkv[:, 2 * c_mix:].T.astype(BF16), gq3, gk3, cs_c)
            extras = (row(od_lam_q1[i]), row(od_lam_k1[i]), row(od_lam_q2[i]), row(od_lam_k2[i]),
                      row(od_c_subnorm[i]))
            yc = _attn_causal(functools.partial(_attn_diff_kernel, lam_init=lam_init), q, k, v, extras,
                              batch=batch, seq=seq, n_blocks=C_HEADS, qk_width=LANES)
            h1, idx, gates = _out_router(h, yc, od_w_out[i].astype(BF16), row(od_norm_ff[i]),
                                         od_router[i].T.astype(F32))
            row_of_pair, visits = _route_plan(idx[:, :2])
            xs = _dispatch(row_of_pair, h1)
            ys = _moe(visits, xs, row(od_norm_ff[i]), od_ex_gate[i], od_ex_up[i], od_ex_down[i])
            h = _combine(row_of_pair, h1, gates, ys)
    return h.reshape(batch, seq, d)
```

```python
import functools
import math

import numpy as np
import jax
import jax.numpy as jnp
from jax import lax
from jax.experimental import pallas as pl
from jax.experimental.pallas import tpu as pltpu

F32 = jnp.float32
BF16 = jnp.bfloat16

D_MODEL = 1024
CHUNK = 64
EPS = 1e-6
LOG2E = 1.4426950408889634
NEG = -1e30

A_HEADS = 8
A_NOPE = 64
A_ROPE = 32
A_V = 64
A_Q_LORA = 256
A_KV_LORA = 128
A_THETA = 10000.0
B_HEADS = 8
B_DH = 64
B_LEFT_CHUNKS = 8
B_REL_CLIP = 128
C_HEADS = 8
C_DH = 64
C_ROPE = C_DH // 4
ROPE_THETA = 500000.0
FF_DENSE = 2752
N_EXPERTS = 8
FF_EXPERT = 3584

LANES = 128
ROW_TILES = D_MODEL // LANES
FF_DENSE_PAD = 2816
VMEM_LIMIT = 56 * 1024 * 1024

EV_KR_OFF = A_Q_LORA + A_KV_LORA
EV_QB_OFF = EV_KR_OFF + LANES
EV_KB_OFF = EV_QB_OFF + B_HEADS * B_DH

TM_PROJ = 512
PROJ_PARTS = 2
TQ_CAUSAL = 256
TQ_BAND = 256
BAND_KEYS = TQ_BAND + B_LEFT_CHUNKS * CHUNK
BAND_PROFILE = 1024
TM_DISPATCH = 512
DMA_UNROLL = 16
TR_MOE = 1024
MOE_SUB = 256
FC_MOE = 512


def _cparams(sem):
    return pltpu.CompilerParams(dimension_semantics=sem, vmem_limit_bytes=VMEM_LIMIT)


def _const_spec(shape):
    nd = len(shape)
    return pl.BlockSpec(shape, lambda *_: (0,) * nd, pipeline_mode=pl.Buffered(1))


def _rms_scale(x):
    return lax.rsqrt(jnp.mean(x * x, axis=-1, keepdims=True) + EPS)


def _lane_iota():
    return lax.broadcasted_iota(jnp.int32, (1, LANES), 1)


def _dot_nt(a, b):
    return lax.dot_general(a, b, (((1,), (1,)), ((), ())), preferred_element_type=F32)


def _rope_tables(cs, half, starts):
    tm = cs.shape[0]
    base = jnp.concatenate([cs, jnp.zeros((tm, LANES - 2 * half), F32)], axis=1)
    shifted = lambda s: base if s % LANES == 0 else pltpu.roll(base, s % LANES, 1)
    lane = _lane_iota()
    cos_t = jnp.ones((tm, LANES), F32)
    sin_a = jnp.zeros((tm, LANES), F32)
    sin_b = jnp.zeros((tm, LANES), F32)
    for s0 in starts:
        first = (lane >= s0) & (lane < s0 + half)
        second = (lane >= s0 + half) & (lane < s0 + 2 * half)
        at_s0 = shifted(s0)
        cos_t = jnp.where(first, at_s0, jnp.where(second, shifted(s0 + half), cos_t))
        sin_a = jnp.where(first, -shifted(s0 - half), sin_a)
        sin_b = jnp.where(second, at_s0, sin_b)
    return cos_t, sin_a, sin_b


def _store_gained(tables, g3_ref, tab_ref):
    for j in range(3):
        tab_ref[j] = tables[j] * g3_ref[j:j + 1, :]


def _rope(x, half, tab_ref, rows=slice(None)):
    return (x * tab_ref[0, rows, :] + pltpu.roll(x, LANES - half, 1) * tab_ref[1, rows, :]
            + pltpu.roll(x, half, 1) * tab_ref[2, rows, :])


def _proj_even_kernel(h_ref, gmix_ref, win_ref, wvbt_ref, gcq_ref, gckv_ref, wuq_ref, wuk_ref, wuvt_ref,
                      gq3_ref, gk_ref, gkr3_ref, gbq_ref, gbk_ref, cs_ref,
                      qa_ref, ka_ref, vat_ref, qb_ref, kb_ref, vbt_ref, tabq_ref, tabk_ref,
                      u_buf, q_buf, kn_buf):
    n_part = u_buf.shape[0]
    rows_part = u_buf.shape[1]
    lo = _lane_iota() < 64

    def project(j):
        rows = slice(j * rows_part, (j + 1) * rows_part)
        x = h_ref[rows, :]
        hn = (x * _rms_scale(x) * gmix_ref[...]).astype(BF16)
        u = jnp.dot(hn, win_ref[...], preferred_element_type=F32)
        u_buf[j] = u
        vbt_ref[:, rows] = _dot_nt(wvbt_ref[...], hn).astype(BF16)
        cq = u[:, 0:A_Q_LORA]
        cqn = (cq * _rms_scale(cq) * gcq_ref[...]).astype(BF16)
        q_buf[j] = jnp.dot(cqn, wuq_ref[...], preferred_element_type=F32)
        ckv = u[:, A_Q_LORA:A_Q_LORA + A_KV_LORA]
        ckvn = (ckv * _rms_scale(ckv) * gckv_ref[...]).astype(BF16)
        kn_buf[j] = jnp.dot(ckvn, wuk_ref[...], preferred_element_type=F32)
        vat_ref[:, rows] = _dot_nt(wuvt_ref[...], ckvn).astype(BF16)

    project(0)
    tables = _rope_tables(cs_ref[...], A_ROPE // 2, (A_NOPE,))
    _store_gained(tables, gq3_ref, tabq_ref)
    _store_gained(tables, gkr3_ref, tabk_ref)

    def pair_norm(blk, g):
        sq = blk * blk
        tot = jnp.sum(sq, axis=-1, keepdims=True)
        s_lo = jnp.sum(jnp.where(lo, sq, 0.0), axis=-1, keepdims=True)
        sc = jnp.where(lo, lax.rsqrt(s_lo * (1.0 / B_DH) + EPS),
                       lax.rsqrt((tot - s_lo) * (1.0 / B_DH) + EPS))
        return (blk * sc * g).astype(BF16)

    for j in range(n_part):
        if j + 1 < n_part:
            project(j + 1)
        rows = slice(j * rows_part, (j + 1) * rows_part)
        kr = u_buf[j, :, EV_KR_OFF:EV_KR_OFF + LANES]
        kr_s = lax.rsqrt(jnp.sum(kr * kr, axis=-1, keepdims=True) * (1.0 / A_ROPE) + EPS)
        krot = _rope(kr, A_ROPE // 2, tabk_ref, rows) * kr_s
        for hd in range(A_HEADS):
            sl = slice(hd * LANES, (hd + 1) * LANES)
            blk = q_buf[j, :, sl]
            sq = blk * blk
            tot = jnp.sum(sq, axis=-1, keepdims=True)
            ssn = jnp.sum(jnp.where(lo, sq, 0.0), axis=-1, keepdims=True)
            sc = jnp.where(lo, lax.rsqrt(ssn * (1.0 / A_NOPE) + EPS),
                           lax.rsqrt((tot - ssn) * (1.0 / A_ROPE) + EPS))
            qa_ref[rows, sl] = (_rope(blk, A_ROPE // 2, tabq_ref, rows) * sc).astype(BF16)
            kb_ = kn_buf[j, :, sl]
            ks = lax.rsqrt(jnp.sum(kb_ * kb_, axis=-1, keepdims=True) * (1.0 / A_NOPE) + EPS)
            ka_ref[rows, sl] = (kb_ * ks * gk_ref[:, sl] + krot).astype(BF16)
        for p in range(B_HEADS // 2):
            sl = slice(p * LANES, (p + 1) * LANES)
            qb_ref[rows, sl] = pair_norm(u_buf[j, :, EV_QB_OFF + p * LANES:EV_QB_OFF + (p + 1) * LANES],
                                         gbq_ref[:, sl])
            kb_ref[rows, sl] = pair_norm(u_buf[j, :, EV_KB_OFF + p * LANES:EV_KB_OFF + (p + 1) * LANES],
                                         gbk_ref[:, sl])


def _proj_even(h, gmix, win, wvbt, gcq, gckv, wuq, wuk, wuvt, gq3, gk, gkr3, gbq, gbk, cs):
    t = h.shape[0]
    tm = TM_PROJ
    row = lambda w: pl.BlockSpec((tm, w), lambda i: (i, 0))
    col = lambda w: pl.BlockSpec((w, tm), lambda i: (0, i))
    rshape = lambda w: jax.ShapeDtypeStruct((t, w), BF16)
    cshape = lambda w: jax.ShapeDtypeStruct((w, t), BF16)
    consts = (gmix, win, wvbt, gcq, gckv, wuq, wuk, wuvt, gq3, gk, gkr3, gbq, gbk)
    return pl.pallas_call(
        _proj_even_kernel,
        grid=(t // tm,),
        in_specs=[row(D_MODEL)] + [_const_spec(c.shape) for c in consts] + [row(cs.shape[1])],
        out_specs=[row(1024), row(1024), col(512), row(512), row(512), col(512)],
        out_shape=[rshape(1024), rshape(1024), cshape(512), rshape(512), rshape(512), cshape(512)],
        scratch_shapes=[pltpu.VMEM((3, tm, LANES), F32)] * 2
        + [pltpu.VMEM((PROJ_PARTS, tm // PROJ_PARTS, w), F32) for w in (win.shape[1], 1024, 1024)],
        compiler_params=_cparams(("parallel",)),
        name="proj_even",
    )(h, *consts, cs)


def _causal_scores(q_ref, k_ref, s_ref, c, *, split):
    tq = TQ_CAUSAL
    q = q_ref[c * tq:(c + 1) * tq, :]
    zero = jnp.zeros_like(q[:, :LANES])
    if split:
        q2 = jnp.concatenate([jnp.concatenate([q[:, :LANES], zero], axis=1),
                              jnp.concatenate([zero, q[:, LANES:]], axis=1)], axis=0)
    else:
        lo = _lane_iota() < 64
        q2 = jnp.concatenate([jnp.where(lo, q, zero), jnp.where(lo, zero, q)], axis=0)
    kchunk = lax.broadcasted_iota(jnp.int32, (tq, 2 * tq), 0) // CHUNK
    qchunk = (lax.broadcasted_iota(jnp.int32, (tq, 2 * tq), 1) % tq) // CHUNK
    n_past = c * tq
    s_d = jnp.where(kchunk <= qchunk, _dot_nt(k_ref[n_past:n_past + tq, :], q2), NEG)
    s_ref[n_past:n_past + tq, :] = s_d
    mx = jnp.max(s_d, axis=0, keepdims=True)
    if c:
        s_p = _dot_nt(k_ref[0:n_past, :], q2)
        s_ref[0:n_past, :] = s_p
        mx = jnp.maximum(mx, jnp.max(s_p, axis=0, keepdims=True))
    return mx


def _causal_values(vt_ref, s_ref, c, mx):
    n_keys = (c + 1) * TQ_CAUSAL
    p = jnp.exp2(s_ref[0:n_keys, :] - mx)
    o = jnp.dot(vt_ref[:, 0:n_keys], p.astype(BF16), preferred_element_type=F32)
    return o * (1.0 / jnp.sum(p, axis=0, keepdims=True))


def _causal_tiles(q_ref, k_ref, vt_ref, s_buf, *, split):
    tq = TQ_CAUSAL
    n = q_ref.shape[0] // tq
    mx = _causal_scores(q_ref, k_ref, s_buf.at[0], 0, split=split)
    for c in range(n):
        mx_next = None
        if c + 1 < n:
            mx_next = _causal_scores(q_ref, k_ref, s_buf.at[(c + 1) % 2], c + 1, split=split)
        o = _causal_values(vt_ref, s_buf.at[c % 2], c, mx)
        yield c, o[:, :tq], o[:, tq:]
        mx = mx_next


def _row_iota(n):
    return lax.broadcasted_iota(jnp.int32, (LANES, n), 0)


def _attn_mla_kernel(q_ref, k_ref, vt_ref, o_ref, s_buf):
    tq = TQ_CAUSAL
    for c, o0, o1 in _causal_tiles(q_ref, k_ref, vt_ref, s_buf, split=True):
        o_ref[c * tq:(c + 1) * tq, :] = jnp.where(_row_iota(tq) < 64, o0, o1).T.astype(o_ref.dtype)


def _attn_diff_kernel(q_ref, k_ref, vt_ref, lq1_ref, lk1_ref, lq2_ref, lk2_ref, gsub_ref, o_ref, s_buf,
                      *, lam_init):
    tq = TQ_CAUSAL
    lam = (jnp.exp(jnp.sum(lq1_ref[...] * lk1_ref[...], axis=-1, keepdims=True))
           - jnp.exp(jnp.sum(lq2_ref[...] * lk2_ref[...], axis=-1, keepdims=True)) + lam_init)
    for c, o0, o1 in _causal_tiles(q_ref, k_ref, vt_ref, s_buf, split=False):
        o = (o0 - lam * o1).T
        o = o * _rms_scale(o) * gsub_ref[...] * (1.0 - lam_init)
        o_ref[c * tq:(c + 1) * tq, :] = o.astype(o_ref.dtype)


def _attn_causal(kernel, q, k, vt, extras, *, batch, seq, n_blocks, qk_width):
    in_specs = [pl.BlockSpec((seq, qk_width), lambda b, p: (b, p)),
                pl.BlockSpec((seq, qk_width), lambda b, p: (b, p)),
                pl.BlockSpec((LANES, seq), lambda b, p: (p, b))]
    in_specs += [_const_spec(e.shape) for e in extras]
    return pl.pallas_call(
        kernel,
        grid=(batch, n_blocks),
        in_specs=in_specs,
        out_specs=pl.BlockSpec((seq, LANES), lambda b, p: (b, p)),
        out_shape=jax.ShapeDtypeStruct((batch * seq, n_blocks * LANES), BF16),
        scratch_shapes=[pltpu.VMEM((2, seq, 2 * TQ_CAUSAL), F32)],
        compiler_params=_cparams(("parallel", "parallel")),
        name=kernel.func.__name__ if isinstance(kernel, functools.partial) else kernel.__name__,
    )(q, k, vt, *extras)


def _attn_band_kernel(q_ref, k_ref, vt_ref, bias_ref, o_ref, s_buf):
    tq = TQ_BAND
    n = q_ref.shape[0] // tq
    lo = _lane_iota() < 64

    def window(c):
        end = (c + 1) * tq
        return max(0, end - BAND_KEYS), end

    def scores(c):
        start, end = window(c)
        q = q_ref[c * tq:(c + 1) * tq, :]
        zero = jnp.zeros_like(q)
        q2 = jnp.concatenate([jnp.where(lo, q, zero), jnp.where(lo, zero, q)], axis=0)
        s = _dot_nt(k_ref[start:end, :], q2) + bias_ref[BAND_KEYS - (end - start):, :]
        s_buf[c % 2, 0:end - start, :] = s
        return jnp.max(s, axis=0, keepdims=True)

    mx = scores(0)
    for c in range(n):
        mx_next = scores(c + 1) if c + 1 < n else None
        start, end = window(c)
        p = jnp.exp2(s_buf[c % 2, 0:end - start, :] - mx)
        o = jnp.dot(vt_ref[:, start:end], p.astype(BF16), preferred_element_type=F32)
        o = o * (1.0 / jnp.sum(p, axis=0, keepdims=True))
        o_ref[c * tq:(c + 1) * tq, :] = jnp.where(_row_iota(tq) < 64, o[:, :tq], o[:, tq:]).T.astype(o_ref.dtype)
        mx = mx_next


def _attn_band(q, k, vt, bias, *, batch, seq):
    npair = B_HEADS // 2
    return pl.pallas_call(
        _attn_band_kernel,
        grid=(batch, npair),
        in_specs=[pl.BlockSpec((seq, LANES), lambda b, p: (b, p)),
                  pl.BlockSpec((seq, LANES), lambda b, p: (b, p)),
                  pl.BlockSpec((LANES, seq), lambda b, p: (p, b)),
                  pl.BlockSpec((None, BAND_KEYS, 2 * TQ_BAND), lambda b, p: (p, 0, 0))],
        out_specs=pl.BlockSpec((seq, LANES), lambda b, p: (b, p)),
        out_shape=jax.ShapeDtypeStruct((batch * seq, npair * LANES), BF16),
        scratch_shapes=[pltpu.VMEM((2, BAND_KEYS, 2 * TQ_BAND), F32)],
        compiler_params=_cparams(("parallel", "parallel")),
        name="attn_band",
    )(q, k, vt, bias)


def _silu(g):
    return g / (1.0 + jnp.exp(-g))


def _out_ffn_kernel(h_ref, ya_ref, yb_ref, woa_ref, wob_ref, gff_ref, wg_ref, wu_ref, wd_ref, o_ref):
    h1 = (h_ref[...] + jnp.dot(ya_ref[...], woa_ref[...], preferred_element_type=F32)
          + jnp.dot(yb_ref[...], wob_ref[...], preferred_element_type=F32))
    xn = (h1 * _rms_scale(h1) * gff_ref[...]).astype(BF16)
    g = jnp.dot(xn, wg_ref[...], preferred_element_type=F32)
    u = jnp.dot(xn, wu_ref[...], preferred_element_type=F32)
    a = (_silu(g) * u).astype(BF16)
    o_ref[...] = h1 + jnp.dot(a, wd_ref[...], preferred_element_type=F32)


def _out_ffn(h, ya, yb, woa, wob, gff, wg, wu, wd):
    t = h.shape[0]
    tm = TM_PROJ
    row = lambda w: pl.BlockSpec((tm, w), lambda i: (i, 0))
    return pl.pallas_call(
        _out_ffn_kernel,
        grid=(t // tm,),
        in_specs=[row(D_MODEL), row(ya.shape[1]), row(yb.shape[1])]
        + [_const_spec(w.shape) for w in (woa, wob, gff, wg, wu, wd)],
        out_specs=row(D_MODEL),
        out_shape=jax.ShapeDtypeStruct((t, D_MODEL), F32),
        compiler_params=_cparams(("parallel",)),
        name="out_ffn",
    )(h, ya, yb, woa, wob, gff, wg, wu, wd)


def _proj_odd_kernel(h_ref, gmix_ref, w_ref, wvt_ref, gq3_ref, gk3_ref, cs_ref, q_ref, k_ref, vt_ref,
                     tabq_ref, tabk_ref, u_buf):
    n_part = u_buf.shape[0]
    rows_part = u_buf.shape[1]
    lo = _lane_iota() < 64
    width = C_HEADS * 2 * C_DH

    def project(j):
        rows = slice(j * rows_part, (j + 1) * rows_part)
        x = h_ref[rows, :]
        hn = (x * _rms_scale(x) * gmix_ref[...]).astype(BF16)
        u_buf[j] = jnp.dot(hn, w_ref[...], preferred_element_type=F32)
        vt_ref[:, rows] = _dot_nt(wvt_ref[...], hn).astype(BF16)

    project(0)
    tables = _rope_tables(cs_ref[...], C_ROPE // 2, (0, C_DH))
    _store_gained(tables, gq3_ref, tabq_ref)
    _store_gained(tables, gk3_ref, tabk_ref)

    def norm_rope(blk, tab_ref, rows):
        sq = blk * blk
        tot = jnp.sum(sq, axis=-1, keepdims=True)
        s_lo = jnp.sum(jnp.where(lo, sq, 0.0), axis=-1, keepdims=True)
        sc = jnp.where(lo, lax.rsqrt(s_lo * (1.0 / C_DH) + EPS),
                       lax.rsqrt((tot - s_lo) * (1.0 / C_DH) + EPS))
        return (_rope(blk, C_ROPE // 2, tab_ref, rows) * sc).astype(BF16)

    for j in range(n_part):
        if j + 1 < n_part:
            project(j + 1)
        rows = slice(j * rows_part, (j + 1) * rows_part)
        for hd in range(C_HEADS):
            sl = slice(hd * LANES, (hd + 1) * LANES)
            q_ref[rows, sl] = norm_rope(u_buf[j, :, sl], tabq_ref, rows)
            k_ref[rows, sl] = norm_rope(u_buf[j, :, width + hd * LANES:width + (hd + 1) * LANES],
                                        tabk_ref, rows)


def _proj_odd(h, gmix, w, wvt, gq3, gk3, cs):
    t = h.shape[0]
    tm = TM_PROJ
    row = lambda wd: pl.BlockSpec((tm, wd), lambda i: (i, 0))
    width = C_HEADS * 2 * C_DH
    consts = (gmix, w, wvt, gq3, gk3)
    return pl.pallas_call(
        _proj_odd_kernel,
        grid=(t // tm,),
        in_specs=[row(D_MODEL)] + [_const_spec(c.shape) for c in consts] + [row(cs.shape[1])],
        out_specs=[row(width), row(width), pl.BlockSpec((width, tm), lambda i: (0, i))],
        out_shape=[jax.ShapeDtypeStruct((t, width), BF16), jax.ShapeDtypeStruct((t, width), BF16),
                   jax.ShapeDtypeStruct((width, t), BF16)],
        scratch_shapes=[pltpu.VMEM((3, tm, LANES), F32)] * 2
        + [pltpu.VMEM((PROJ_PARTS, tm // PROJ_PARTS, 2 * width), F32)],
        compiler_params=_cparams(("parallel",)),
        name="proj_odd",
    )(h, *consts, cs)


def _tiled_rows(ref, start, n):
    return jnp.concatenate([ref[pl.ds(start * ROW_TILES + s, n, stride=ROW_TILES), :]
                            for s in range(ROW_TILES)], axis=1)


def _store_tiled(ref, start, val, accumulate=False):
    for s in range(ROW_TILES):
        idx = pl.ds(start * ROW_TILES + s, val.shape[0], stride=ROW_TILES)
        piece = val[:, s * LANES:(s + 1) * LANES]
        ref[idx, :] = ref[idx, :] + piece if accumulate else piece


def _out_router_kernel(h_ref, y_ref, wo_ref, gff_ref, wr_ref, h1_ref, idx_ref, gate_ref):
    h1 = h_ref[...] + jnp.dot(y_ref[...], wo_ref[...], preferred_element_type=F32)
    _store_tiled(h1_ref, 0, h1)
    xn = h1 * _rms_scale(h1) * gff_ref[...]
    logits = [jnp.sum(xn * wr_ref[e:e + 1, :], axis=-1, keepdims=True) for e in range(N_EXPERTS)]
    m1 = functools.reduce(jnp.maximum, logits)
    idx1 = functools.reduce(jnp.minimum,
                            [jnp.where(logits[e] == m1, e, N_EXPERTS) for e in range(N_EXPERTS)])
    rest = [jnp.where(idx1 == e, -jnp.inf, logits[e]) for e in range(N_EXPERTS)]
    m2 = functools.reduce(jnp.maximum, rest)
    idx2 = functools.reduce(jnp.minimum,
                            [jnp.where(rest[e] == m2, e, N_EXPERTS) for e in range(N_EXPERTS)])
    e2 = jnp.exp(m2 - m1)
    g1 = 1.0 / (1.0 + e2)
    g2 = e2 * g1
    lane = _lane_iota()
    idx_ref[...] = jnp.where(lane == 0, idx1, jnp.where(lane == 1, idx2, 0))
    gate_ref[...] = jnp.where(lane == 0, g1, jnp.where(lane == 1, g2, 0.0))


def _out_router(h, y, wo, gff, wr):
    t = h.shape[0]
    tm = TM_PROJ
    row = lambda w: pl.BlockSpec((tm, w), lambda i: (i, 0))
    return pl.pallas_call(
        _out_router_kernel,
        grid=(t // tm,),
        in_specs=[row(D_MODEL), row(y.shape[1]), _const_spec(wo.shape), _const_spec(gff.shape),
                  _const_spec(wr.shape)],
        out_specs=[pl.BlockSpec((tm * ROW_TILES, LANES), lambda i: (i, 0)), row(LANES), row(LANES)],
        out_shape=[jax.ShapeDtypeStruct((t * ROW_TILES, LANES), F32),
                   jax.ShapeDtypeStruct((t, LANES), jnp.int32), jax.ShapeDtypeStruct((t, LANES), F32)],
        compiler_params=_cparams(("parallel",)),
        name="out_router",
    )(h, y, wo, gff, wr)


def _row_copy(src_ref, src_row, dst_ref, dst_row, sem):
    tile = lambda r: pl.ds(pl.multiple_of(r * ROW_TILES, ROW_TILES), ROW_TILES)
    return pltpu.make_async_copy(src_ref.at[tile(src_row)], dst_ref.at[tile(dst_row)], sem)


def _dispatch_kernel(row_ref, x_ref, xs_ref, sem):
    tm = x_ref.shape[0] // ROW_TILES
    base = pl.program_id(0) * tm

    def issue(r, carry):
        for k in range(2):
            _row_copy(x_ref, r, xs_ref, row_ref[2 * (base + r) + k], sem).start()
        return carry

    lax.fori_loop(0, tm, issue, 0, unroll=DMA_UNROLL)
    for k in range(2):
        pltpu.make_async_copy(x_ref, xs_ref.at[pl.ds(0, tm * ROW_TILES)], sem).wait()


def _dispatch(row_of_pair, x):
    t = x.shape[0] // ROW_TILES
    tm = TM_DISPATCH
    grid_spec = pltpu.PrefetchScalarGridSpec(
        num_scalar_prefetch=1,
        grid=(t // tm,),
        in_specs=[pl.BlockSpec((tm * ROW_TILES, LANES), lambda i, rows: (i, 0))],
        out_specs=pl.BlockSpec(memory_space=pl.ANY),
        scratch_shapes=[pltpu.SemaphoreType.DMA(())],
    )
    return pl.pallas_call(
        _dispatch_kernel,
        grid_spec=grid_spec,
        out_shape=jax.ShapeDtypeStruct((2 * t * ROW_TILES, LANES), x.dtype),
        compiler_params=_cparams(("arbitrary",)),
        name="moe_dispatch",
    )(row_of_pair, x)


def _moe_kernel(vt_ref, ve_ref, lo_ref, hi_ref, first_ref, x_ref, gff_ref, wg_ref, wu_ref, wd_ref, o_ref,
                xb_ref, acc_ref):
    v = pl.program_id(0)
    c = pl.program_id(1)
    tr = xb_ref.shape[0]

    lo = lo_ref[v]
    hi = hi_ref[v]
    base = vt_ref[v] * tr
    whole_tile = (lo == base) & (hi == base + tr)

    @pl.when((c == 0) & (first_ref[v] == 1) & jnp.logical_not(whole_tile))
    def _():
        o_ref[...] = jnp.zeros_like(o_ref)

    @pl.when(c == 0)
    def _():
        x = _tiled_rows(x_ref, 0, tr)
        xb_ref[...] = (x * _rms_scale(x) * gff_ref[...]).astype(BF16)
        acc_ref[...] = jnp.zeros_like(acc_ref)

    def expert_rows(off, n, whole):
        rows = slice(off, off + n)
        x = xb_ref[rows, :]
        g = jnp.dot(x, wg_ref[0].astype(BF16), preferred_element_type=F32)
        u = jnp.dot(x, wu_ref[0].astype(BF16), preferred_element_type=F32)
        a = (_silu(g) * u).astype(BF16)
        y = jnp.dot(a, wd_ref[0].astype(BF16), preferred_element_type=F32)
        if not whole:
            r = base + off + lax.broadcasted_iota(jnp.int32, (n, 1), 0)
            y = jnp.where((r >= lo) & (r < hi), y, 0.0)
        acc_ref[rows, :] += y

    @pl.when(whole_tile)
    def _():
        expert_rows(0, tr, True)

    for s in range(tr // MOE_SUB):
        @pl.when(jnp.logical_not(whole_tile) & (base + s * MOE_SUB < hi) & (base + (s + 1) * MOE_SUB > lo))
        def _(s=s):
            expert_rows(s * MOE_SUB, MOE_SUB, False)

    last = c == pl.num_programs(1) - 1

    @pl.when(last & whole_tile)
    def _():
        _store_tiled(o_ref, 0, acc_ref[...])

    @pl.when(last & jnp.logical_not(whole_tile) & (lo < hi))
    def _():
        _store_tiled(o_ref, 0, acc_ref[...], accumulate=True)


def _moe(visits, xs, gff, wg, wu, wd):
    r = xs.shape[0] // ROW_TILES
    tr, fc = TR_MOE, FC_MOE
    nc = FF_EXPERT // fc
    n_visits = r // tr + N_EXPERTS - 1

    def chunk(v, c, lo, hi):
        return jnp.where(lo[v] < hi[v], c, nc - 1)

    tile = lambda v, c, vt, ve, lo, hi, first: (vt[v], 0)
    wcol = lambda v, c, vt, ve, lo, hi, first: (ve[v], 0, chunk(v, c, lo, hi))
    wrow = lambda v, c, vt, ve, lo, hi, first: (ve[v], chunk(v, c, lo, hi), 0)
    grid_spec = pltpu.PrefetchScalarGridSpec(
        num_scalar_prefetch=5,
        grid=(n_visits, nc),
        in_specs=[pl.BlockSpec((tr * ROW_TILES, LANES), tile),
                  pl.BlockSpec((1, D_MODEL), lambda v, c, vt, ve, lo, hi, first: (0, 0)),
                  pl.BlockSpec((1, D_MODEL, fc), wcol),
                  pl.BlockSpec((1, D_MODEL, fc), wcol),
                  pl.BlockSpec((1, fc, D_MODEL), wrow)],
        out_specs=pl.BlockSpec((tr * ROW_TILES, LANES), tile),
        scratch_shapes=[pltpu.VMEM((tr, D_MODEL), BF16), pltpu.VMEM((tr, D_MODEL), F32)],
    )
    return pl.pallas_call(
        _moe_kernel,
        grid_spec=grid_spec,
        out_shape=jax.ShapeDtypeStruct((r * ROW_TILES, LANES), F32),
        compiler_params=_cparams(("arbitrary", "arbitrary")),
        name="moe_grouped",
    )(*visits, xs, gff, wg, wu, wd)


def _combine_kernel(row_ref, h_ref, gate_ref, ys_ref, o_ref, buf, sem):
    tm = gate_ref.shape[0]
    base = pl.program_id(0) * tm

    def issue(r, carry):
        for k in range(2):
            _row_copy(ys_ref, row_ref[2 * (base + r) + k], buf, k * tm + r, sem).start()
        return carry

    lax.fori_loop(0, tm, issue, 0, unroll=DMA_UNROLL)
    pltpu.make_async_copy(ys_ref.at[pl.ds(0, 2 * tm * ROW_TILES)], buf, sem).wait()
    gate = gate_ref[...]
    o_ref[...] = (_tiled_rows(h_ref, 0, tm) + gate[:, 0:1] * _tiled_rows(buf, 0, tm)
                  + gate[:, 1:2] * _tiled_rows(buf, tm, tm))


def _combine(row_of_pair, h1, gates, ys):
    t = gates.shape[0]
    d = D_MODEL
    tm = TM_DISPATCH
    grid_spec = pltpu.PrefetchScalarGridSpec(
        num_scalar_prefetch=1,
        grid=(t // tm,),
        in_specs=[pl.BlockSpec((tm * ROW_TILES, LANES), lambda i, rows: (i, 0)),
                  pl.BlockSpec((tm, LANES), lambda i, rows: (i, 0)),
                  pl.BlockSpec(memory_space=pl.ANY)],
        out_specs=pl.BlockSpec((tm, d), lambda i, rows: (i, 0)),
        scratch_shapes=[pltpu.VMEM((2 * tm * ROW_TILES, LANES), F32), pltpu.SemaphoreType.DMA(())],
    )
    return pl.pallas_call(
        _combine_kernel,
        grid_spec=grid_spec,
        out_shape=jax.ShapeDtypeStruct((t, d), F32),
        compiler_params=_cparams(("arbitrary",)),
        name="moe_combine",
    )(row_of_pair, h1, gates, ys)


def _route_plan(idx):
    n_pairs = idx.size
    tr = TR_MOE
    n_tiles = n_pairs // tr
    n_visits = n_tiles + N_EXPERTS - 1
    e_flat = idx.reshape(-1)
    onehot = (e_flat[:, None] == jnp.arange(N_EXPERTS, dtype=jnp.int32)[None, :]).astype(jnp.int32)
    csum = jnp.cumsum(onehot, axis=0)
    counts = csum[-1]
    ends = jnp.cumsum(counts)
    starts = ends - counts
    pos = jnp.sum((csum - onehot) * onehot, axis=1)
    row_of_pair = (starts[e_flat] + pos).astype(jnp.int32)
    nv = jnp.where(counts > 0, (ends - 1) // tr - starts // tr + 1, 0)
    vend = jnp.cumsum(nv)
    vid = jnp.arange(n_visits, dtype=jnp.int32)
    live = vid < vend[-1]
    ve = jnp.minimum(jnp.sum((vid[:, None] >= vend[None, :]).astype(jnp.int32), axis=1), N_EXPERTS - 1)
    vt = jnp.where(live, starts[ve] // tr + vid - (vend - nv)[ve], n_tiles - 1)
    lo = jnp.where(live, jnp.maximum(starts[ve], vt * tr), 0)
    hi = jnp.where(live, jnp.minimum(ends[ve], (vt + 1) * tr), 0)
    last_e = jnp.max(jnp.where(counts > 0, jnp.arange(N_EXPERTS, dtype=jnp.int32), 0))
    ve = jnp.where(live, ve, last_e)
    first = (live & (lo == vt * tr)).astype(jnp.int32)
    i32 = lambda a: a.astype(jnp.int32)
    return row_of_pair, (i32(vt), i32(ve), i32(lo), i32(hi), first)


def _cos_sin(positions, dim, theta):
    inv = 1.0 / (theta ** (jnp.arange(0, dim, 2, dtype=F32) / dim))
    ang = positions.astype(F32).reshape(-1, 1) * inv
    return jnp.concatenate([jnp.cos(ang), jnp.sin(ang)], axis=1)


def _gain3(g, half):
    return jnp.stack([g, jnp.roll(g, -half), jnp.roll(g, half)]).astype(F32)


def _band_bias(rel_table):
    nh = rel_table.shape[0]
    nk, nq = BAND_KEYS, TQ_BAND
    length = BAND_PROFILE
    n_lo = nk - 1 - B_LEFT_CHUNKS * CHUNK - B_REL_CLIP
    n_hi = length - n_lo - rel_table.shape[1]
    prof = jnp.concatenate([jnp.broadcast_to(rel_table[:, :1], (nh, n_lo)), rel_table,
                            jnp.broadcast_to(rel_table[:, -1:], (nh, n_hi))], axis=1)
    return pl.pallas_call(
        _band_bias_kernel,
        grid=(nh,),
        in_specs=[pl.BlockSpec((None, 1, length), lambda h: (h, 0, 0))],
        out_specs=pl.BlockSpec((None, nk, nq), lambda h: (h // 2, 0, h % 2)),
        out_shape=jax.ShapeDtypeStruct((nh // 2, nk, 2 * nq), F32),
        compiler_params=_cparams(("parallel",)),
        name="band_bias",
    )(prof.astype(F32).reshape(nh, 1, length))


def _band_bias_kernel(p_ref, o_ref):
    nk, nq = o_ref.shape
    length = p_ref.shape[1]
    x = jnp.broadcast_to(p_ref[...] * LOG2E, (8, length))
    sub = lax.broadcasted_iota(jnp.int32, (8, nq), 0)
    qchunk = lax.broadcasted_iota(jnp.int32, (8, nq), 1) // CHUNK

    def body(jb, carry):
        rolled = pltpu.roll(x, jb * 8 + (length - (nk - 1)), 1, stride=1, stride_axis=0)
        cdiff = qchunk + B_LEFT_CHUNKS - (jb * 8 + sub) // CHUNK
        allowed = (cdiff >= 0) & (cdiff <= B_LEFT_CHUNKS)
        o_ref[pl.ds(pl.multiple_of(jb * 8, 8), 8), :] = jnp.where(allowed, rolled[:, :nq], NEG)
        return carry

    lax.fori_loop(0, nk // 8, body, 0, unroll=8)


def _pad_last(a, width):
    return jnp.pad(a, [(0, 0)] * (a.ndim - 1) + [(0, width - a.shape[-1])])


def kernel(x, positions, ev_norm_mix, ev_w_in, ev_g_cq, ev_g_ckv, ev_w_uq, ev_w_ukv, ev_a_qnorm, ev_a_knorm, ev_b_qnorm, ev_b_knorm, ev_b_rel_bias, ev_w_out, ev_norm_ff, ev_ff_gate, ev_ff_up, ev_ff_down, od_norm_mix, od_w_qkv, od_c_qnorm, od_c_knorm, od_lam_q1, od_lam_k1, od_lam_q2, od_lam_k2, od_c_subnorm, od_w_out, od_norm_ff, od_router, od_ex_gate, od_ex_up, od_ex_down):
    batch, seq, d = x.shape
    t = batch * seq
    depth = ev_w_in.shape[0] + od_w_qkv.shape[0]
    h = x.reshape(t, d)
    row = lambda a: a.reshape(1, -1).astype(F32)
    cs_a = _cos_sin(positions, A_ROPE, A_THETA)
    cs_c = _cos_sin(positions, C_ROPE, ROPE_THETA)

    for layer in range(depth):
        i = layer // 2
        if layer % 2 == 0:
            w = ev_w_in[i]
            o1, o2, o3 = A_Q_LORA, A_Q_LORA + A_KV_LORA, A_Q_LORA + A_KV_LORA + A_ROPE
            zc = lambda n: jnp.zeros((d, n), F32)
            bw = B_HEADS * B_DH
            win = jnp.concatenate([w[:, :o2], zc(A_NOPE), w[:, o2:o3], zc(LANES - A_NOPE - A_ROPE),
                                   w[:, o3:o3 + 2 * bw]], 1).astype(BF16)
            wvbt = w[:, o3 + 2 * bw:].T.astype(BF16)
            wuq = _pad_last(ev_w_uq[i], LANES).reshape(A_Q_LORA, A_HEADS * LANES).astype(BF16)
            wuk = _pad_last(ev_w_ukv[i][:, :, :A_NOPE], LANES).reshape(A_KV_LORA, A_HEADS * LANES).astype(BF16)
            wuvt = ev_w_ukv[i][:, :, A_NOPE:].reshape(A_KV_LORA, A_HEADS * A_V).T.astype(BF16)
            sc_a = (A_NOPE + A_ROPE) ** -0.5 * LOG2E
            gq3 = _gain3(_pad_last(ev_a_qnorm[i], LANES) * sc_a, A_ROPE // 2)
            gk = jnp.tile(_pad_last(ev_a_knorm[i][:A_NOPE], LANES), A_HEADS).reshape(1, -1)
            gkr3 = _gain3(jnp.concatenate([jnp.zeros((A_NOPE,), F32), ev_a_knorm[i][A_NOPE:],
                                           jnp.zeros((LANES - A_NOPE - A_ROPE,), F32)]), A_ROPE // 2)
            gbq = jnp.tile(ev_b_qnorm[i], B_HEADS).reshape(1, -1) * (B_DH ** -0.5 * LOG2E)
            gbk = jnp.tile(ev_b_knorm[i], B_HEADS).reshape(1, -1)
            qa, ka, vat, qb, kb, vbt = _proj_even(
                h, row(ev_norm_mix[i]), win, wvbt, row(ev_g_cq[i]), row(ev_g_ckv[i]), wuq, wuk, wuvt,
                gq3, gk, gkr3, gbq, gbk, cs_a)
            ya = _attn_causal(_attn_mla_kernel, qa, ka, vat, (), batch=batch, seq=seq,
                              n_blocks=A_HEADS // 2, qk_width=2 * LANES)
            yb = _attn_band(qb, kb, vbt, _band_bias(ev_b_rel_bias[i]), batch=batch, seq=seq)
            wo = ev_w_out[i].astype(BF16)
            na = A_HEADS * A_V
            padf = lambda a: _pad_last(a, FF_DENSE_PAD).astype(BF16)
            wd = jnp.pad(ev_ff_down[i], ((0, FF_DENSE_PAD - FF_DENSE), (0, 0))).astype(BF16)
            h = _out_ffn(h, ya, yb, wo[:na], wo[na:], row(ev_norm_ff[i]),
                         padf(ev_ff_gate[i]), padf(ev_ff_up[i]), wd)
        else:
            lam_init = 0.8 - 0.6 * math.exp(-0.3 * layer)
            gq3 = _gain3(jnp.tile(od_c_qnorm[i], 2) * (C_DH ** -0.5 * LOG2E), C_ROPE // 2)
            gk3 = _gain3(jnp.tile(od_c_knorm[i], 2), C_ROPE // 2)
            c_mix = C_HEADS * 2 * C_DH
            wqkv = od_w_qkv[i]
            q, k, v = _proj_odd(h, row(od_norm_mix[i]), wqkv[:, :2 * c_mix].astype(BF16),
                                wqkv[:, 2 * c_mix:].T.astype(BF16), gq3, gk3, cs_c)
            extras = (row(od_lam_q1[i]), row(od_lam_k1[i]), row(od_lam_q2[i]), row(od_lam_k2[i]),
                      row(od_c_subnorm[i]))
            yc = _attn_causal(functools.partial(_attn_diff_kernel, lam_init=lam_init), q, k, v, extras,
                              batch=batch, seq=seq, n_blocks=C_HEADS, qk_width=LANES)
            h1, idx, gates = _out_router(h, yc, od_w_out[i].astype(BF16), row(od_norm_ff[i]),
                                         od_router[i].T.astype(F32))
            row_of_pair, visits = _route_plan(idx[:, :2])
            xs = _dispatch(row_of_pair, h1)
            ys = _moe(visits, xs, row(od_norm_ff[i]), od_ex_gate[i], od_ex_up[i], od_ex_down[i])
            h = _combine(row_of_pair, h1, gates, ys)
    return h.reshape(batch, seq, d)
```

```python
import functools
import math

import numpy as np
import jax
import jax.numpy as jnp
from jax import lax
from jax.experimental import pallas as pl
from jax.experimental.pallas import tpu as pltpu

F32 = jnp.float32
BF16 = jnp.bfloat16

D_MODEL = 1024
CHUNK = 64
EPS = 1e-6
LOG2E = 1.4426950408889634
NEG = -1e30

A_HEADS = 8
A_NOPE = 64
A_ROPE = 32
A_V = 64
A_Q_LORA = 256
A_KV_LORA = 128
A_THETA = 10000.0
B_HEADS = 8
B_DH = 64
B_LEFT_CHUNKS = 8
B_REL_CLIP = 128
C_HEADS = 8
C_DH = 64
C_ROPE = C_DH // 4
ROPE_THETA = 500000.0
FF_DENSE = 2752
N_EXPERTS = 8
FF_EXPERT = 3584

LANES = 128
ROW_TILES = D_MODEL // LANES
FF_DENSE_PAD = 2816
VMEM_LIMIT = 56 * 1024 * 1024

EV_KR_OFF = A_Q_LORA + A_KV_LORA
EV_QB_OFF = EV_KR_OFF + LANES
EV_KB_OFF = EV_QB_OFF + B_HEADS * B_DH

TM_PROJ = 512
PROJ_PARTS = 2
TQ_CAUSAL = 256
TQ_BAND = 256
BAND_KEYS = TQ_BAND + B_LEFT_CHUNKS * CHUNK
BAND_PROFILE = 1024
TM_DISPATCH = 512
DMA_UNROLL = 16
TR_MOE = 1024
MOE_SUB = 256
FC_MOE = 512


def _cparams(sem):
    return pltpu.CompilerParams(dimension_semantics=sem, vmem_limit_bytes=VMEM_LIMIT)


def _const_spec(shape):
    nd = len(shape)
    return pl.BlockSpec(shape, lambda *_: (0,) * nd, pipeline_mode=pl.Buffered(1))


def _rms_scale(x):
    return lax.rsqrt(jnp.mean(x * x, axis=-1, keepdims=True) + EPS)


def _lane_iota():
    return lax.broadcasted_iota(jnp.int32, (1, LANES), 1)


def _dot_nt(a, b):
    return lax.dot_general(a, b, (((1,), (1,)), ((), ())), preferred_element_type=F32)


def _rope_tables(cs, half, starts):
    tm = cs.shape[0]
    base = jnp.concatenate([cs, jnp.zeros((tm, LANES - 2 * half), F32)], axis=1)
    shifted = lambda s: base if s % LANES == 0 else pltpu.roll(base, s % LANES, 1)
    lane = _lane_iota()
    cos_t = jnp.ones((tm, LANES), F32)
    sin_a = jnp.zeros((tm, LANES), F32)
    sin_b = jnp.zeros((tm, LANES), F32)
    for s0 in starts:
        first = (lane >= s0) & (lane < s0 + half)
        second = (lane >= s0 + half) & (lane < s0 + 2 * half)
        at_s0 = shifted(s0)
        cos_t = jnp.where(first, at_s0, jnp.where(second, shifted(s0 + half), cos_t))
        sin_a = jnp.where(first, -shifted(s0 - half), sin_a)
        sin_b = jnp.where(second, at_s0, sin_b)
    return cos_t, sin_a, sin_b


def _store_gained(tables, g3_ref, tab_ref):
    for j in range(3):
        tab_ref[j] = tables[j] * g3_ref[j:j + 1, :]


def _rope(x, half, tab_ref, rows=slice(None)):
    return (x * tab_ref[0, rows, :] + pltpu.roll(x, LANES - half, 1) * tab_ref[1, rows, :]
            + pltpu.roll(x, half, 1) * tab_ref[2, rows, :])


def _proj_even_kernel(h_ref, gmix_ref, win_ref, wvbt_ref, gcq_ref, gckv_ref, wuq_ref, wuk_ref, wuvt_ref,
                      gq3_ref, gk_ref, gkr3_ref, gbq_ref, gbk_ref, cs_ref,
                      qa_ref, ka_ref, vat_ref, qb_ref, kb_ref, vbt_ref, tabq_ref, tabk_ref,
                      u_buf, q_buf, kn_buf):
    n_part = u_buf.shape[0]
    rows_part = u_buf.shape[1]
    lo = _lane_iota() < 64

    def project(j):
        rows = slice(j * rows_part, (j + 1) * rows_part)
        x = h_ref[rows, :]
        hn = (x * _rms_scale(x) * gmix_ref[...]).astype(BF16)
        u = jnp.dot(hn, win_ref[...], preferred_element_type=F32)
        u_buf[j] = u
        vbt_ref[:, rows] = _dot_nt(wvbt_ref[...], hn).astype(BF16)
        cq = u[:, 0:A_Q_LORA]
        cqn = (cq * _rms_scale(cq) * gcq_ref[...]).astype(BF16)
        q_buf[j] = jnp.dot(cqn, wuq_ref[...], preferred_element_type=F32)
        ckv = u[:, A_Q_LORA:A_Q_LORA + A_KV_LORA]
        ckvn = (ckv * _rms_scale(ckv) * gckv_ref[...]).astype(BF16)
        kn_buf[j] = jnp.dot(ckvn, wuk_ref[...], preferred_element_type=F32)
        vat_ref[:, rows] = _dot_nt(wuvt_ref[...], ckvn).astype(BF16)

    project(0)
    tables = _rope_tables(cs_ref[...], A_ROPE // 2, (A_NOPE,))
    _store_gained(tables, gq3_ref, tabq_ref)
    _store_gained(tables, gkr3_ref, tabk_ref)

    def pair_norm(blk, g):
        sq = blk * blk
        tot = jnp.sum(sq, axis=-1, keepdims=True)
        s_lo = jnp.sum(jnp.where(lo, sq, 0.0), axis=-1, keepdims=True)
        sc = jnp.where(lo, lax.rsqrt(s_lo * (1.0 / B_DH) + EPS),
                       lax.rsqrt((tot - s_lo) * (1.0 / B_DH) + EPS))
        return (blk * sc * g).astype(BF16)

    for j in range(n_part):
        if j + 1 < n_part:
            project(j + 1)
        rows = slice(j * rows_part, (j + 1) * rows_part)
        kr = u_buf[j, :, EV_KR_OFF:EV_KR_OFF + LANES]
        kr_s = lax.rsqrt(jnp.sum(kr * kr, axis=-1, keepdims=True) * (1.0 / A_ROPE) + EPS)
        krot = _rope(kr, A_ROPE // 2, tabk_ref, rows) * kr_s
        for hd in range(A_HEADS):
            sl = slice(hd * LANES, (hd + 1) * LANES)
            blk = q_buf[j, :, sl]
            sq = blk * blk
            tot = jnp.sum(sq, axis=-1, keepdims=True)
            ssn = jnp.sum(jnp.where(lo, sq, 0.0), axis=-1, keepdims=True)
            sc = jnp.where(lo, lax.rsqrt(ssn * (1.0 / A_NOPE) + EPS),
                           lax.rsqrt((tot - ssn) * (1.0 / A_ROPE) + EPS))
            qa_ref[rows, sl] = (_rope(blk, A_ROPE // 2, tabq_ref, rows) * sc).astype(BF16)
            kb_ = kn_buf[j, :, sl]
            ks = lax.rsqrt(jnp.sum(kb_ * kb_, axis=-1, keepdims=True) * (1.0 / A_NOPE) + EPS)
            ka_ref[rows, sl] = (kb_ * ks * gk_ref[:, sl] + krot).astype(BF16)
        for p in range(B_HEADS // 2):
            sl = slice(p * LANES, (p + 1) * LANES)
            qb_ref[rows, sl] = pair_norm(u_buf[j, :, EV_QB_OFF + p * LANES:EV_QB_OFF + (p + 1) * LANES],
                                         gbq_ref[:, sl])
            kb_ref[rows, sl] = pair_norm(u_buf[j, :, EV_KB_OFF + p * LANES:EV_KB_OFF + (p + 1) * LANES],
                                         gbk_ref[:, sl])


def _proj_even(h, gmix, win, wvbt, gcq, gckv, wuq, wuk, wuvt, gq3, gk, gkr3, gbq, gbk, cs):
    t = h.shape[0]
    tm = TM_PROJ
    row = lambda w: pl.BlockSpec((tm, w), lambda i: (i, 0))
    col = lambda w: pl.BlockSpec((w, tm), lambda i: (0, i))
    rshape = lambda w: jax.ShapeDtypeStruct((t, w), BF16)
    cshape = lambda w: jax.ShapeDtypeStruct((w, t), BF16)
    consts = (gmix, win, wvbt, gcq, gckv, wuq, wuk, wuvt, gq3, gk, gkr3, gbq, gbk)
    return pl.pallas_call(
        _proj_even_kernel,
        grid=(t // tm,),
        in_specs=[row(D_MODEL)] + [_const_spec(c.shape) for c in consts] + [row(cs.shape[1])],
        out_specs=[row(1024), row(1024), col(512), row(512), row(512), col(512)],
        out_shape=[rshape(1024), rshape(1024), cshape(512), rshape(512), rshape(512), cshape(512)],
        scratch_shapes=[pltpu.VMEM((3, tm, LANES), F32)] * 2
        + [pltpu.VMEM((PROJ_PARTS, tm // PROJ_PARTS, w), F32) for w in (win.shape[1], 1024, 1024)],
        compiler_params=_cparams(("parallel",)),
        name="proj_even",
    )(h, *consts, cs)


def _causal_scores(q_ref, k_ref, s_ref, c, *, split):
    tq = TQ_CAUSAL
    q = q_ref[c * tq:(c + 1) * tq, :]
    zero = jnp.zeros_like(q[:, :LANES])
    if split:
        q2 = jnp.concatenate([jnp.concatenate([q[:, :LANES], zero], axis=1),
                              jnp.concatenate([zero, q[:, LANES:]], axis=1)], axis=0)
    else:
        lo = _lane_iota() < 64
        q2 = jnp.concatenate([jnp.where(lo, q, zero), jnp.where(lo, zero, q)], axis=0)
    kchunk = lax.broadcasted_iota(jnp.int32, (tq, 2 * tq), 0) // CHUNK
    qchunk = (lax.broadcasted_iota(jnp.int32, (tq, 2 * tq), 1) % tq) // CHUNK
    n_past = c * tq
    s_d = jnp.where(kchunk <= qchunk, _dot_nt(k_ref[n_past:n_past + tq, :], q2), NEG)
    s_ref[n_past:n_past + tq, :] = s_d
    mx = jnp.max(s_d, axis=0, keepdims=True)
    if c:
        s_p = _dot_nt(k_ref[0:n_past, :], q2)
        s_ref[0:n_past, :] = s_p
        mx = jnp.maximum(mx, jnp.max(s_p, axis=0, keepdims=True))
    return mx


def _causal_values(vt_ref, s_ref, c, mx):
    n_keys = (c + 1) * TQ_CAUSAL
    p = jnp.exp2(s_ref[0:n_keys, :] - mx)
    o = jnp.dot(vt_ref[:, 0:n_keys], p.astype(BF16), preferred_element_type=F32)
    return o * (1.0 / jnp.sum(p, axis=0, keepdims=True))


def _causal_tiles(q_ref, k_ref, vt_ref, s_buf, *, split):
    tq = TQ_CAUSAL
    n = q_ref.shape[0] // tq
    mx = _causal_scores(q_ref, k_ref, s_buf.at[0], 0, split=split)
    for c in range(n):
        mx_next = None
        if c + 1 < n:
            mx_next = _causal_scores(q_ref, k_ref, s_buf.at[(c + 1) % 2], c + 1, split=split)
        o = _causal_values(vt_ref, s_buf.at[c % 2], c, mx)
        yield c, o[:, :tq], o[:, tq:]
        mx = mx_next


def _row_iota(n):
    return lax.broadcasted_iota(jnp.int32, (LANES, n), 0)


def _attn_mla_kernel(q_ref, k_ref, vt_ref, o_ref, s_buf):
    tq = TQ_CAUSAL
    for c, o0, o1 in _causal_tiles(q_ref, k_ref, vt_ref, s_buf, split=True):
        o_ref[c * tq:(c + 1) * tq, :] = jnp.where(_row_iota(tq) < 64, o0, o1).T.astype(o_ref.dtype)


def _attn_diff_kernel(q_ref, k_ref, vt_ref, lq1_ref, lk1_ref, lq2_ref, lk2_ref, gsub_ref, o_ref, s_buf,
                      *, lam_init):
    tq = TQ_CAUSAL
    lam = (jnp.exp(jnp.sum(lq1_ref[...] * lk1_ref[...], axis=-1, keepdims=True))
           - jnp.exp(jnp.sum(lq2_ref[...] * lk2_ref[...], axis=-1, keepdims=True)) + lam_init)
    for c, o0, o1 in _causal_tiles(q_ref, k_ref, vt_ref, s_buf, split=False):
        o = (o0 - lam * o1).T
        o = o * _rms_scale(o) * gsub_ref[...] * (1.0 - lam_init)
        o_ref[c * tq:(c + 1) * tq, :] = o.astype(o_ref.dtype)


def _attn_causal(kernel, q, k, vt, extras, *, batch, seq, n_blocks, qk_width):
    in_specs = [pl.BlockSpec((seq, qk_width), lambda b, p: (b, p)),
                pl.BlockSpec((seq, qk_width), lambda b, p: (b, p)),
                pl.BlockSpec((LANES, seq), lambda b, p: (p, b))]
    in_specs += [_const_spec(e.shape) for e in extras]
    return pl.pallas_call(
        kernel,
        grid=(batch, n_blocks),
        in_specs=in_specs,
        out_specs=pl.BlockSpec((seq, LANES), lambda b, p: (b, p)),
        out_shape=jax.ShapeDtypeStruct((batch * seq, n_blocks * LANES), BF16),
        scratch_shapes=[pltpu.VMEM((2, seq, 2 * TQ_CAUSAL), F32)],
        compiler_params=_cparams(("parallel", "parallel")),
        name=kernel.func.__name__ if isinstance(kernel, functools.partial) else kernel.__name__,
    )(q, k, vt, *extras)


def _attn_band_kernel(q_ref, k_ref, vt_ref, bias_ref, o_ref, s_buf):
    tq = TQ_BAND
    n = q_ref.shape[0] // tq
    lo = _lane_iota() < 64

    def window(c):
        end = (c + 1) * tq
        return max(0, end - BAND_KEYS), end

    def scores(c):
        start, end = window(c)
        q = q_ref[c * tq:(c + 1) * tq, :]
        zero = jnp.zeros_like(q)
        q2 = jnp.concatenate([jnp.where(lo, q, zero), jnp.where(lo, zero, q)], axis=0)
        s = _dot_nt(k_ref[start:end, :], q2) + bias_ref[BAND_KEYS - (end - start):, :]
        s_buf[c % 2, 0:end - start, :] = s
        return jnp.max(s, axis=0, keepdims=True)

    mx = scores(0)
    for c in range(n):
        mx_next = scores(c + 1) if c + 1 < n else None
        start, end = window(c)
        p = jnp.exp2(s_buf[c % 2, 0:end - start, :] - mx)
        o = jnp.dot(vt_ref[:, start:end], p.astype(BF16), preferred_element_type=F32)
        o = o * (1.0 / jnp.sum(p, axis=0, keepdims=True))
        o_ref[c * tq:(c + 1) * tq, :] = jnp.where(_row_iota(tq) < 64, o[:, :tq], o[:, tq:]).T.astype(o_ref.dtype)
        mx = mx_next


def _attn_band(q, k, vt, bias, *, batch, seq):
    npair = B_HEADS // 2
    return pl.pallas_call(
        _attn_band_kernel,
        grid=(batch, npair),
        in_specs=[pl.BlockSpec((seq, LANES), lambda b, p: (b, p)),
                  pl.BlockSpec((seq, LANES), lambda b, p: (b, p)),
                  pl.BlockSpec((LANES, seq), lambda b, p: (p, b)),
                  pl.BlockSpec((None, BAND_KEYS, 2 * TQ_BAND), lambda b, p: (p, 0, 0))],
        out_specs=pl.BlockSpec((seq, LANES), lambda b, p: (b, p)),
        out_shape=jax.ShapeDtypeStruct((batch * seq, npair * LANES), BF16),
        scratch_shapes=[pltpu.VMEM((2, BAND_KEYS, 2 * TQ_BAND), F32)],
        compiler_params=_cparams(("parallel", "parallel")),
        name="attn_band",
    )(q, k, vt, bias)


def _silu(g):
    return g / (1.0 + jnp.exp(-g))


def _out_ffn_kernel(h_ref, ya_ref, yb_ref, woa_ref, wob_ref, gff_ref, wg_ref, wu_ref, wd_ref, o_ref):
    h1 = (h_ref[...] + jnp.dot(ya_ref[...], woa_ref[...], preferred_element_type=F32)
          + jnp.dot(yb_ref[...], wob_ref[...], preferred_element_type=F32))
    xn = (h1 * _rms_scale(h1) * gff_ref[...]).astype(BF16)
    g = jnp.dot(xn, wg_ref[...], preferred_element_type=F32)
    u = jnp.dot(xn, wu_ref[...], preferred_element_type=F32)
    a = (_silu(g) * u).astype(BF16)
    o_ref[...] = h1 + jnp.dot(a, wd_ref[...], preferred_element_type=F32)


def _out_ffn(h, ya, yb, woa, wob, gff, wg, wu, wd):
    t = h.shape[0]
    tm = TM_PROJ
    row = lambda w: pl.BlockSpec((tm, w), lambda i: (i, 0))
    return pl.pallas_call(
        _out_ffn_kernel,
        grid=(t // tm,),
        in_specs=[row(D_MODEL), row(ya.shape[1]), row(yb.shape[1])]
        + [_const_spec(w.shape) for w in (woa, wob, gff, wg, wu, wd)],
        out_specs=row(D_MODEL),
        out_shape=jax.ShapeDtypeStruct((t, D_MODEL), F32),
        compiler_params=_cparams(("parallel",)),
        name="out_ffn",
    )(h, ya, yb, woa, wob, gff, wg, wu, wd)


def _proj_odd_kernel(h_ref, gmix_ref, w_ref, wvt_ref, gq3_ref, gk3_ref, cs_ref, q_ref, k_ref, vt_ref,
                     tabq_ref, tabk_ref, u_buf):
    n_part = u_buf.shape[0]
    rows_part = u_buf.shape[1]
    lo = _lane_iota() < 64
    width = C_HEADS * 2 * C_DH

    def project(j):
        rows = slice(j * rows_part, (j + 1) * rows_part)
        x = h_ref[rows, :]
        hn = (x * _rms_scale(x) * gmix_ref[...]).astype(BF16)
        u_buf[j] = jnp.dot(hn, w_ref[...], preferred_element_type=F32)
        vt_ref[:, rows] = _dot_nt(wvt_ref[...], hn).astype(BF16)

    project(0)
    tables = _rope_tables(cs_ref[...], C_ROPE // 2, (0, C_DH))
    _store_gained(tables, gq3_ref, tabq_ref)
    _store_gained(tables, gk3_ref, tabk_ref)

    def norm_rope(blk, tab_ref, rows):
        sq = blk * blk
        tot = jnp.sum(sq, axis=-1, keepdims=True)
        s_lo = jnp.sum(jnp.where(lo, sq, 0.0), axis=-1, keepdims=True)
        sc = jnp.where(lo, lax.rsqrt(s_lo * (1.0 / C_DH) + EPS),
                       lax.rsqrt((tot - s_lo) * (1.0 / C_DH) + EPS))
        return (_rope(blk, C_ROPE // 2, tab_ref, rows) * sc).astype(BF16)

    for j in range(n_part):
        if j + 1 < n_part:
            project(j + 1)
        rows = slice(j * rows_part, (j + 1) * rows_part)
        for hd in range(C_HEADS):
            sl = slice(hd * LANES, (hd + 1) * LANES)
            q_ref[rows, sl] = norm_rope(u_buf[j, :, sl], tabq_ref, rows)
            k_ref[rows, sl] = norm_rope(u_buf[j, :, width + hd * LANES:width + (hd + 1) * LANES],
                                        tabk_ref, rows)


def _proj_odd(h, gmix, w, wvt, gq3, gk3, cs):
    t = h.shape[0]
    tm = TM_PROJ
    row = lambda wd: pl.BlockSpec((tm, wd), lambda i: (i, 0))
    width = C_HEADS * 2 * C_DH
    consts = (gmix, w, wvt, gq3, gk3)
    return pl.pallas_call(
        _proj_odd_kernel,
        grid=(t // tm,),
        in_specs=[row(D_MODEL)] + [_const_spec(c.shape) for c in consts] + [row(cs.shape[1])],
        out_specs=[row(width), row(width), pl.BlockSpec((width, tm), lambda i: (0, i))],
        out_shape=[jax.ShapeDtypeStruct((t, width), BF16), jax.ShapeDtypeStruct((t, width), BF16),
                   jax.ShapeDtypeStruct((width, t), BF16)],
        scratch_shapes=[pltpu.VMEM((3, tm, LANES), F32)] * 2
        + [pltpu.VMEM((PROJ_PARTS, tm // PROJ_PARTS, 2 * width), F32)],
        compiler_params=_cparams(("parallel",)),
        name="proj_odd",
    )(h, *consts, cs)


def _tiled_rows(ref, start, n):
    return jnp.concatenate([ref[pl.ds(start * ROW_TILES + s, n, stride=ROW_TILES), :]
                            for s in range(ROW_TILES)], axis=1)


def _store_tiled(ref, start, val, accumulate=False):
    for s in range(ROW_TILES):
        idx = pl.ds(start * ROW_TILES + s, val.shape[0], stride=ROW_TILES)
        piece = val[:, s * LANES:(s + 1) * LANES]
        ref[idx, :] = ref[idx, :] + piece if accumulate else piece


def _out_router_kernel(h_ref, y_ref, wo_ref, gff_ref, wr_ref, h1_ref, idx_ref, gate_ref):
    h1 = h_ref[...] + jnp.dot(y_ref[...], wo_ref[...], preferred_element_type=F32)
    _store_tiled(h1_ref, 0, h1)
    xn = h1 * _rms_scale(h1) * gff_ref[...]
    logits = [jnp.sum(xn * wr_ref[e:e + 1, :], axis=-1, keepdims=True) for e in range(N_EXPERTS)]
    m1 = functools.reduce(jnp.maximum, logits)
    idx1 = functools.reduce(jnp.minimum,
                            [jnp.where(logits[e] == m1, e, N_EXPERTS) for e in range(N_EXPERTS)])
    rest = [jnp.where(idx1 == e, -jnp.inf, logits[e]) for e in range(N_EXPERTS)]
    m2 = functools.reduce(jnp.maximum, rest)
    idx2 = functools.reduce(jnp.minimum,
                            [jnp.where(rest[e] == m2, e, N_EXPERTS) for e in range(N_EXPERTS)])
    e2 = jnp.exp(m2 - m1)
    g1 = 1.0 / (1.0 + e2)
    g2 = e2 * g1
    lane = _lane_iota()
    idx_ref[...] = jnp.where(lane == 0, idx1, jnp.where(lane == 1, idx2, 0))
    gate_ref[...] = jnp.where(lane == 0, g1, jnp.where(lane == 1, g2, 0.0))


def _out_router(h, y, wo, gff, wr):
    t = h.shape[0]
    tm = TM_PROJ
    row = lambda w: pl.BlockSpec((tm, w), lambda i: (i, 0))
    return pl.pallas_call(
        _out_router_kernel,
        grid=(t // tm,),
        in_specs=[row(D_MODEL), row(y.shape[1]), _const_spec(wo.shape), _const_spec(gff.shape),
                  _const_spec(wr.shape)],
        out_specs=[pl.BlockSpec((tm * ROW_TILES, LANES), lambda i: (i, 0)), row(LANES), row(LANES)],
        out_shape=[jax.ShapeDtypeStruct((t * ROW_TILES, LANES), F32),
                   jax.ShapeDtypeStruct((t, LANES), jnp.int32), jax.ShapeDtypeStruct((t, LANES), F32)],
        compiler_params=_cparams(("parallel",)),
        name="out_router",
    )(h, y, wo, gff, wr)


def _row_copy(src_ref, src_row, dst_ref, dst_row, sem):
    tile = lambda r: pl.ds(pl.multiple_of(r * ROW_TILES, ROW_TILES), ROW_TILES)
    return pltpu.make_async_copy(src_ref.at[tile(src_row)], dst_ref.at[tile(dst_row)], sem)


def _dispatch_kernel(row_ref, x_ref, xs_ref, sem):
    tm = x_ref.shape[0] // ROW_TILES
    base = pl.program_id(0) * tm

    def issue(r, carry):
        for k in range(2):
            _row_copy(x_ref, r, xs_ref, row_ref[2 * (base + r) + k], sem).start(priority=k)
        return carry

    lax.fori_loop(0, tm, issue, 0, unroll=DMA_UNROLL)
    for k in range(2):
        pltpu.make_async_copy(x_ref, xs_ref.at[pl.ds(0, tm * ROW_TILES)], sem).wait()


def _dispatch(row_of_pair, x):
    t = x.shape[0] // ROW_TILES
    tm = TM_DISPATCH
    grid_spec = pltpu.PrefetchScalarGridSpec(
        num_scalar_prefetch=1,
        grid=(t // tm,),
        in_specs=[pl.BlockSpec((tm * ROW_TILES, LANES), lambda i, rows: (i, 0))],
        out_specs=pl.BlockSpec(memory_space=pl.ANY),
        scratch_shapes=[pltpu.SemaphoreType.DMA(())],
    )
    return pl.pallas_call(
        _dispatch_kernel,
        grid_spec=grid_spec,
        out_shape=jax.ShapeDtypeStruct((2 * t * ROW_TILES, LANES), x.dtype),
        compiler_params=_cparams(("arbitrary",)),
        name="moe_dispatch",
    )(row_of_pair, x)


def _moe_kernel(vt_ref, ve_ref, lo_ref, hi_ref, first_ref, x_ref, gff_ref, wg_ref, wu_ref, wd_ref, o_ref,
                xb_ref, acc_ref):
    v = pl.program_id(0)
    c = pl.program_id(1)
    tr = xb_ref.shape[0]

    lo = lo_ref[v]
    hi = hi_ref[v]
    base = vt_ref[v] * tr
    whole_tile = (lo == base) & (hi == base + tr)

    @pl.when((c == 0) & (first_ref[v] == 1) & jnp.logical_not(whole_tile))
    def _():
        o_ref[...] = jnp.zeros_like(o_ref)

    @pl.when(c == 0)
    def _():
        x = _tiled_rows(x_ref, 0, tr)
        xb_ref[...] = (x * _rms_scale(x) * gff_ref[...]).astype(BF16)
        acc_ref[...] = jnp.zeros_like(acc_ref)

    def expert_rows(off, n, whole):
        rows = slice(off, off + n)
        x = xb_ref[rows, :]
        g = jnp.dot(x, wg_ref[0].astype(BF16), preferred_element_type=F32)
        u = jnp.dot(x, wu_ref[0].astype(BF16), preferred_element_type=F32)
        a = (_silu(g) * u).astype(BF16)
        y = jnp.dot(a, wd_ref[0].astype(BF16), preferred_element_type=F32)
        if not whole:
            r = base + off + lax.broadcasted_iota(jnp.int32, (n, 1), 0)
            y = jnp.where((r >= lo) & (r < hi), y, 0.0)
        acc_ref[rows, :] += y

    @pl.when(whole_tile)
    def _():
        expert_rows(0, tr, True)

    for s in range(tr // MOE_SUB):
        @pl.when(jnp.logical_not(whole_tile) & (base + s * MOE_SUB < hi) & (base + (s + 1) * MOE_SUB > lo))
        def _(s=s):
            expert_rows(s * MOE_SUB, MOE_SUB, False)

    last = c == pl.num_programs(1) - 1

    @pl.when(last & whole_tile)
    def _():
        _store_tiled(o_ref, 0, acc_ref[...])

    @pl.when(last & jnp.logical_not(whole_tile) & (lo < hi))
    def _():
        _store_tiled(o_ref, 0, acc_ref[...], accumulate=True)


def _moe(visits, xs, gff, wg, wu, wd):
    r = xs.shape[0] // ROW_TILES
    tr, fc = TR_MOE, FC_MOE
    nc = FF_EXPERT // fc
    n_visits = r // tr + N_EXPERTS - 1

    def chunk(v, c, lo, hi):
        return jnp.where(lo[v] < hi[v], c, nc - 1)

    tile = lambda v, c, vt, ve, lo, hi, first: (vt[v], 0)
    wcol = lambda v, c, vt, ve, lo, hi, first: (ve[v], 0, chunk(v, c, lo, hi))
    wrow = lambda v, c, vt, ve, lo, hi, first: (ve[v], chunk(v, c, lo, hi), 0)
    grid_spec = pltpu.PrefetchScalarGridSpec(
        num_scalar_prefetch=5,
        grid=(n_visits, nc),
        in_specs=[pl.BlockSpec((tr * ROW_TILES, LANES), tile),
                  pl.BlockSpec((1, D_MODEL), lambda v, c, vt, ve, lo, hi, first: (0, 0)),
                  pl.BlockSpec((1, D_MODEL, fc), wcol),
                  pl.BlockSpec((1, D_MODEL, fc), wcol),
                  pl.BlockSpec((1, fc, D_MODEL), wrow)],
        out_specs=pl.BlockSpec((tr * ROW_TILES, LANES), tile),
        scratch_shapes=[pltpu.VMEM((tr, D_MODEL), BF16), pltpu.VMEM((tr, D_MODEL), F32)],
    )
    return pl.pallas_call(
        _moe_kernel,
        grid_spec=grid_spec,
        out_shape=jax.ShapeDtypeStruct((r * ROW_TILES, LANES), F32),
        compiler_params=_cparams(("arbitrary", "arbitrary")),
        name="moe_grouped",
    )(*visits, xs, gff, wg, wu, wd)


def _combine_kernel(row_ref, h_ref, gate_ref, ys_ref, o_ref, buf, sem):
    tm = gate_ref.shape[0]
    base = pl.program_id(0) * tm

    def issue(r, carry):
        for k in range(2):
            _row_copy(ys_ref, row_ref[2 * (base + r) + k], buf, k * tm + r, sem).start(priority=k)
        return carry

    lax.fori_loop(0, tm, issue, 0, unroll=DMA_UNROLL)
    pltpu.make_async_copy(ys_ref.at[pl.ds(0, 2 * tm * ROW_TILES)], buf, sem).wait()
    gate = gate_ref[...]
    o_ref[...] = (_tiled_rows(h_ref, 0, tm) + gate[:, 0:1] * _tiled_rows(buf, 0, tm)
                  + gate[:, 1:2] * _tiled_rows(buf, tm, tm))


def _combine(row_of_pair, h1, gates, ys):
    t = gates.shape[0]
    d = D_MODEL
    tm = TM_DISPATCH
    grid_spec = pltpu.PrefetchScalarGridSpec(
        num_scalar_prefetch=1,
        grid=(t // tm,),
        in_specs=[pl.BlockSpec((tm * ROW_TILES, LANES), lambda i, rows: (i, 0)),
                  pl.BlockSpec((tm, LANES), lambda i, rows: (i, 0)),
                  pl.BlockSpec(memory_space=pl.ANY)],
        out_specs=pl.BlockSpec((tm, d), lambda i, rows: (i, 0)),
        scratch_shapes=[pltpu.VMEM((2 * tm * ROW_TILES, LANES), F32), pltpu.SemaphoreType.DMA(())],
    )
    return pl.pallas_call(
        _combine_kernel,
        grid_spec=grid_spec,
        out_shape=jax.ShapeDtypeStruct((t, d), F32),
        compiler_params=_cparams(("arbitrary",)),
        name="moe_combine",
    )(row_of_pair, h1, gates, ys)


def _route_plan(idx):
    n_pairs = idx.size
    tr = TR_MOE
    n_tiles = n_pairs // tr
    n_visits = n_tiles + N_EXPERTS - 1
    e_flat = idx.reshape(-1)
    onehot = (e_flat[:, None] == jnp.arange(N_EXPERTS, dtype=jnp.int32)[None, :]).astype(jnp.int32)
    csum = jnp.cumsum(onehot, axis=0)
    counts = csum[-1]
    ends = jnp.cumsum(counts)
    starts = ends - counts
    pos = jnp.sum((csum - onehot) * onehot, axis=1)
    row_of_pair = (starts[e_flat] + pos).astype(jnp.int32)
    nv = jnp.where(counts > 0, (ends - 1) // tr - starts // tr + 1, 0)
    vend = jnp.cumsum(nv)
    vid = jnp.arange(n_visits, dtype=jnp.int32)
    live = vid < vend[-1]
    ve = jnp.minimum(jnp.sum((vid[:, None] >= vend[None, :]).astype(jnp.int32), axis=1), N_EXPERTS - 1)
    vt = jnp.where(live, starts[ve] // tr + vid - (vend - nv)[ve], n_tiles - 1)
    lo = jnp.where(live, jnp.maximum(starts[ve], vt * tr), 0)
    hi = jnp.where(live, jnp.minimum(ends[ve], (vt + 1) * tr), 0)
    last_e = jnp.max(jnp.where(counts > 0, jnp.arange(N_EXPERTS, dtype=jnp.int32), 0))
    ve = jnp.where(live, ve, last_e)
    first = (live & (lo == vt * tr)).astype(jnp.int32)
    i32 = lambda a: a.astype(jnp.int32)
    return row_of_pair, (i32(vt), i32(ve), i32(lo), i32(hi), first)


def _cos_sin(positions, dim, theta):
    inv = 1.0 / (theta ** (jnp.arange(0, dim, 2, dtype=F32) / dim))
    ang = positions.astype(F32).reshape(-1, 1) * inv
    return jnp.concatenate([jnp.cos(ang), jnp.sin(ang)], axis=1)


def _gain3(g, half):
    return jnp.stack([g, jnp.roll(g, -half), jnp.roll(g, half)]).astype(F32)


def _band_bias(rel_table):
    nh = rel_table.shape[0]
    nk, nq = BAND_KEYS, TQ_BAND
    length = BAND_PROFILE
    n_lo = nk - 1 - B_LEFT_CHUNKS * CHUNK - B_REL_CLIP
    n_hi = length - n_lo - rel_table.shape[1]
    prof = jnp.concatenate([jnp.broadcast_to(rel_table[:, :1], (nh, n_lo)), rel_table,
                            jnp.broadcast_to(rel_table[:, -1:], (nh, n_hi))], axis=1)
    return pl.pallas_call(
        _band_bias_kernel,
        grid=(nh,),
        in_specs=[pl.BlockSpec((None, 1, length), lambda h: (h, 0, 0))],
        out_specs=pl.BlockSpec((None, nk, nq), lambda h: (h // 2, 0, h % 2)),
        out_shape=jax.ShapeDtypeStruct((nh // 2, nk, 2 * nq), F32),
        compiler_params=_cparams(("parallel",)),
        name="band_bias",
    )(prof.astype(F32).reshape(nh, 1, length))


def _band_bias_kernel(p_ref, o_ref):
    nk, nq = o_ref.shape
    length = p_ref.shape[1]
    x = jnp.broadcast_to(p_ref[...] * LOG2E, (8, length))
    sub = lax.broadcasted_iota(jnp.int32, (8, nq), 0)
    qchunk = lax.broadcasted_iota(jnp.int32, (8, nq), 1) // CHUNK

    def body(jb, carry):
        rolled = pltpu.roll(x, jb * 8 + (length - (nk - 1)), 1, stride=1, stride_axis=0)
        cdiff = qchunk + B_LEFT_CHUNKS - (jb * 8 + sub) // CHUNK
        allowed = (cdiff >= 0) & (cdiff <= B_LEFT_CHUNKS)
        o_ref[pl.ds(pl.multiple_of(jb * 8, 8), 8), :] = jnp.where(allowed, rolled[:, :nq], NEG)
        return carry

    lax.fori_loop(0, nk // 8, body, 0, unroll=8)


def _pad_last(a, width):
    return jnp.pad(a, [(0, 0)] * (a.ndim - 1) + [(0, width - a.shape[-1])])


def kernel(x, positions, ev_norm_mix, ev_w_in, ev_g_cq, ev_g_ckv, ev_w_uq, ev_w_ukv, ev_a_qnorm, ev_a_knorm, ev_b_qnorm, ev_b_knorm, ev_b_rel_bias, ev_w_out, ev_norm_ff, ev_ff_gate, ev_ff_up, ev_ff_down, od_norm_mix, od_w_qkv, od_c_qnorm, od_c_knorm, od_lam_q1, od_lam_k1, od_lam_q2, od_lam_k2, od_c_subnorm, od_w_out, od_norm_ff, od_router, od_ex_gate, od_ex_up, od_ex_down):
    batch, seq, d = x.shape
    t = batch * seq
    depth = ev_w_in.shape[0] + od_w_qkv.shape[0]
    h = x.reshape(t, d)
    row = lambda a: a.reshape(1, -1).astype(F32)
    cs_a = _cos_sin(positions, A_ROPE, A_THETA)
    cs_c = _cos_sin(positions, C_ROPE, ROPE_THETA)

    for layer in range(depth):
        i = layer // 2
        if layer % 2 == 0:
            w = ev_w_in[i]
            o1, o2, o3 = A_Q_LORA, A_Q_LORA + A_KV_LORA, A_Q_LORA + A_KV_LORA + A_ROPE
            zc = lambda n: jnp.zeros((d, n), F32)
            bw = B_HEADS * B_DH
            win = jnp.concatenate([w[:, :o2], zc(A_NOPE), w[:, o2:o3], zc(LANES - A_NOPE - A_ROPE),
                                   w[:, o3:o3 + 2 * bw]], 1).astype(BF16)
            wvbt = w[:, o3 + 2 * bw:].T.astype(BF16)
            wuq = _pad_last(ev_w_uq[i], LANES).reshape(A_Q_LORA, A_HEADS * LANES).astype(BF16)
            wuk = _pad_last(ev_w_ukv[i][:, :, :A_NOPE], LANES).reshape(A_KV_LORA, A_HEADS * LANES).astype(BF16)
            wuvt = ev_w_ukv[i][:, :, A_NOPE:].reshape(A_KV_LORA, A_HEADS * A_V).T.astype(BF16)
            sc_a = (A_NOPE + A_ROPE) ** -0.5 * LOG2E
            gq3 = _gain3(_pad_last(ev_a_qnorm[i], LANES) * sc_a, A_ROPE // 2)
            gk = jnp.tile(_pad_last(ev_a_knorm[i][:A_NOPE], LANES), A_HEADS).reshape(1, -1)
            gkr3 = _gain3(jnp.concatenate([jnp.zeros((A_NOPE,), F32), ev_a_knorm[i][A_NOPE:],
                                           jnp.zeros((LANES - A_NOPE - A_ROPE,), F32)]), A_ROPE // 2)
            gbq = jnp.tile(ev_b_qnorm[i], B_HEADS).reshape(1, -1) * (B_DH ** -0.5 * LOG2E)
            gbk = jnp.tile(ev_b_knorm[i], B_HEADS).reshape(1, -1)
            qa, ka, vat, qb, kb, vbt = _proj_even(
                h, row(ev_norm_mix[i]), win, wvbt, row(ev_g_cq[i]), row(ev_g_ckv[i]), wuq, wuk, wuvt,
                gq3, gk, gkr3, gbq, gbk, cs_a)
            ya = _attn_causal(_attn_mla_kernel, qa, ka, vat, (), batch=batch, seq=seq,
                              n_blocks=A_HEADS // 2, qk_width=2 * LANES)
            yb = _attn_band(qb, kb, vbt, _band_bias(ev_b_rel_bias[i]), batch=batch, seq=seq)
            wo = ev_w_out[i].astype(BF16)
            na = A_HEADS * A_V
            padf = lambda a: _pad_last(a, FF_DENSE_PAD).astype(BF16)
            wd = jnp.pad(ev_ff_down[i], ((0, FF_DENSE_PAD - FF_DENSE), (0, 0))).astype(BF16)
            h = _out_ffn(h, ya, yb, wo[:na], wo[na:], row(ev_norm_ff[i]),
                         padf(ev_ff_gate[i]), padf(ev_ff_up[i]), wd)
        else:
            lam_init = 0.8 - 0.6 * math.exp(-0.3 * layer)
            gq3 = _gain3(jnp.tile(od_c_qnorm[i], 2) * (C_DH ** -0.5 * LOG2E), C_ROPE // 2)
            gk3 = _gain3(jnp.tile(od_c_knorm[i], 2), C_ROPE // 2)
            c_mix = C_HEADS * 2 * C_DH
            wqkv = od_w_qkv[i]
            q, k, v = _proj_odd(h, row(od_norm_mix[i]), wqkv[:, :2 * c_mix].astype(BF16),
                                wqkv[:, 2 * c_mix:].T.astype(BF16), gq3, gk3, cs_c)
            extras = (row(od_lam_q1[i]), row(od_lam_k1[i]), row(od_lam_q2[i]), row(od_lam_k2[i]),
                      row(od_c_subnorm[i]))
            yc = _attn_causal(functools.partial(_attn_diff_kernel, lam_init=lam_init), q, k, v, extras,
                              batch=batch, seq=seq, n_blocks=C_HEADS, qk_width=LANES)
            h1, idx, gates = _out_router(h, yc, od_w_out[i].astype(BF16), row(od_norm_ff[i]),
                                         od_router[i].T.astype(F32))
            row_of_pair, visits = _route_plan(idx[:, :2])
            xs = _dispatch(row_of_pair, h1)
            ys = _moe(visits, xs, row(od_norm_ff[i]), od_ex_gate[i], od_ex_up[i], od_ex_down[i])
            h = _combine(row_of_pair, h1, gates, ys)
    return h.reshape(batch, seq, d)
```

```python
import functools
import math

import numpy as np
import jax
import jax.numpy as jnp
from jax import lax
from jax.experimental import pallas as pl
from jax.experimental.pallas import tpu as pltpu

F32 = jnp.float32
BF16 = jnp.bfloat16

D_MODEL = 1024
CHUNK = 64
EPS = 1e-6
LOG2E = 1.4426950408889634
NEG = -1e30

A_HEADS = 8
A_NOPE = 64
A_ROPE = 32
A_V = 64
A_Q_LORA = 256
A_KV_LORA = 128
A_THETA = 10000.0
B_HEADS = 8
B_DH = 64
B_LEFT_CHUNKS = 8
B_REL_CLIP = 128
C_HEADS = 8
C_DH = 64
C_ROPE = C_DH // 4
ROPE_THETA = 500000.0
FF_DENSE = 2752
N_EXPERTS = 8
FF_EXPERT = 3584

LANES = 128
ROW_TILES = D_MODEL // LANES
FF_DENSE_PAD = 2816
VMEM_LIMIT = 56 * 1024 * 1024

EV_KR_OFF = A_Q_LORA + A_KV_LORA
EV_QB_OFF = EV_KR_OFF + LANES
EV_KB_OFF = EV_QB_OFF + B_HEADS * B_DH

TM_PROJ = 512
PROJ_PARTS = 2
TQ_CAUSAL = 256
TQ_BAND = 256
BAND_KEYS = TQ_BAND + B_LEFT_CHUNKS * CHUNK
BAND_PROFILE = 1024
TM_DISPATCH = 1024
DMA_UNROLL = 16
TR_MOE = 1024
MOE_SUB = 256
FC_MOE = 512


def _cparams(sem):
    return pltpu.CompilerParams(dimension_semantics=sem, vmem_limit_bytes=VMEM_LIMIT)


def _const_spec(shape):
    nd = len(shape)
    return pl.BlockSpec(shape, lambda *_: (0,) * nd, pipeline_mode=pl.Buffered(1))


def _rms_scale(x):
    return lax.rsqrt(jnp.mean(x * x, axis=-1, keepdims=True) + EPS)


def _lane_iota():
    return lax.broadcasted_iota(jnp.int32, (1, LANES), 1)


def _dot_nt(a, b):
    return lax.dot_general(a, b, (((1,), (1,)), ((), ())), preferred_element_type=F32)


def _rope_tables(cs, half, starts):
    tm = cs.shape[0]
    base = jnp.concatenate([cs, jnp.zeros((tm, LANES - 2 * half), F32)], axis=1)
    shifted = lambda s: base if s % LANES == 0 else pltpu.roll(base, s % LANES, 1)
    lane = _lane_iota()
    cos_t = jnp.ones((tm, LANES), F32)
    sin_a = jnp.zeros((tm, LANES), F32)
    sin_b = jnp.zeros((tm, LANES), F32)
    for s0 in starts:
        first = (lane >= s0) & (lane < s0 + half)
        second = (lane >= s0 + half) & (lane < s0 + 2 * half)
        at_s0 = shifted(s0)
        cos_t = jnp.where(first, at_s0, jnp.where(second, shifted(s0 + half), cos_t))
        sin_a = jnp.where(first, -shifted(s0 - half), sin_a)
        sin_b = jnp.where(second, at_s0, sin_b)
    return cos_t, sin_a, sin_b


def _store_gained(tables, g3_ref, tab_ref):
    for j in range(3):
        tab_ref[j] = tables[j] * g3_ref[j:j + 1, :]


def _rope(x, half, tab_ref, rows=slice(None)):
    return (x * tab_ref[0, rows, :] + pltpu.roll(x, LANES - half, 1) * tab_ref[1, rows, :]
            + pltpu.roll(x, half, 1) * tab_ref[2, rows, :])


def _proj_even_kernel(h_ref, gmix_ref, win_ref, wvbt_ref, gcq_ref, gckv_ref, wuq_ref, wuk_ref, wuvt_ref,
                      gq3_ref, gk_ref, gkr3_ref, gbq_ref, gbk_ref, cs_ref,
                      qa_ref, ka_ref, vat_ref, qb_ref, kb_ref, vbt_ref, tabq_ref, tabk_ref,
                      u_buf, q_buf, kn_buf):
    n_part = u_buf.shape[0]
    rows_part = u_buf.shape[1]
    lo = _lane_iota() < 64

    def project(j):
        rows = slice(j * rows_part, (j + 1) * rows_part)
        x = h_ref[rows, :]
        hn = (x * _rms_scale(x) * gmix_ref[...]).astype(BF16)
        u = jnp.dot(hn, win_ref[...], preferred_element_type=F32)
        u_buf[j] = u
        vbt_ref[:, rows] = _dot_nt(wvbt_ref[...], hn).astype(BF16)
        cq = u[:, 0:A_Q_LORA]
        cqn = (cq * _rms_scale(cq) * gcq_ref[...]).astype(BF16)
        q_buf[j] = jnp.dot(cqn, wuq_ref[...], preferred_element_type=F32)
        ckv = u[:, A_Q_LORA:A_Q_LORA + A_KV_LORA]
        ckvn = (ckv * _rms_scale(ckv) * gckv_ref[...]).astype(BF16)
        kn_buf[j] = jnp.dot(ckvn, wuk_ref[...], preferred_element_type=F32)
        vat_ref[:, rows] = _dot_nt(wuvt_ref[...], ckvn).astype(BF16)

    project(0)
    tables = _rope_tables(cs_ref[...], A_ROPE // 2, (A_NOPE,))
    _store_gained(tables, gq3_ref, tabq_ref)
    _store_gained(tables, gkr3_ref, tabk_ref)

    def pair_norm(blk, g):
        sq = blk * blk
        tot = jnp.sum(sq, axis=-1, keepdims=True)
        s_lo = jnp.sum(jnp.where(lo, sq, 0.0), axis=-1, keepdims=True)
        sc = jnp.where(lo, lax.rsqrt(s_lo * (1.0 / B_DH) + EPS),
                       lax.rsqrt((tot - s_lo) * (1.0 / B_DH) + EPS))
        return (blk * sc * g).astype(BF16)

    for j in range(n_part):
        if j + 1 < n_part:
            project(j + 1)
        rows = slice(j * rows_part, (j + 1) * rows_part)
        kr = u_buf[j, :, EV_KR_OFF:EV_KR_OFF + LANES]
        kr_s = lax.rsqrt(jnp.sum(kr * kr, axis=-1, keepdims=True) * (1.0 / A_ROPE) + EPS)
        krot = _rope(kr, A_ROPE // 2, tabk_ref, rows) * kr_s
        for hd in range(A_HEADS):
            sl = slice(hd * LANES, (hd + 1) * LANES)
            blk = q_buf[j, :, sl]
            sq = blk * blk
            tot = jnp.sum(sq, axis=-1, keepdims=True)
            ssn = jnp.sum(jnp.where(lo, sq, 0.0), axis=-1, keepdims=True)
            sc = jnp.where(lo, lax.rsqrt(ssn * (1.0 / A_NOPE) + EPS),
                           lax.rsqrt((tot - ssn) * (1.0 / A_ROPE) + EPS))
            qa_ref[rows, sl] = (_rope(blk, A_ROPE // 2, tabq_ref, rows) * sc).astype(BF16)
            kb_ = kn_buf[j, :, sl]
            ks = lax.rsqrt(jnp.sum(kb_ * kb_, axis=-1, keepdims=True) * (1.0 / A_NOPE) + EPS)
            ka_ref[rows, sl] = (kb_ * ks * gk_ref[:, sl] + krot).astype(BF16)
        for p in range(B_HEADS // 2):
            sl = slice(p * LANES, (p + 1) * LANES)
            qb_ref[rows, sl] = pair_norm(u_buf[j, :, EV_QB_OFF + p * LANES:EV_QB_OFF + (p + 1) * LANES],
                                         gbq_ref[:, sl])
            kb_ref[rows, sl] = pair_norm(u_buf[j, :, EV_KB_OFF + p * LANES:EV_KB_OFF + (p + 1) * LANES],
                                         gbk_ref[:, sl])


def _proj_even(h, gmix, win, wvbt, gcq, gckv, wuq, wuk, wuvt, gq3, gk, gkr3, gbq, gbk, cs):
    t = h.shape[0]
    tm = TM_PROJ
    row = lambda w: pl.BlockSpec((tm, w), lambda i: (i, 0))
    col = lambda w: pl.BlockSpec((w, tm), lambda i: (0, i))
    rshape = lambda w: jax.ShapeDtypeStruct((t, w), BF16)
    cshape = lambda w: jax.ShapeDtypeStruct((w, t), BF16)
    consts = (gmix, win, wvbt, gcq, gckv, wuq, wuk, wuvt, gq3, gk, gkr3, gbq, gbk)
    return pl.pallas_call(
        _proj_even_kernel,
        grid=(t // tm,),
        in_specs=[row(D_MODEL)] + [_const_spec(c.shape) for c in consts] + [row(cs.shape[1])],
        out_specs=[row(1024), row(1024), col(512), row(512), row(512), col(512)],
        out_shape=[rshape(1024), rshape(1024), cshape(512), rshape(512), rshape(512), cshape(512)],
        scratch_shapes=[pltpu.VMEM((3, tm, LANES), F32)] * 2
        + [pltpu.VMEM((PROJ_PARTS, tm // PROJ_PARTS, w), F32) for w in (win.shape[1], 1024, 1024)],
        compiler_params=_cparams(("parallel",)),
        name="proj_even",
    )(h, *consts, cs)


def _causal_scores(q_ref, k_ref, s_ref, c, *, split):
    tq = TQ_CAUSAL
    q = q_ref[c * tq:(c + 1) * tq, :]
    zero = jnp.zeros_like(q[:, :LANES])
    if split:
        q2 = jnp.concatenate([jnp.concatenate([q[:, :LANES], zero], axis=1),
                              jnp.concatenate([zero, q[:, LANES:]], axis=1)], axis=0)
    else:
        lo = _lane_iota() < 64
        q2 = jnp.concatenate([jnp.where(lo, q, zero), jnp.where(lo, zero, q)], axis=0)
    kchunk = lax.broadcasted_iota(jnp.int32, (tq, 2 * tq), 0) // CHUNK
    qchunk = (lax.broadcasted_iota(jnp.int32, (tq, 2 * tq), 1) % tq) // CHUNK
    n_past = c * tq
    s_d = jnp.where(kchunk <= qchunk, _dot_nt(k_ref[n_past:n_past + tq, :], q2), NEG)
    s_ref[n_past:n_past + tq, :] = s_d
    mx = jnp.max(s_d, axis=0, keepdims=True)
    if c:
        s_p = _dot_nt(k_ref[0:n_past, :], q2)
        s_ref[0:n_past, :] = s_p
        mx = jnp.maximum(mx, jnp.max(s_p, axis=0, keepdims=True))
    return mx


def _causal_values(vt_ref, s_ref, c, mx):
    n_keys = (c + 1) * TQ_CAUSAL
    p = jnp.exp2(s_ref[0:n_keys, :] - mx)
    o = jnp.dot(vt_ref[:, 0:n_keys], p.astype(BF16), preferred_element_type=F32)
    return o * (1.0 / jnp.sum(p, axis=0, keepdims=True))


def _causal_tiles(q_ref, k_ref, vt_ref, s_buf, *, split):
    tq = TQ_CAUSAL
    n = q_ref.shape[0] // tq
    mx = _causal_scores(q_ref, k_ref, s_buf.at[0], 0, split=split)
    for c in range(n):
        mx_next = None
        if c + 1 < n:
            mx_next = _causal_scores(q_ref, k_ref, s_buf.at[(c + 1) % 2], c + 1, split=split)
        o = _causal_values(vt_ref, s_buf.at[c % 2], c, mx)
        yield c, o[:, :tq], o[:, tq:]
        mx = mx_next


def _row_iota(n):
    return lax.broadcasted_iota(jnp.int32, (LANES, n), 0)


def _attn_mla_kernel(q_ref, k_ref, vt_ref, o_ref, s_buf):
    tq = TQ_CAUSAL
    for c, o0, o1 in _causal_tiles(q_ref, k_ref, vt_ref, s_buf, split=True):
        o_ref[c * tq:(c + 1) * tq, :] = jnp.where(_row_iota(tq) < 64, o0, o1).T.astype(o_ref.dtype)


def _attn_diff_kernel(q_ref, k_ref, vt_ref, lq1_ref, lk1_ref, lq2_ref, lk2_ref, gsub_ref, o_ref, s_buf,
                      *, lam_init):
    tq = TQ_CAUSAL
    lam = (jnp.exp(jnp.sum(lq1_ref[...] * lk1_ref[...], axis=-1, keepdims=True))
           - jnp.exp(jnp.sum(lq2_ref[...] * lk2_ref[...], axis=-1, keepdims=True)) + lam_init)
    for c, o0, o1 in _causal_tiles(q_ref, k_ref, vt_ref, s_buf, split=False):
        o = (o0 - lam * o1).T
        o = o * _rms_scale(o) * gsub_ref[...] * (1.0 - lam_init)
        o_ref[c * tq:(c + 1) * tq, :] = o.astype(o_ref.dtype)


def _attn_causal(kernel, q, k, vt, extras, *, batch, seq, n_blocks, qk_width):
    in_specs = [pl.BlockSpec((seq, qk_width), lambda b, p: (b, p)),
                pl.BlockSpec((seq, qk_width), lambda b, p: (b, p)),
                pl.BlockSpec((LANES, seq), lambda b, p: (p, b))]
    in_specs += [_const_spec(e.shape) for e in extras]
    return pl.pallas_call(
        kernel,
        grid=(batch, n_blocks),
        in_specs=in_specs,
        out_specs=pl.BlockSpec((seq, LANES), lambda b, p: (b, p)),
        out_shape=jax.ShapeDtypeStruct((batch * seq, n_blocks * LANES), BF16),
        scratch_shapes=[pltpu.VMEM((2, seq, 2 * TQ_CAUSAL), F32)],
        compiler_params=_cparams(("parallel", "parallel")),
        name=kernel.func.__name__ if isinstance(kernel, functools.partial) else kernel.__name__,
    )(q, k, vt, *extras)


def _attn_band_kernel(q_ref, k_ref, vt_ref, bias_ref, o_ref, s_buf):
    tq = TQ_BAND
    n = q_ref.shape[0] // tq
    lo = _lane_iota() < 64

    def window(c):
        end = (c + 1) * tq
        return max(0, end - BAND_KEYS), end

    def scores(c):
        start, end = window(c)
        q = q_ref[c * tq:(c + 1) * tq, :]
        zero = jnp.zeros_like(q)
        q2 = jnp.concatenate([jnp.where(lo, q, zero), jnp.where(lo, zero, q)], axis=0)
        s = _dot_nt(k_ref[start:end, :], q2) + bias_ref[BAND_KEYS - (end - start):, :]
        s_buf[c % 2, 0:end - start, :] = s
        return jnp.max(s, axis=0, keepdims=True)

    mx = scores(0)
    for c in range(n):
        mx_next = scores(c + 1) if c + 1 < n else None
        start, end = window(c)
        p = jnp.exp2(s_buf[c % 2, 0:end - start, :] - mx)
        o = jnp.dot(vt_ref[:, start:end], p.astype(BF16), preferred_element_type=F32)
        o = o * (1.0 / jnp.sum(p, axis=0, keepdims=True))
        o_ref[c * tq:(c + 1) * tq, :] = jnp.where(_row_iota(tq) < 64, o[:, :tq], o[:, tq:]).T.astype(o_ref.dtype)
        mx = mx_next


def _attn_band(q, k, vt, bias, *, batch, seq):
    npair = B_HEADS // 2
    return pl.pallas_call(
        _attn_band_kernel,
        grid=(batch, npair),
        in_specs=[pl.BlockSpec((seq, LANES), lambda b, p: (b, p)),
                  pl.BlockSpec((seq, LANES), lambda b, p: (b, p)),
                  pl.BlockSpec((LANES, seq), lambda b, p: (p, b)),
                  pl.BlockSpec((None, BAND_KEYS, 2 * TQ_BAND), lambda b, p: (p, 0, 0))],
        out_specs=pl.BlockSpec((seq, LANES), lambda b, p: (b, p)),
        out_shape=jax.ShapeDtypeStruct((batch * seq, npair * LANES), BF16),
        scratch_shapes=[pltpu.VMEM((2, BAND_KEYS, 2 * TQ_BAND), F32)],
        compiler_params=_cparams(("parallel", "parallel")),
        name="attn_band",
    )(q, k, vt, bias)


def _silu(g):
    return g / (1.0 + jnp.exp(-g))


def _out_ffn_kernel(h_ref, ya_ref, yb_ref, woa_ref, wob_ref, gff_ref, wg_ref, wu_ref, wd_ref, o_ref):
    h1 = (h_ref[...] + jnp.dot(ya_ref[...], woa_ref[...], preferred_element_type=F32)
          + jnp.dot(yb_ref[...], wob_ref[...], preferred_element_type=F32))
    xn = (h1 * _rms_scale(h1) * gff_ref[...]).astype(BF16)
    g = jnp.dot(xn, wg_ref[...], preferred_element_type=F32)
    u = jnp.dot(xn, wu_ref[...], preferred_element_type=F32)
    a = (_silu(g) * u).astype(BF16)
    o_ref[...] = h1 + jnp.dot(a, wd_ref[...], preferred_element_type=F32)


def _out_ffn(h, ya, yb, woa, wob, gff, wg, wu, wd):
    t = h.shape[0]
    tm = TM_PROJ
    row = lambda w: pl.BlockSpec((tm, w), lambda i: (i, 0))
    return pl.pallas_call(
        _out_ffn_kernel,
        grid=(t // tm,),
        in_specs=[row(D_MODEL), row(ya.shape[1]), row(yb.shape[1])]
        + [_const_spec(w.shape) for w in (woa, wob, gff, wg, wu, wd)],
        out_specs=row(D_MODEL),
        out_shape=jax.ShapeDtypeStruct((t, D_MODEL), F32),
        compiler_params=_cparams(("parallel",)),
        name="out_ffn",
    )(h, ya, yb, woa, wob, gff, wg, wu, wd)


def _proj_odd_kernel(h_ref, gmix_ref, w_ref, wvt_ref, gq3_ref, gk3_ref, cs_ref, q_ref, k_ref, vt_ref,
                     tabq_ref, tabk_ref, u_buf):
    n_part = u_buf.shape[0]
    rows_part = u_buf.shape[1]
    lo = _lane_iota() < 64
    width = C_HEADS * 2 * C_DH

    def project(j):
        rows = slice(j * rows_part, (j + 1) * rows_part)
        x = h_ref[rows, :]
        hn = (x * _rms_scale(x) * gmix_ref[...]).astype(BF16)
        u_buf[j] = jnp.dot(hn, w_ref[...], preferred_element_type=F32)
        vt_ref[:, rows] = _dot_nt(wvt_ref[...], hn).astype(BF16)

    project(0)
    tables = _rope_tables(cs_ref[...], C_ROPE // 2, (0, C_DH))
    _store_gained(tables, gq3_ref, tabq_ref)
    _store_gained(tables, gk3_ref, tabk_ref)

    def norm_rope(blk, tab_ref, rows):
        sq = blk * blk
        tot = jnp.sum(sq, axis=-1, keepdims=True)
        s_lo = jnp.sum(jnp.where(lo, sq, 0.0), axis=-1, keepdims=True)
        sc = jnp.where(lo, lax.rsqrt(s_lo * (1.0 / C_DH) + EPS),
                       lax.rsqrt((tot - s_lo) * (1.0 / C_DH) + EPS))
        return (_rope(blk, C_ROPE // 2, tab_ref, rows) * sc).astype(BF16)

    for j in range(n_part):
        if j + 1 < n_part:
            project(j + 1)
        rows = slice(j * rows_part, (j + 1) * rows_part)
        for hd in range(C_HEADS):
            sl = slice(hd * LANES, (hd + 1) * LANES)
            q_ref[rows, sl] = norm_rope(u_buf[j, :, sl], tabq_ref, rows)
            k_ref[rows, sl] = norm_rope(u_buf[j, :, width + hd * LANES:width + (hd + 1) * LANES],
                                        tabk_ref, rows)


def _proj_odd(h, gmix, w, wvt, gq3, gk3, cs):
    t = h.shape[0]
    tm = TM_PROJ
    row = lambda wd: pl.BlockSpec((tm, wd), lambda i: (i, 0))
    width = C_HEADS * 2 * C_DH
    consts = (gmix, w, wvt, gq3, gk3)
    return pl.pallas_call(
        _proj_odd_kernel,
        grid=(t // tm,),
        in_specs=[row(D_MODEL)] + [_const_spec(c.shape) for c in consts] + [row(cs.shape[1])],
        out_specs=[row(width), row(width), pl.BlockSpec((width, tm), lambda i: (0, i))],
        out_shape=[jax.ShapeDtypeStruct((t, width), BF16), jax.ShapeDtypeStruct((t, width), BF16),
                   jax.ShapeDtypeStruct((width, t), BF16)],
        scratch_shapes=[pltpu.VMEM((3, tm, LANES), F32)] * 2
        + [pltpu.VMEM((PROJ_PARTS, tm // PROJ_PARTS, 2 * width), F32)],
        compiler_params=_cparams(("parallel",)),
        name="proj_odd",
    )(h, *consts, cs)


def _tiled_rows(ref, start, n):
    return jnp.concatenate([ref[pl.ds(start * ROW_TILES + s, n, stride=ROW_TILES), :]
                            for s in range(ROW_TILES)], axis=1)


def _store_tiled(ref, start, val, accumulate=False):
    for s in range(ROW_TILES):
        idx = pl.ds(start * ROW_TILES + s, val.shape[0], stride=ROW_TILES)
        piece = val[:, s * LANES:(s + 1) * LANES]
        ref[idx, :] = ref[idx, :] + piece if accumulate else piece


def _out_router_kernel(h_ref, y_ref, wo_ref, gff_ref, wr_ref, h1_ref, idx_ref, gate_ref):
    h1 = h_ref[...] + jnp.dot(y_ref[...], wo_ref[...], preferred_element_type=F32)
    _store_tiled(h1_ref, 0, h1)
    xn = h1 * _rms_scale(h1) * gff_ref[...]
    logits = [jnp.sum(xn * wr_ref[e:e + 1, :], axis=-1, keepdims=True) for e in range(N_EXPERTS)]
    m1 = functools.reduce(jnp.maximum, logits)
    idx1 = functools.reduce(jnp.minimum,
                            [jnp.where(logits[e] == m1, e, N_EXPERTS) for e in range(N_EXPERTS)])
    rest = [jnp.where(idx1 == e, -jnp.inf, logits[e]) for e in range(N_EXPERTS)]
    m2 = functools.reduce(jnp.maximum, rest)
    idx2 = functools.reduce(jnp.minimum,
                            [jnp.where(rest[e] == m2, e, N_EXPERTS) for e in range(N_EXPERTS)])
    e2 = jnp.exp(m2 - m1)
    g1 = 1.0 / (1.0 + e2)
    g2 = e2 * g1
    lane = _lane_iota()
    idx_ref[...] = jnp.where(lane == 0, idx1, jnp.where(lane == 1, idx2, 0))
    gate_ref[...] = jnp.where(lane == 0, g1, jnp.where(lane == 1, g2, 0.0))


def _out_router(h, y, wo, gff, wr):
    t = h.shape[0]
    tm = TM_PROJ
    row = lambda w: pl.BlockSpec((tm, w), lambda i: (i, 0))
    return pl.pallas_call(
        _out_router_kernel,
        grid=(t // tm,),
        in_specs=[row(D_MODEL), row(y.shape[1]), _const_spec(wo.shape), _const_spec(gff.shape),
                  _const_spec(wr.shape)],
        out_specs=[pl.BlockSpec((tm * ROW_TILES, LANES), lambda i: (i, 0)), row(LANES), row(LANES)],
        out_shape=[jax.ShapeDtypeStruct((t * ROW_TILES, LANES), F32),
                   jax.ShapeDtypeStruct((t, LANES), jnp.int32), jax.ShapeDtypeStruct((t, LANES), F32)],
        compiler_params=_cparams(("parallel",)),
        name="out_router",
    )(h, y, wo, gff, wr)


def _row_copy(src_ref, src_row, dst_ref, dst_row, sem):
    tile = lambda r: pl.ds(pl.multiple_of(r * ROW_TILES, ROW_TILES), ROW_TILES)
    return pltpu.make_async_copy(src_ref.at[tile(src_row)], dst_ref.at[tile(dst_row)], sem)


def _dispatch_kernel(row_ref, x_ref, xs_ref, sem):
    tm = x_ref.shape[0] // ROW_TILES
    base = pl.program_id(0) * tm

    def issue(r, carry):
        for k in range(2):
            _row_copy(x_ref, r, xs_ref, row_ref[2 * (base + r) + k], sem).start(priority=k)
        return carry

    lax.fori_loop(0, tm, issue, 0, unroll=DMA_UNROLL)
    for k in range(2):
        pltpu.make_async_copy(x_ref, xs_ref.at[pl.ds(0, tm * ROW_TILES)], sem).wait()


def _dispatch(row_of_pair, x):
    t = x.shape[0] // ROW_TILES
    tm = TM_DISPATCH
    grid_spec = pltpu.PrefetchScalarGridSpec(
        num_scalar_prefetch=1,
        grid=(t // tm,),
        in_specs=[pl.BlockSpec((tm * ROW_TILES, LANES), lambda i, rows: (i, 0))],
        out_specs=pl.BlockSpec(memory_space=pl.ANY),
        scratch_shapes=[pltpu.SemaphoreType.DMA(())],
    )
    return pl.pallas_call(
        _dispatch_kernel,
        grid_spec=grid_spec,
        out_shape=jax.ShapeDtypeStruct((2 * t * ROW_TILES, LANES), x.dtype),
        compiler_params=_cparams(("arbitrary",)),
        name="moe_dispatch",
    )(row_of_pair, x)


def _moe_kernel(vt_ref, ve_ref, lo_ref, hi_ref, first_ref, x_ref, gff_ref, wg_ref, wu_ref, wd_ref, o_ref,
                xb_ref, acc_ref):
    v = pl.program_id(0)
    c = pl.program_id(1)
    tr = xb_ref.shape[0]

    lo = lo_ref[v]
    hi = hi_ref[v]
    base = vt_ref[v] * tr
    whole_tile = (lo == base) & (hi == base + tr)

    @pl.when((c == 0) & (first_ref[v] == 1) & jnp.logical_not(whole_tile))
    def _():
        o_ref[...] = jnp.zeros_like(o_ref)

    @pl.when(c == 0)
    def _():
        x = _tiled_rows(x_ref, 0, tr)
        xb_ref[...] = (x * _rms_scale(x) * gff_ref[...]).astype(BF16)
        acc_ref[...] = jnp.zeros_like(acc_ref)

    def expert_rows(off, n, whole):
        rows = slice(off, off + n)
        x = xb_ref[rows, :]
        g = jnp.dot(x, wg_ref[0].astype(BF16), preferred_element_type=F32)
        u = jnp.dot(x, wu_ref[0].astype(BF16), preferred_element_type=F32)
        a = (_silu(g) * u).astype(BF16)
        y = jnp.dot(a, wd_ref[0].astype(BF16), preferred_element_type=F32)
        if not whole:
            r = base + off + lax.broadcasted_iota(jnp.int32, (n, 1), 0)
            y = jnp.where((r >= lo) & (r < hi), y, 0.0)
        acc_ref[rows, :] += y

    @pl.when(whole_tile)
    def _():
        expert_rows(0, tr, True)

    for s in range(tr // MOE_SUB):
        @pl.when(jnp.logical_not(whole_tile) & (base + s * MOE_SUB < hi) & (base + (s + 1) * MOE_SUB > lo))
        def _(s=s):
            expert_rows(s * MOE_SUB, MOE_SUB, False)

    last = c == pl.num_programs(1) - 1

    @pl.when(last & whole_tile)
    def _():
        _store_tiled(o_ref, 0, acc_ref[...])

    @pl.when(last & jnp.logical_not(whole_tile) & (lo < hi))
    def _():
        _store_tiled(o_ref, 0, acc_ref[...], accumulate=True)


def _moe(visits, xs, gff, wg, wu, wd):
    r = xs.shape[0] // ROW_TILES
    tr, fc = TR_MOE, FC_MOE
    nc = FF_EXPERT // fc
    n_visits = r // tr + N_EXPERTS - 1

    def chunk(v, c, lo, hi):
        return jnp.where(lo[v] < hi[v], c, nc - 1)

    tile = lambda v, c, vt, ve, lo, hi, first: (vt[v], 0)
    wcol = lambda v, c, vt, ve, lo, hi, first: (ve[v], 0, chunk(v, c, lo, hi))
    wrow = lambda v, c, vt, ve, lo, hi, first: (ve[v], chunk(v, c, lo, hi), 0)
    grid_spec = pltpu.PrefetchScalarGridSpec(
        num_scalar_prefetch=5,
        grid=(n_visits, nc),
        in_specs=[pl.BlockSpec((tr * ROW_TILES, LANES), tile),
                  pl.BlockSpec((1, D_MODEL), lambda v, c, vt, ve, lo, hi, first: (0, 0)),
                  pl.BlockSpec((1, D_MODEL, fc), wcol),
                  pl.BlockSpec((1, D_MODEL, fc), wcol),
                  pl.BlockSpec((1, fc, D_MODEL), wrow)],
        out_specs=pl.BlockSpec((tr * ROW_TILES, LANES), tile),
        scratch_shapes=[pltpu.VMEM((tr, D_MODEL), BF16), pltpu.VMEM((tr, D_MODEL), F32)],
    )
    return pl.pallas_call(
        _moe_kernel,
        grid_spec=grid_spec,
        out_shape=jax.ShapeDtypeStruct((r * ROW_TILES, LANES), F32),
        compiler_params=_cparams(("arbitrary", "arbitrary")),
        name="moe_grouped",
    )(*visits, xs, gff, wg, wu, wd)


def _combine_kernel(row_ref, h_ref, gate_ref, ys_ref, o_ref, buf, sem):
    tm = gate_ref.shape[0]
    base = pl.program_id(0) * tm

    def issue(r, carry):
        for k in range(2):
            _row_copy(ys_ref, row_ref[2 * (base + r) + k], buf, k * tm + r, sem).start(priority=k)
        return carry

    lax.fori_loop(0, tm, issue, 0, unroll=DMA_UNROLL)
    pltpu.make_async_copy(ys_ref.at[pl.ds(0, 2 * tm * ROW_TILES)], buf, sem).wait()
    gate = gate_ref[...]
    o_ref[...] = (_tiled_rows(h_ref, 0, tm) + gate[:, 0:1] * _tiled_rows(buf, 0, tm)
                  + gate[:, 1:2] * _tiled_rows(buf, tm, tm))


def _combine(row_of_pair, h1, gates, ys):
    t = gates.shape[0]
    d = D_MODEL
    tm = TM_DISPATCH
    grid_spec = pltpu.PrefetchScalarGridSpec(
        num_scalar_prefetch=1,
        grid=(t // tm,),
        in_specs=[pl.BlockSpec((tm * ROW_TILES, LANES), lambda i, rows: (i, 0)),
                  pl.BlockSpec((tm, LANES), lambda i, rows: (i, 0)),
                  pl.BlockSpec(memory_space=pl.ANY)],
        out_specs=pl.BlockSpec((tm, d), lambda i, rows: (i, 0)),
        scratch_shapes=[pltpu.VMEM((2 * tm * ROW_TILES, LANES), F32), pltpu.SemaphoreType.DMA(())],
    )
    return pl.pallas_call(
        _combine_kernel,
        grid_spec=grid_spec,
        out_shape=jax.ShapeDtypeStruct((t, d), F32),
        compiler_params=_cparams(("arbitrary",)),
        name="moe_combine",
    )(row_of_pair, h1, gates, ys)


def _route_plan(idx):
    n_pairs = idx.size
    tr = TR_MOE
    n_tiles = n_pairs // tr
    n_visits = n_tiles + N_EXPERTS - 1
    e_flat = idx.reshape(-1)
    onehot = (e_flat[:, None] == jnp.arange(N_EXPERTS, dtype=jnp.int32)[None, :]).astype(jnp.int32)
    csum = jnp.cumsum(onehot, axis=0)
    counts = csum[-1]
    ends = jnp.cumsum(counts)
    starts = ends - counts
    pos = jnp.sum((csum - onehot) * onehot, axis=1)
    row_of_pair = (starts[e_flat] + pos).astype(jnp.int32)
    nv = jnp.where(counts > 0, (ends - 1) // tr - starts // tr + 1, 0)
    vend = jnp.cumsum(nv)
    vid = jnp.arange(n_visits, dtype=jnp.int32)
    live = vid < vend[-1]
    ve = jnp.minimum(jnp.sum((vid[:, None] >= vend[None, :]).astype(jnp.int32), axis=1), N_EXPERTS - 1)
    vt = jnp.where(live, starts[ve] // tr + vid - (vend - nv)[ve], n_tiles - 1)
    lo = jnp.where(live, jnp.maximum(starts[ve], vt * tr), 0)
    hi = jnp.where(live, jnp.minimum(ends[ve], (vt + 1) * tr), 0)
    last_e = jnp.max(jnp.where(counts > 0, jnp.arange(N_EXPERTS, dtype=jnp.int32), 0))
    ve = jnp.where(live, ve, last_e)
    first = (live & (lo == vt * tr)).astype(jnp.int32)
    i32 = lambda a: a.astype(jnp.int32)
    return row_of_pair, (i32(vt), i32(ve), i32(lo), i32(hi), first)


def _cos_sin(positions, dim, theta):
    inv = 1.0 / (theta ** (jnp.arange(0, dim, 2, dtype=F32) / dim))
    ang = positions.astype(F32).reshape(-1, 1) * inv
    return jnp.concatenate([jnp.cos(ang), jnp.sin(ang)], axis=1)


def _gain3(g, half):
    return jnp.stack([g, jnp.roll(g, -half), jnp.roll(g, half)]).astype(F32)


def _band_bias(rel_table):
    nh = rel_table.shape[0]
    nk, nq = BAND_KEYS, TQ_BAND
    length = BAND_PROFILE
    n_lo = nk - 1 - B_LEFT_CHUNKS * CHUNK - B_REL_CLIP
    n_hi = length - n_lo - rel_table.shape[1]
    prof = jnp.concatenate([jnp.broadcast_to(rel_table[:, :1], (nh, n_lo)), rel_table,
                            jnp.broadcast_to(rel_table[:, -1:], (nh, n_hi))], axis=1)
    return pl.pallas_call(
        _band_bias_kernel,
        grid=(nh,),
        in_specs=[pl.BlockSpec((None, 1, length), lambda h: (h, 0, 0))],
        out_specs=pl.BlockSpec((None, nk, nq), lambda h: (h // 2, 0, h % 2)),
        out_shape=jax.ShapeDtypeStruct((nh // 2, nk, 2 * nq), F32),
        compiler_params=_cparams(("parallel",)),
        name="band_bias",
    )(prof.astype(F32).reshape(nh, 1, length))


def _band_bias_kernel(p_ref, o_ref):
    nk, nq = o_ref.shape
    length = p_ref.shape[1]
    x = jnp.broadcast_to(p_ref[...] * LOG2E, (8, length))
    sub = lax.broadcasted_iota(jnp.int32, (8, nq), 0)
    qchunk = lax.broadcasted_iota(jnp.int32, (8, nq), 1) // CHUNK

    def body(jb, carry):
        rolled = pltpu.roll(x, jb * 8 + (length - (nk - 1)), 1, stride=1, stride_axis=0)
        cdiff = qchunk + B_LEFT_CHUNKS - (jb * 8 + sub) // CHUNK
        allowed = (cdiff >= 0) & (cdiff <= B_LEFT_CHUNKS)
        o_ref[pl.ds(pl.multiple_of(jb * 8, 8), 8), :] = jnp.where(allowed, rolled[:, :nq], NEG)
        return carry

    lax.fori_loop(0, nk // 8, body, 0, unroll=8)


def _pad_last(a, width):
    return jnp.pad(a, [(0, 0)] * (a.ndim - 1) + [(0, width - a.shape[-1])])


def kernel(x, positions, ev_norm_mix, ev_w_in, ev_g_cq, ev_g_ckv, ev_w_uq, ev_w_ukv, ev_a_qnorm, ev_a_knorm, ev_b_qnorm, ev_b_knorm, ev_b_rel_bias, ev_w_out, ev_norm_ff, ev_ff_gate, ev_ff_up, ev_ff_down, od_norm_mix, od_w_qkv, od_c_qnorm, od_c_knorm, od_lam_q1, od_lam_k1, od_lam_q2, od_lam_k2, od_c_subnorm, od_w_out, od_norm_ff, od_router, od_ex_gate, od_ex_up, od_ex_down):
    batch, seq, d = x.shape
    t = batch * seq
    depth = ev_w_in.shape[0] + od_w_qkv.shape[0]
    h = x.reshape(t, d)
    row = lambda a: a.reshape(1, -1).astype(F32)
    cs_a = _cos_sin(positions, A_ROPE, A_THETA)
    cs_c = _cos_sin(positions, C_ROPE, ROPE_THETA)

    for layer in range(depth):
        i = layer // 2
        if layer % 2 == 0:
            w = ev_w_in[i]
            o1, o2, o3 = A_Q_LORA, A_Q_LORA + A_KV_LORA, A_Q_LORA + A_KV_LORA + A_ROPE
            zc = lambda n: jnp.zeros((d, n), F32)
            bw = B_HEADS * B_DH
            win = jnp.concatenate([w[:, :o2], zc(A_NOPE), w[:, o2:o3], zc(LANES - A_NOPE - A_ROPE),
                                   w[:, o3:o3 + 2 * bw]], 1).astype(BF16)
            wvbt = w[:, o3 + 2 * bw:].T.astype(BF16)
            wuq = _pad_last(ev_w_uq[i], LANES).reshape(A_Q_LORA, A_HEADS * LANES).astype(BF16)
            wuk = _pad_last(ev_w_ukv[i][:, :, :A_NOPE], LANES).reshape(A_KV_LORA, A_HEADS * LANES).astype(BF16)
            wuvt = ev_w_ukv[i][:, :, A_NOPE:].reshape(A_KV_LORA, A_HEADS * A_V).T.astype(BF16)
            sc_a = (A_NOPE + A_ROPE) ** -0.5 * LOG2E
            gq3 = _gain3(_pad_last(ev_a_qnorm[i], LANES) * sc_a, A_ROPE // 2)
            gk = jnp.tile(_pad_last(ev_a_knorm[i][:A_NOPE], LANES), A_HEADS).reshape(1, -1)
            gkr3 = _gain3(jnp.concatenate([jnp.zeros((A_NOPE,), F32), ev_a_knorm[i][A_NOPE:],
                                           jnp.zeros((LANES - A_NOPE - A_ROPE,), F32)]), A_ROPE // 2)
            gbq = jnp.tile(ev_b_qnorm[i], B_HEADS).reshape(1, -1) * (B_DH ** -0.5 * LOG2E)
            gbk = jnp.tile(ev_b_knorm[i], B_HEADS).reshape(1, -1)
            qa, ka, vat, qb, kb, vbt = _proj_even(
                h, row(ev_norm_mix[i]), win, wvbt, row(ev_g_cq[i]), row(ev_g_ckv[i]), wuq, wuk, wuvt,
                gq3, gk, gkr3, gbq, gbk, cs_a)
            ya = _attn_causal(_attn_mla_kernel, qa, ka, vat, (), batch=batch, seq=seq,
                              n_blocks=A_HEADS // 2, qk_width=2 * LANES)
            yb = _attn_band(qb, kb, vbt, _band_bias(ev_b_rel_bias[i]), batch=batch, seq=seq)
            wo = ev_w_out[i].astype(BF16)
            na = A_HEADS * A_V
            padf = lambda a: _pad_last(a, FF_DENSE_PAD).astype(BF16)
            wd = jnp.pad(ev_ff_down[i], ((0, FF_DENSE_PAD - FF_DENSE), (0, 0))).astype(BF16)
            h = _out_ffn(h, ya, yb, wo[:na], wo[na:], row(ev_norm_ff[i]),
                         padf(ev_ff_gate[i]), padf(ev_ff_up[i]), wd)
        else:
            lam_init = 0.8 - 0.6 * math.exp(-0.3 * layer)
            gq3 = _gain3(jnp.tile(od_c_qnorm[i], 2) * (C_DH ** -0.5 * LOG2E), C_ROPE // 2)
            gk3 = _gain3(jnp.tile(od_c_knorm[i], 2), C_ROPE // 2)
            c_mix = C_HEADS * 2 * C_DH
            wqkv = od_w_qkv[i]
            q, k, v = _proj_odd(h, row(od_norm_mix[i]), wqkv[:, :2 * c_mix].astype(BF16),
                                wqkv[:, 2 * c_mix:].T.astype(BF16), gq3, gk3, cs_c)
            extras = (row(od_lam_q1[i]), row(od_lam_k1[i]), row(od_lam_q2[i]), row(od_lam_k2[i]),
                      row(od_c_subnorm[i]))
            yc = _attn_causal(functools.partial(_attn_diff_kernel, lam_init=lam_init), q, k, v, extras,
                              batch=batch, seq=seq, n_blocks=C_HEADS, qk_width=LANES)
            h1, idx, gates = _out_router(h, yc, od_w_out[i].astype(BF16), row(od_norm_ff[i]),
                                         od_router[i].T.astype(F32))
            row_of_pair, visits = _route_plan(idx[:, :2])
            xs = _dispatch(row_of_pair, h1)
            ys = _moe(visits, xs, row(od_norm_ff[i]), od_ex_gate[i], od_ex_up[i], od_ex_down[i])
            h = _combine(row_of_pair, h1, gates, ys)
    return h.reshape(batch, seq, d)
```

```python
import functools
import math

import numpy as np
import jax
import jax.numpy as jnp
from jax import lax
from jax.experimental import pallas as pl
from jax.experimental.pallas import tpu as pltpu

F32 = jnp.float32
BF16 = jnp.bfloat16

D_MODEL = 1024
CHUNK = 64
EPS = 1e-6
LOG2E = 1.4426950408889634
NEG = -1e30

A_HEADS = 8
A_NOPE = 64
A_ROPE = 32
A_V = 64
A_Q_LORA = 256
A_KV_LORA = 128
A_THETA = 10000.0
B_HEADS = 8
B_DH = 64
B_LEFT_CHUNKS = 8
B_REL_CLIP = 128
C_HEADS = 8
C_DH = 64
C_ROPE = C_DH // 4
ROPE_THETA = 500000.0
FF_DENSE = 2752
N_EXPERTS = 8
FF_EXPERT = 3584

LANES = 128
ROW_TILES = D_MODEL // LANES
FF_DENSE_PAD = 2816
VMEM_LIMIT = 56 * 1024 * 1024

EV_KR_OFF = A_Q_LORA + A_KV_LORA
EV_QB_OFF = EV_KR_OFF + LANES
EV_KB_OFF = EV_QB_OFF + B_HEADS * B_DH

TM_PROJ = 512
PROJ_PARTS = 2
TQ_CAUSAL = 256
TQ_BAND = 256
BAND_KEYS = TQ_BAND + B_LEFT_CHUNKS * CHUNK
BAND_PROFILE = 1024
TM_DISPATCH = 1024
DMA_UNROLL = 16
TR_MOE = 1024
MOE_SUB = 256
FC_MOE = 512


def _cparams(sem):
    return pltpu.CompilerParams(dimension_semantics=sem, vmem_limit_bytes=VMEM_LIMIT)


def _const_spec(shape):
    nd = len(shape)
    return pl.BlockSpec(shape, lambda *_: (0,) * nd, pipeline_mode=pl.Buffered(1))


def _rms_scale(x):
    return lax.rsqrt(jnp.mean(x * x, axis=-1, keepdims=True) + EPS)


def _lane_iota():
    return lax.broadcasted_iota(jnp.int32, (1, LANES), 1)


def _dot_nt(a, b):
    return lax.dot_general(a, b, (((1,), (1,)), ((), ())), preferred_element_type=F32)


def _rope_tables(cs, half, starts):
    tm = cs.shape[0]
    base = jnp.concatenate([cs, jnp.zeros((tm, LANES - 2 * half), F32)], axis=1)
    shifted = lambda s: base if s % LANES == 0 else pltpu.roll(base, s % LANES, 1)
    lane = _lane_iota()
    cos_t = jnp.ones((tm, LANES), F32)
    sin_a = jnp.zeros((tm, LANES), F32)
    sin_b = jnp.zeros((tm, LANES), F32)
    for s0 in starts:
        first = (lane >= s0) & (lane < s0 + half)
        second = (lane >= s0 + half) & (lane < s0 + 2 * half)
        at_s0 = shifted(s0)
        cos_t = jnp.where(first, at_s0, jnp.where(second, shifted(s0 + half), cos_t))
        sin_a = jnp.where(first, -shifted(s0 - half), sin_a)
        sin_b = jnp.where(second, at_s0, sin_b)
    return cos_t, sin_a, sin_b


def _store_gained(tables, g3_ref, tab_ref):
    for j in range(3):
        tab_ref[j] = tables[j] * g3_ref[j:j + 1, :]


def _rope(x, half, tab_ref, rows=slice(None)):
    return (x * tab_ref[0, rows, :] + pltpu.roll(x, LANES - half, 1) * tab_ref[1, rows, :]
            + pltpu.roll(x, half, 1) * tab_ref[2, rows, :])


def _proj_even_kernel(h_ref, gmix_ref, win_ref, wvbt_ref, gcq_ref, gckv_ref, wuq_ref, wuk_ref, wuvt_ref,
                      gq3_ref, gk_ref, gkr3_ref, gbq_ref, gbk_ref, cs_ref,
                      qa_ref, ka_ref, vat_ref, qb_ref, kb_ref, vbt_ref, tabq_ref, tabk_ref,
                      u_buf, q_buf, kn_buf):
    n_part = u_buf.shape[0]
    rows_part = u_buf.shape[1]
    lo = _lane_iota() < 64

    def project(j):
        rows = slice(j * rows_part, (j + 1) * rows_part)
        x = h_ref[rows, :]
        hn = (x * _rms_scale(x) * gmix_ref[...]).astype(BF16)
        u = jnp.dot(hn, win_ref[...], preferred_element_type=F32)
        u_buf[j] = u
        vbt_ref[:, rows] = _dot_nt(wvbt_ref[...], hn).astype(BF16)
        cq = u[:, 0:A_Q_LORA]
        cqn = (cq * _rms_scale(cq) * gcq_ref[...]).astype(BF16)
        q_buf[j] = jnp.dot(cqn, wuq_ref[...], preferred_element_type=F32)
        ckv = u[:, A_Q_LORA:A_Q_LORA + A_KV_LORA]
        ckvn = (ckv * _rms_scale(ckv) * gckv_ref[...]).astype(BF16)
        kn_buf[j] = jnp.dot(ckvn, wuk_ref[...], preferred_element_type=F32)
        vat_ref[:, rows] = _dot_nt(wuvt_ref[...], ckvn).astype(BF16)

    project(0)
    tables = _rope_tables(cs_ref[...], A_ROPE // 2, (A_NOPE,))
    _store_gained(tables, gq3_ref, tabq_ref)
    _store_gained(tables, gkr3_ref, tabk_ref)

    def pair_norm(blk, g):
        sq = blk * blk
        tot = jnp.sum(sq, axis=-1, keepdims=True)
        s_lo = jnp.sum(jnp.where(lo, sq, 0.0), axis=-1, keepdims=True)
        sc = jnp.where(lo, lax.rsqrt(s_lo * (1.0 / B_DH) + EPS),
                       lax.rsqrt((tot - s_lo) * (1.0 / B_DH) + EPS))
        return (blk * sc * g).astype(BF16)

    for j in range(n_part):
        if j + 1 < n_part:
            project(j + 1)
        rows = slice(j * rows_part, (j + 1) * rows_part)
        kr = u_buf[j, :, EV_KR_OFF:EV_KR_OFF + LANES]
        kr_s = lax.rsqrt(jnp.sum(kr * kr, axis=-1, keepdims=True) * (1.0 / A_ROPE) + EPS)
        krot = _rope(kr, A_ROPE // 2, tabk_ref, rows) * kr_s
        for hd in range(A_HEADS):
            sl = slice(hd * LANES, (hd + 1) * LANES)
            blk = q_buf[j, :, sl]
            sq = blk * blk
            tot = jnp.sum(sq, axis=-1, keepdims=True)
            ssn = jnp.sum(jnp.where(lo, sq, 0.0), axis=-1, keepdims=True)
            sc = jnp.where(lo, lax.rsqrt(ssn * (1.0 / A_NOPE) + EPS),
                           lax.rsqrt((tot - ssn) * (1.0 / A_ROPE) + EPS))
            qa_ref[rows, sl] = (_rope(blk, A_ROPE // 2, tabq_ref, rows) * sc).astype(BF16)
            kb_ = kn_buf[j, :, sl]
            ks = lax.rsqrt(jnp.sum(kb_ * kb_, axis=-1, keepdims=True) * (1.0 / A_NOPE) + EPS)
            ka_ref[rows, sl] = (kb_ * ks * gk_ref[:, sl] + krot).astype(BF16)
        for p in range(B_HEADS // 2):
            sl = slice(p * LANES, (p + 1) * LANES)
            qb_ref[rows, sl] = pair_norm(u_buf[j, :, EV_QB_OFF + p * LANES:EV_QB_OFF + (p + 1) * LANES],
                                         gbq_ref[:, sl])
            kb_ref[rows, sl] = pair_norm(u_buf[j, :, EV_KB_OFF + p * LANES:EV_KB_OFF + (p + 1) * LANES],
                                         gbk_ref[:, sl])


def _proj_even(h, gmix, win, wvbt, gcq, gckv, wuq, wuk, wuvt, gq3, gk, gkr3, gbq, gbk, cs):
    t = h.shape[0]
    tm = TM_PROJ
    row = lambda w: pl.BlockSpec((tm, w), lambda i: (i, 0))
    col = lambda w: pl.BlockSpec((w, tm), lambda i: (0, i))
    rshape = lambda w: jax.ShapeDtypeStruct((t, w), BF16)
    cshape = lambda w: jax.ShapeDtypeStruct((w, t), BF16)
    consts = (gmix, win, wvbt, gcq, gckv, wuq, wuk, wuvt, gq3, gk, gkr3, gbq, gbk)
    return pl.pallas_call(
        _proj_even_kernel,
        grid=(t // tm,),
        in_specs=[row(D_MODEL)] + [_const_spec(c.shape) for c in consts] + [row(cs.shape[1])],
        out_specs=[row(1024), row(1024), col(512), row(512), row(512), col(512)],
        out_shape=[rshape(1024), rshape(1024), cshape(512), rshape(512), rshape(512), cshape(512)],
        scratch_shapes=[pltpu.VMEM((3, tm, LANES), F32)] * 2
        + [pltpu.VMEM((PROJ_PARTS, tm // PROJ_PARTS, w), F32) for w in (win.shape[1], 1024, 1024)],
        compiler_params=_cparams(("parallel",)),
        name="proj_even",
    )(h, *consts, cs)


def _causal_scores(q_ref, k_ref, s_ref, c, *, split):
    tq = TQ_CAUSAL
    q = q_ref[c * tq:(c + 1) * tq, :]
    zero = jnp.zeros_like(q[:, :LANES])
    if split:
        q2 = jnp.concatenate([jnp.concatenate([q[:, :LANES], zero], axis=1),
                              jnp.concatenate([zero, q[:, LANES:]], axis=1)], axis=0)
    else:
        lo = _lane_iota() < 64
        q2 = jnp.concatenate([jnp.where(lo, q, zero), jnp.where(lo, zero, q)], axis=0)
    kchunk = lax.broadcasted_iota(jnp.int32, (tq, 2 * tq), 0) // CHUNK
    qchunk = (lax.broadcasted_iota(jnp.int32, (tq, 2 * tq), 1) % tq) // CHUNK
    n_past = c * tq
    s_d = jnp.where(kchunk <= qchunk, _dot_nt(k_ref[n_past:n_past + tq, :], q2), NEG)
    s_ref[n_past:n_past + tq, :] = s_d
    mx = jnp.max(s_d, axis=0, keepdims=True)
    if c:
        s_p = _dot_nt(k_ref[0:n_past, :], q2)
        s_ref[0:n_past, :] = s_p
        mx = jnp.maximum(mx, jnp.max(s_p, axis=0, keepdims=True))
    return mx


def _causal_values(vt_ref, s_ref, c, mx):
    n_keys = (c + 1) * TQ_CAUSAL
    p = jnp.exp2(s_ref[0:n_keys, :] - mx)
    o = jnp.dot(vt_ref[:, 0:n_keys], p.astype(BF16), preferred_element_type=F32)
    return o * (1.0 / jnp.sum(p, axis=0, keepdims=True))


def _causal_tiles(q_ref, k_ref, vt_ref, s_buf, *, split):
    tq = TQ_CAUSAL
    n = q_ref.shape[0] // tq
    mx = _causal_scores(q_ref, k_ref, s_buf.at[0], 0, split=split)
    for c in range(n):
        mx_next = None
        if c + 1 < n:
            mx_next = _causal_scores(q_ref, k_ref, s_buf.at[(c + 1) % 2], c + 1, split=split)
        o = _causal_values(vt_ref, s_buf.at[c % 2], c, mx)
        yield c, o[:, :tq], o[:, tq:]
        mx = mx_next


def _row_iota(n):
    return lax.broadcasted_iota(jnp.int32, (LANES, n), 0)


def _attn_mla_kernel(q_ref, k_ref, vt_ref, o_ref, s_buf):
    tq = TQ_CAUSAL
    for c, o0, o1 in _causal_tiles(q_ref, k_ref, vt_ref, s_buf, split=True):
        o_ref[c * tq:(c + 1) * tq, :] = jnp.where(_row_iota(tq) < 64, o0, o1).T.astype(o_ref.dtype)


def _attn_diff_kernel(q_ref, k_ref, vt_ref, lq1_ref, lk1_ref, lq2_ref, lk2_ref, gsub_ref, o_ref, s_buf,
                      *, lam_init):
    tq = TQ_CAUSAL
    lam = (jnp.exp(jnp.sum(lq1_ref[...] * lk1_ref[...], axis=-1, keepdims=True))
           - jnp.exp(jnp.sum(lq2_ref[...] * lk2_ref[...], axis=-1, keepdims=True)) + lam_init)
    for c, o0, o1 in _causal_tiles(q_ref, k_ref, vt_ref, s_buf, split=False):
        o = (o0 - lam * o1).T
        o = o * _rms_scale(o) * gsub_ref[...] * (1.0 - lam_init)
        o_ref[c * tq:(c + 1) * tq, :] = o.astype(o_ref.dtype)


def _attn_causal(kernel, q, k, vt, extras, *, batch, seq, n_blocks, qk_width):
    in_specs = [pl.BlockSpec((seq, qk_width), lambda b, p: (b, p)),
                pl.BlockSpec((seq, qk_width), lambda b, p: (b, p)),
                pl.BlockSpec((LANES, seq), lambda b, p: (p, b))]
    in_specs += [_const_spec(e.shape) for e in extras]
    return pl.pallas_call(
        kernel,
        grid=(batch, n_blocks),
        in_specs=in_specs,
        out_specs=pl.BlockSpec((seq, LANES), lambda b, p: (b, p)),
        out_shape=jax.ShapeDtypeStruct((batch * seq, n_blocks * LANES), BF16),
        scratch_shapes=[pltpu.VMEM((2, seq, 2 * TQ_CAUSAL), F32)],
        compiler_params=_cparams(("parallel", "parallel")),
        name=kernel.func.__name__ if isinstance(kernel, functools.partial) else kernel.__name__,
    )(q, k, vt, *extras)


def _attn_band_kernel(q_ref, k_ref, vt_ref, bias_ref, o_ref, s_buf):
    tq = TQ_BAND
    n = q_ref.shape[0] // tq
    lo = _lane_iota() < 64

    def window(c):
        end = (c + 1) * tq
        return max(0, end - BAND_KEYS), end

    def scores(c):
        start, end = window(c)
        q = q_ref[c * tq:(c + 1) * tq, :]
        zero = jnp.zeros_like(q)
        q2 = jnp.concatenate([jnp.where(lo, q, zero), jnp.where(lo, zero, q)], axis=0)
        s = _dot_nt(k_ref[start:end, :], q2) + bias_ref[BAND_KEYS - (end - start):, :]
        s_buf[c % 2, 0:end - start, :] = s
        return jnp.max(s, axis=0, keepdims=True)

    mx = scores(0)
    for c in range(n):
        mx_next = scores(c + 1) if c + 1 < n else None
        start, end = window(c)
        p = jnp.exp2(s_buf[c % 2, 0:end - start, :] - mx)
        o = jnp.dot(vt_ref[:, start:end], p.astype(BF16), preferred_element_type=F32)
        o = o * (1.0 / jnp.sum(p, axis=0, keepdims=True))
        o_ref[c * tq:(c + 1) * tq, :] = jnp.where(_row_iota(tq) < 64, o[:, :tq], o[:, tq:]).T.astype(o_ref.dtype)
        mx = mx_next


def _attn_band(q, k, vt, bias, *, batch, seq):
    npair = B_HEADS // 2
    return pl.pallas_call(
        _attn_band_kernel,
        grid=(batch, npair),
        in_specs=[pl.BlockSpec((seq, LANES), lambda b, p: (b, p)),
                  pl.BlockSpec((seq, LANES), lambda b, p: (b, p)),
                  pl.BlockSpec((LANES, seq), lambda b, p: (p, b)),
                  pl.BlockSpec((None, BAND_KEYS, 2 * TQ_BAND), lambda b, p: (p, 0, 0))],
        out_specs=pl.BlockSpec((seq, LANES), lambda b, p: (b, p)),
        out_shape=jax.ShapeDtypeStruct((batch * seq, npair * LANES), BF16),
        scratch_shapes=[pltpu.VMEM((2, BAND_KEYS, 2 * TQ_BAND), F32)],
        compiler_params=_cparams(("parallel", "parallel")),
        name="attn_band",
    )(q, k, vt, bias)


def _silu(g):
    return g / (1.0 + jnp.exp(-g))


def _out_ffn_kernel(h_ref, ya_ref, yb_ref, woa_ref, wob_ref, gff_ref, wg_ref, wu_ref, wd_ref, o_ref):
    h1 = (h_ref[...] + jnp.dot(ya_ref[...], woa_ref[...], preferred_element_type=F32)
          + jnp.dot(yb_ref[...], wob_ref[...], preferred_element_type=F32))
    xn = (h1 * _rms_scale(h1) * gff_ref[...]).astype(BF16)
    g = jnp.dot(xn, wg_ref[...], preferred_element_type=F32)
    u = jnp.dot(xn, wu_ref[...], preferred_element_type=F32)
    a = (_silu(g) * u).astype(BF16)
    o_ref[...] = h1 + jnp.dot(a, wd_ref[...], preferred_element_type=F32)


def _out_ffn(h, ya, yb, woa, wob, gff, wg, wu, wd):
    t = h.shape[0]
    tm = TM_PROJ
    row = lambda w: pl.BlockSpec((tm, w), lambda i: (i, 0))
    return pl.pallas_call(
        _out_ffn_kernel,
        grid=(t // tm,),
        in_specs=[row(D_MODEL), row(ya.shape[1]), row(yb.shape[1])]
        + [_const_spec(w.shape) for w in (woa, wob, gff, wg, wu, wd)],
        out_specs=row(D_MODEL),
        out_shape=jax.ShapeDtypeStruct((t, D_MODEL), F32),
        compiler_params=_cparams(("parallel",)),
        name="out_ffn",
    )(h, ya, yb, woa, wob, gff, wg, wu, wd)


def _proj_odd_kernel(h_ref, gmix_ref, w_ref, wvt_ref, gq3_ref, gk3_ref, cs_ref, q_ref, k_ref, vt_ref,
                     tabq_ref, tabk_ref, u_buf):
    n_part = u_buf.shape[0]
    rows_part = u_buf.shape[1]
    lo = _lane_iota() < 64
    width = C_HEADS * 2 * C_DH

    def project(j):
        rows = slice(j * rows_part, (j + 1) * rows_part)
        x = h_ref[rows, :]
        hn = (x * _rms_scale(x) * gmix_ref[...]).astype(BF16)
        u_buf[j] = jnp.dot(hn, w_ref[...], preferred_element_type=F32)
        vt_ref[:, rows] = _dot_nt(wvt_ref[...], hn).astype(BF16)

    project(0)
    tables = _rope_tables(cs_ref[...], C_ROPE // 2, (0, C_DH))
    _store_gained(tables, gq3_ref, tabq_ref)
    _store_gained(tables, gk3_ref, tabk_ref)

    def norm_rope(blk, tab_ref, rows):
        sq = blk * blk
        tot = jnp.sum(sq, axis=-1, keepdims=True)
        s_lo = jnp.sum(jnp.where(lo, sq, 0.0), axis=-1, keepdims=True)
        sc = jnp.where(lo, lax.rsqrt(s_lo * (1.0 / C_DH) + EPS),
                       lax.rsqrt((tot - s_lo) * (1.0 / C_DH) + EPS))
        return (_rope(blk, C_ROPE // 2, tab_ref, rows) * sc).astype(BF16)

    for j in range(n_part):
        if j + 1 < n_part:
            project(j + 1)
        rows = slice(j * rows_part, (j + 1) * rows_part)
        for hd in range(C_HEADS):
            sl = slice(hd * LANES, (hd + 1) * LANES)
            q_ref[rows, sl] = norm_rope(u_buf[j, :, sl], tabq_ref, rows)
            k_ref[rows, sl] = norm_rope(u_buf[j, :, width + hd * LANES:width + (hd + 1) * LANES],
                                        tabk_ref, rows)


def _proj_odd(h, gmix, w, wvt, gq3, gk3, cs):
    t = h.shape[0]
    tm = TM_PROJ
    row = lambda wd: pl.BlockSpec((tm, wd), lambda i: (i, 0))
    width = C_HEADS * 2 * C_DH
    consts = (gmix, w, wvt, gq3, gk3)
    return pl.pallas_call(
        _proj_odd_kernel,
        grid=(t // tm,),
        in_specs=[row(D_MODEL)] + [_const_spec(c.shape) for c in consts] + [row(cs.shape[1])],
        out_specs=[row(width), row(width), pl.BlockSpec((width, tm), lambda i: (0, i))],
        out_shape=[jax.ShapeDtypeStruct((t, width), BF16), jax.ShapeDtypeStruct((t, width), BF16),
                   jax.ShapeDtypeStruct((width, t), BF16)],
        scratch_shapes=[pltpu.VMEM((3, tm, LANES), F32)] * 2
        + [pltpu.VMEM((PROJ_PARTS, tm // PROJ_PARTS, 2 * width), F32)],
        compiler_params=_cparams(("parallel",)),
        name="proj_odd",
    )(h, *consts, cs)


def _tiled_rows(ref, start, n):
    return jnp.concatenate([ref[pl.ds(start * ROW_TILES + s, n, stride=ROW_TILES), :]
                            for s in range(ROW_TILES)], axis=1)


def _store_tiled(ref, start, val, accumulate=False):
    for s in range(ROW_TILES):
        idx = pl.ds(start * ROW_TILES + s, val.shape[0], stride=ROW_TILES)
        piece = val[:, s * LANES:(s + 1) * LANES]
        ref[idx, :] = ref[idx, :] + piece if accumulate else piece


def _out_router_kernel(h_ref, y_ref, wo_ref, gff_ref, wr_ref, h1_ref, idx_ref, gate_ref, h1_buf):
    n_part = h1_buf.shape[0]
    rows_part = h1_buf.shape[1]
    lane = _lane_iota()

    def project(j):
        rows = slice(j * rows_part, (j + 1) * rows_part)
        h1 = h_ref[rows, :] + jnp.dot(y_ref[rows, :], wo_ref[...], preferred_element_type=F32)
        h1_buf[j] = h1
        _store_tiled(h1_ref, j * rows_part, h1)

    project(0)
    for j in range(n_part):
        if j + 1 < n_part:
            project(j + 1)
        rows = slice(j * rows_part, (j + 1) * rows_part)
        h1 = h1_buf[j]
        xn = h1 * _rms_scale(h1) * gff_ref[...]
        logits = [jnp.sum(xn * wr_ref[e:e + 1, :], axis=-1, keepdims=True) for e in range(N_EXPERTS)]
        m1 = functools.reduce(jnp.maximum, logits)
        idx1 = functools.reduce(jnp.minimum,
                                [jnp.where(logits[e] == m1, e, N_EXPERTS) for e in range(N_EXPERTS)])
        rest = [jnp.where(idx1 == e, -jnp.inf, logits[e]) for e in range(N_EXPERTS)]
        m2 = functools.reduce(jnp.maximum, rest)
        idx2 = functools.reduce(jnp.minimum,
                                [jnp.where(rest[e] == m2, e, N_EXPERTS) for e in range(N_EXPERTS)])
        e2 = jnp.exp(m2 - m1)
        g1 = 1.0 / (1.0 + e2)
        g2 = e2 * g1
        idx_ref[rows, :] = jnp.where(lane == 0, idx1, jnp.where(lane == 1, idx2, 0))
        gate_ref[rows, :] = jnp.where(lane == 0, g1, jnp.where(lane == 1, g2, 0.0))


def _out_router(h, y, wo, gff, wr):
    t = h.shape[0]
    tm = TM_PROJ
    row = lambda w: pl.BlockSpec((tm, w), lambda i: (i, 0))
    return pl.pallas_call(
        _out_router_kernel,
        grid=(t // tm,),
        in_specs=[row(D_MODEL), row(y.shape[1]), _const_spec(wo.shape), _const_spec(gff.shape),
                  _const_spec(wr.shape)],
        out_specs=[pl.BlockSpec((tm * ROW_TILES, LANES), lambda i: (i, 0)), row(LANES), row(LANES)],
        out_shape=[jax.ShapeDtypeStruct((t * ROW_TILES, LANES), F32),
                   jax.ShapeDtypeStruct((t, LANES), jnp.int32), jax.ShapeDtypeStruct((t, LANES), F32)],
        scratch_shapes=[pltpu.VMEM((PROJ_PARTS, tm // PROJ_PARTS, D_MODEL), F32)],
        compiler_params=_cparams(("parallel",)),
        name="out_router",
    )(h, y, wo, gff, wr)


def _row_copy(src_ref, src_row, dst_ref, dst_row, sem):
    tile = lambda r: pl.ds(pl.multiple_of(r * ROW_TILES, ROW_TILES), ROW_TILES)
    return pltpu.make_async_copy(src_ref.at[tile(src_row)], dst_ref.at[tile(dst_row)], sem)


def _dispatch_kernel(row_ref, x_ref, xs_ref, sem):
    tm = x_ref.shape[0] // ROW_TILES
    base = pl.program_id(0) * tm

    def issue(r, carry):
        for k in range(2):
            _row_copy(x_ref, r, xs_ref, row_ref[2 * (base + r) + k], sem).start(priority=k)
        return carry

    lax.fori_loop(0, tm, issue, 0, unroll=DMA_UNROLL)
    for k in range(2):
        pltpu.make_async_copy(x_ref, xs_ref.at[pl.ds(0, tm * ROW_TILES)], sem).wait()


def _dispatch(row_of_pair, x):
    t = x.shape[0] // ROW_TILES
    tm = TM_DISPATCH
    grid_spec = pltpu.PrefetchScalarGridSpec(
        num_scalar_prefetch=1,
        grid=(t // tm,),
        in_specs=[pl.BlockSpec((tm * ROW_TILES, LANES), lambda i, rows: (i, 0))],
        out_specs=pl.BlockSpec(memory_space=pl.ANY),
        scratch_shapes=[pltpu.SemaphoreType.DMA(())],
    )
    return pl.pallas_call(
        _dispatch_kernel,
        grid_spec=grid_spec,
        out_shape=jax.ShapeDtypeStruct((2 * t * ROW_TILES, LANES), x.dtype),
        compiler_params=_cparams(("arbitrary",)),
        name="moe_dispatch",
    )(row_of_pair, x)


def _moe_kernel(vt_ref, ve_ref, lo_ref, hi_ref, first_ref, x_ref, gff_ref, wg_ref, wu_ref, wd_ref, o_ref,
                xb_ref, acc_ref):
    v = pl.program_id(0)
    c = pl.program_id(1)
    tr = xb_ref.shape[0]

    lo = lo_ref[v]
    hi = hi_ref[v]
    base = vt_ref[v] * tr
    whole_tile = (lo == base) & (hi == base + tr)

    @pl.when((c == 0) & (first_ref[v] == 1) & jnp.logical_not(whole_tile))
    def _():
        o_ref[...] = jnp.zeros_like(o_ref)

    @pl.when(c == 0)
    def _():
        x = _tiled_rows(x_ref, 0, tr)
        xb_ref[...] = (x * _rms_scale(x) * gff_ref[...]).astype(BF16)
        acc_ref[...] = jnp.zeros_like(acc_ref)

    def expert_rows(off, n, whole):
        rows = slice(off, off + n)
        x = xb_ref[rows, :]
        g = jnp.dot(x, wg_ref[0].astype(BF16), preferred_element_type=F32)
        u = jnp.dot(x, wu_ref[0].astype(BF16), preferred_element_type=F32)
        a = (_silu(g) * u).astype(BF16)
        y = jnp.dot(a, wd_ref[0].astype(BF16), preferred_element_type=F32)
        if not whole:
            r = base + off + lax.broadcasted_iota(jnp.int32, (n, 1), 0)
            y = jnp.where((r >= lo) & (r < hi), y, 0.0)
        acc_ref[rows, :] += y

    @pl.when(whole_tile)
    def _():
        expert_rows(0, tr, True)

    for s in range(tr // MOE_SUB):
        @pl.when(jnp.logical_not(whole_tile) & (base + s * MOE_SUB < hi) & (base + (s + 1) * MOE_SUB > lo))
        def _(s=s):
            expert_rows(s * MOE_SUB, MOE_SUB, False)

    last = c == pl.num_programs(1) - 1

    @pl.when(last & whole_tile)
    def _():
        _store_tiled(o_ref, 0, acc_ref[...])

    @pl.when(last & jnp.logical_not(whole_tile) & (lo < hi))
    def _():
        _store_tiled(o_ref, 0, acc_ref[...], accumulate=True)


def _moe(visits, xs, gff, wg, wu, wd):
    r = xs.shape[0] // ROW_TILES
    tr, fc = TR_MOE, FC_MOE
    nc = FF_EXPERT // fc
    n_visits = r // tr + N_EXPERTS - 1

    def chunk(v, c, lo, hi):
        return jnp.where(lo[v] < hi[v], c, nc - 1)

    tile = lambda v, c, vt, ve, lo, hi, first: (vt[v], 0)
    wcol = lambda v, c, vt, ve, lo, hi, first: (ve[v], 0, chunk(v, c, lo, hi))
    wrow = lambda v, c, vt, ve, lo, hi, first: (ve[v], chunk(v, c, lo, hi), 0)
    grid_spec = pltpu.PrefetchScalarGridSpec(
        num_scalar_prefetch=5,
        grid=(n_visits, nc),
        in_specs=[pl.BlockSpec((tr * ROW_TILES, LANES), tile),
                  pl.BlockSpec((1, D_MODEL), lambda v, c, vt, ve, lo, hi, first: (0, 0)),
                  pl.BlockSpec((1, D_MODEL, fc), wcol),
                  pl.BlockSpec((1, D_MODEL, fc), wcol),
                  pl.BlockSpec((1, fc, D_MODEL), wrow)],
        out_specs=pl.BlockSpec((tr * ROW_TILES, LANES), tile),
        scratch_shapes=[pltpu.VMEM((tr, D_MODEL), BF16), pltpu.VMEM((tr, D_MODEL), F32)],
    )
    return pl.pallas_call(
        _moe_kernel,
        grid_spec=grid_spec,
        out_shape=jax.ShapeDtypeStruct((r * ROW_TILES, LANES), F32),
        compiler_params=_cparams(("arbitrary", "arbitrary")),
        name="moe_grouped",
    )(*visits, xs, gff, wg, wu, wd)


def _combine_kernel(row_ref, h_ref, gate_ref, ys_ref, o_ref, buf, sem):
    tm = gate_ref.shape[0]
    base = pl.program_id(0) * tm

    def issue(r, carry):
        for k in range(2):
            _row_copy(ys_ref, row_ref[2 * (base + r) + k], buf, k * tm + r, sem).start(priority=k)
        return carry

    lax.fori_loop(0, tm, issue, 0, unroll=DMA_UNROLL)
    pltpu.make_async_copy(ys_ref.at[pl.ds(0, 2 * tm * ROW_TILES)], buf, sem).wait()
    gate = gate_ref[...]
    o_ref[...] = (_tiled_rows(h_ref, 0, tm) + gate[:, 0:1] * _tiled_rows(buf, 0, tm)
                  + gate[:, 1:2] * _tiled_rows(buf, tm, tm))


def _combine(row_of_pair, h1, gates, ys):
    t = gates.shape[0]
    d = D_MODEL
    tm = TM_DISPATCH
    grid_spec = pltpu.PrefetchScalarGridSpec(
        num_scalar_prefetch=1,
        grid=(t // tm,),
        in_specs=[pl.BlockSpec((tm * ROW_TILES, LANES), lambda i, rows: (i, 0)),
                  pl.BlockSpec((tm, LANES), lambda i, rows: (i, 0)),
                  pl.BlockSpec(memory_space=pl.ANY)],
        out_specs=pl.BlockSpec((tm, d), lambda i, rows: (i, 0)),
        scratch_shapes=[pltpu.VMEM((2 * tm * ROW_TILES, LANES), F32), pltpu.SemaphoreType.DMA(())],
    )
    return pl.pallas_call(
        _combine_kernel,
        grid_spec=grid_spec,
        out_shape=jax.ShapeDtypeStruct((t, d), F32),
        compiler_params=_cparams(("arbitrary",)),
        name="moe_combine",
    )(row_of_pair, h1, gates, ys)


def _route_plan(idx):
    n_pairs = idx.size
    tr = TR_MOE
    n_tiles = n_pairs // tr
    n_visits = n_tiles + N_EXPERTS - 1
    e_flat = idx.reshape(-1)
    onehot = (e_flat[:, None] == jnp.arange(N_EXPERTS, dtype=jnp.int32)[None, :]).astype(jnp.int32)
    csum = jnp.cumsum(onehot, axis=0)
    counts = csum[-1]
    ends = jnp.cumsum(counts)
    starts = ends - counts
    pos = jnp.sum((csum - onehot) * onehot, axis=1)
    row_of_pair = (starts[e_flat] + pos).astype(jnp.int32)
    nv = jnp.where(counts > 0, (ends - 1) // tr - starts // tr + 1, 0)
    vend = jnp.cumsum(nv)
    vid = jnp.arange(n_visits, dtype=jnp.int32)
    live = vid < vend[-1]
    ve = jnp.minimum(jnp.sum((vid[:, None] >= vend[None, :]).astype(jnp.int32), axis=1), N_EXPERTS - 1)
    vt = jnp.where(live, starts[ve] // tr + vid - (vend - nv)[ve], n_tiles - 1)
    lo = jnp.where(live, jnp.maximum(starts[ve], vt * tr), 0)
    hi = jnp.where(live, jnp.minimum(ends[ve], (vt + 1) * tr), 0)
    last_e = jnp.max(jnp.where(counts > 0, jnp.arange(N_EXPERTS, dtype=jnp.int32), 0))
    ve = jnp.where(live, ve, last_e)
    first = (live & (lo == vt * tr)).astype(jnp.int32)
    i32 = lambda a: a.astype(jnp.int32)
    return row_of_pair, (i32(vt), i32(ve), i32(lo), i32(hi), first)


def _cos_sin(positions, dim, theta):
    inv = 1.0 / (theta ** (jnp.arange(0, dim, 2, dtype=F32) / dim))
    ang = positions.astype(F32).reshape(-1, 1) * inv
    return jnp.concatenate([jnp.cos(ang), jnp.sin(ang)], axis=1)


def _gain3(g, half):
    return jnp.stack([g, jnp.roll(g, -half), jnp.roll(g, half)]).astype(F32)


def _band_bias(rel_table):
    nh = rel_table.shape[0]
    nk, nq = BAND_KEYS, TQ_BAND
    length = BAND_PROFILE
    n_lo = nk - 1 - B_LEFT_CHUNKS * CHUNK - B_REL_CLIP
    n_hi = length - n_lo - rel_table.shape[1]
    prof = jnp.concatenate([jnp.broadcast_to(rel_table[:, :1], (nh, n_lo)), rel_table,
                            jnp.broadcast_to(rel_table[:, -1:], (nh, n_hi))], axis=1)
    return pl.pallas_call(
        _band_bias_kernel,
        grid=(nh,),
        in_specs=[pl.BlockSpec((None, 1, length), lambda h: (h, 0, 0))],
        out_specs=pl.BlockSpec((None, nk, nq), lambda h: (h // 2, 0, h % 2)),
        out_shape=jax.ShapeDtypeStruct((nh // 2, nk, 2 * nq), F32),
        compiler_params=_cparams(("parallel",)),
        name="band_bias",
    )(prof.astype(F32).reshape(nh, 1, length))


def _band_bias_kernel(p_ref, o_ref):
    nk, nq = o_ref.shape
    length = p_ref.shape[1]
    x = jnp.broadcast_to(p_ref[...] * LOG2E, (8, length))
    sub = lax.broadcasted_iota(jnp.int32, (8, nq), 0)
    qchunk = lax.broadcasted_iota(jnp.int32, (8, nq), 1) // CHUNK

    def body(jb, carry):
        rolled = pltpu.roll(x, jb * 8 + (length - (nk - 1)), 1, stride=1, stride_axis=0)
        cdiff = qchunk + B_LEFT_CHUNKS - (jb * 8 + sub) // CHUNK
        allowed = (cdiff >= 0) & (cdiff <= B_LEFT_CHUNKS)
        o_ref[pl.ds(pl.multiple_of(jb * 8, 8), 8), :] = jnp.where(allowed, rolled[:, :nq], NEG)
        return carry

    lax.fori_loop(0, nk // 8, body, 0, unroll=8)


def _pad_last(a, width):
    return jnp.pad(a, [(0, 0)] * (a.ndim - 1) + [(0, width - a.shape[-1])])


def kernel(x, positions, ev_norm_mix, ev_w_in, ev_g_cq, ev_g_ckv, ev_w_uq, ev_w_ukv, ev_a_qnorm, ev_a_knorm, ev_b_qnorm, ev_b_knorm, ev_b_rel_bias, ev_w_out, ev_norm_ff, ev_ff_gate, ev_ff_up, ev_ff_down, od_norm_mix, od_w_qkv, od_c_qnorm, od_c_knorm, od_lam_q1, od_lam_k1, od_lam_q2, od_lam_k2, od_c_subnorm, od_w_out, od_norm_ff, od_router, od_ex_gate, od_ex_up, od_ex_down):
    batch, seq, d = x.shape
    t = batch * seq
    depth = ev_w_in.shape[0] + od_w_qkv.shape[0]
    h = x.reshape(t, d)
    row = lambda a: a.reshape(1, -1).astype(F32)
    cs_a = _cos_sin(positions, A_ROPE, A_THETA)
    cs_c = _cos_sin(positions, C_ROPE, ROPE_THETA)

    for layer in range(depth):
        i = layer // 2
        if layer % 2 == 0:
            w = ev_w_in[i]
            o1, o2, o3 = A_Q_LORA, A_Q_LORA + A_KV_LORA, A_Q_LORA + A_KV_LORA + A_ROPE
            zc = lambda n: jnp.zeros((d, n), F32)
            bw = B_HEADS * B_DH
            win = jnp.concatenate([w[:, :o2], zc(A_NOPE), w[:, o2:o3], zc(LANES - A_NOPE - A_ROPE),
                                   w[:, o3:o3 + 2 * bw]], 1).astype(BF16)
            wvbt = w[:, o3 + 2 * bw:].T.astype(BF16)
            wuq = _pad_last(ev_w_uq[i], LANES).reshape(A_Q_LORA, A_HEADS * LANES).astype(BF16)
            wuk = _pad_last(ev_w_ukv[i][:, :, :A_NOPE], LANES).reshape(A_KV_LORA, A_HEADS * LANES).astype(BF16)
            wuvt = ev_w_ukv[i][:, :, A_NOPE:].reshape(A_KV_LORA, A_HEADS * A_V).T.astype(BF16)
            sc_a = (A_NOPE + A_ROPE) ** -0.5 * LOG2E
            gq3 = _gain3(_pad_last(ev_a_qnorm[i], LANES) * sc_a, A_ROPE // 2)
            gk = jnp.tile(_pad_last(ev_a_knorm[i][:A_NOPE], LANES), A_HEADS).reshape(1, -1)
            gkr3 = _gain3(jnp.concatenate([jnp.zeros((A_NOPE,), F32), ev_a_knorm[i][A_NOPE:],
                                           jnp.zeros((LANES - A_NOPE - A_ROPE,), F32)]), A_ROPE // 2)
            gbq = jnp.tile(ev_b_qnorm[i], B_HEADS).reshape(1, -1) * (B_DH ** -0.5 * LOG2E)
            gbk = jnp.tile(ev_b_knorm[i], B_HEADS).reshape(1, -1)
            qa, ka, vat, qb, kb, vbt = _proj_even(
                h, row(ev_norm_mix[i]), win, wvbt, row(ev_g_cq[i]), row(ev_g_ckv[i]), wuq, wuk, wuvt,
                gq3, gk, gkr3, gbq, gbk, cs_a)
            ya = _attn_causal(_attn_mla_kernel, qa, ka, vat, (), batch=batch, seq=seq,
                              n_blocks=A_HEADS // 2, qk_width=2 * LANES)
            yb = _attn_band(qb, kb, vbt, _band_bias(ev_b_rel_bias[i]), batch=batch, seq=seq)
            wo = ev_w_out[i].astype(BF16)
            na = A_HEADS * A_V
            padf = lambda a: _pad_last(a, FF_DENSE_PAD).astype(BF16)
            wd = jnp.pad(ev_ff_down[i], ((0, FF_DENSE_PAD - FF_DENSE), (0, 0))).astype(BF16)
            h = _out_ffn(h, ya, yb, wo[:na], wo[na:], row(ev_norm_ff[i]),
                         padf(ev_ff_gate[i]), padf(ev_ff_up[i]), wd)
        else:
            lam_init = 0.8 - 0.6 * math.exp(-0.3 * layer)
            gq3 = _gain3(jnp.tile(od_c_qnorm[i], 2) * (C_DH ** -0.5 * LOG2E), C_ROPE // 2)
            gk3 = _gain3(jnp.tile(od_c_knorm[i], 2), C_ROPE // 2)
            c_mix = C_HEADS * 2 * C_DH
            wqkv = od_w_qkv[i]
            q, k, v = _proj_odd(h, row(od_norm_mix[i]), wqkv[:, :2 * c_mix].astype(BF16),
                                wqkv[:, 2 * c_mix:].T.astype(BF16), gq3, gk3, cs_c)
            extras = (row(od_lam_q1[i]), row(od_lam_k1[i]), row(od_lam_q2[i]), row(od_lam_k2[i]),
                      row(od_c_subnorm[i]))
            yc = _attn_causal(functools.partial(_attn_diff_kernel, lam_init=lam_init), q, k, v, extras,
                              batch=batch, seq=seq, n_blocks=C_HEADS, qk_width=LANES)
            h1, idx, gates = _out_router(h, yc, od_w_out[i].astype(BF16), row(od_norm_ff[i]),
                                         od_router[i].T.astype(F32))
            row_of_pair, visits = _route_plan(idx[:, :2])
            xs = _dispatch(row_of_pair, h1)
            ys = _moe(visits, xs, row(od_norm_ff[i]), od_ex_gate[i], od_ex_up[i], od_ex_down[i])
            h = _combine(row_of_pair, h1, gates, ys)
    return h.reshape(batch, seq, d)
```
